```python
import math
import jax, jax.numpy as jnp
from jax import lax
import numpy as np

D_MODEL = 1024
BATCH = 1
SEQ = 16384
DEPTH = 4

N_MIXERS = 4
EPS = 1e-6
Q_BLOCK = 128
D_FF = 4 * D_MODEL

DSA_HEADS = 8
DSA_HEAD_DIM = D_MODEL // DSA_HEADS
IDX_HEADS = 8
IDX_DIM = 64
TOPK_MAX = 256
DSA_IN = 3 * D_MODEL + IDX_HEADS * IDX_DIM + IDX_DIM + IDX_HEADS

CONV_WIDTH = 31

MLA_HEADS = 16
MLA_Q_LORA = 384
MLA_KV_LORA = 256
MLA_NOPE = 64
MLA_ROPE = 32
MLA_V = 64
ROPE_THETA = 10000.0
MLA_IN = MLA_Q_LORA + MLA_KV_LORA + MLA_ROPE

GDN_HEADS = 8
GDN_DK = 128
GDN_DV = 128
GDN_CONV = 4
GDN_CHUNK = 64
GDN_HK = GDN_HEADS * GDN_DK
GDN_HV = GDN_HEADS * GDN_DV
GDN_IN = 2 * GDN_HK + 2 * GDN_HV + 2 * GDN_HEADS

kernel_name = 'hybrid_dsa_conformer_mla_gdn_trunk'


def rmsnorm(x, g):
    xf = x.astype(jnp.float32)
    y = xf * lax.rsqrt(jnp.mean(xf * xf, axis=-1, keepdims=True) + EPS)
    return (y * g.astype(jnp.float32)).astype(x.dtype)


def layernorm(x, g, b):
    xf = x.astype(jnp.float32)
    mu = jnp.mean(xf, axis=-1, keepdims=True)
    var = jnp.mean(jnp.square(xf - mu), axis=-1, keepdims=True)
    y = (xf - mu) * lax.rsqrt(var + EPS) * g.astype(jnp.float32) + b.astype(jnp.float32)
    return y.astype(x.dtype)


def l2norm(x):
    return x * lax.rsqrt(jnp.sum(x * x, axis=-1, keepdims=True) + EPS)


def causal_depthwise_conv(x, w):
    width = w.shape[0]
    xp = jnp.pad(x, ((0, 0), (width - 1, 0), (0, 0)))
    return lax.conv_general_dilated(xp, w[:, None, :].astype(x.dtype), window_strides=(1,), padding='VALID',
                                    dimension_numbers=('NWC', 'WIO', 'NWC'), feature_group_count=x.shape[-1])


def rope_tables(length, dim):
    pos = jnp.arange(length, dtype=jnp.float32)
    inv_freq = ROPE_THETA ** (-jnp.arange(0, dim, 2, dtype=jnp.float32) / dim)
    ang = pos[:, None] * inv_freq[None, :]
    return jnp.cos(ang), jnp.sin(ang)


def apply_rope(x, cos, sin):
    xf = x.astype(jnp.float32)
    half = x.shape[-1] // 2
    x1, x2 = xf[..., :half], xf[..., half:]
    return jnp.concatenate([x1 * cos - x2 * sin, x2 * cos + x1 * sin], axis=-1).astype(x.dtype)


def blocked_queries(fn, *qs):
    b, length = qs[0].shape[0], qs[0].shape[1]
    nb = length // Q_BLOCK
    blks = [jnp.moveaxis(q.reshape(b, nb, Q_BLOCK, *q.shape[2:]), 1, 0) for q in qs]
    t0 = jnp.arange(nb, dtype=jnp.int32) * Q_BLOCK
    out = lax.map(lambda a: fn(*a), (t0, *blks))
    out = jnp.moveaxis(out, 0, 1)
    return out.reshape(b, length, *out.shape[3:])


def dsa_mixer(h, w_in, idx_k_g, idx_k_b, w_out):
    b, length, _ = h.shape
    topk = min(TOPK_MAX, length // 4)
    proj = h @ w_in
    o1 = D_MODEL; o2 = 2 * D_MODEL; o3 = 3 * D_MODEL
    o4 = o3 + IDX_HEADS * IDX_DIM; o5 = o4 + IDX_DIM
    q = proj[..., :o1].reshape(b, length, DSA_HEADS, DSA_HEAD_DIM)
    k = proj[..., o1:o2].reshape(b, length, DSA_HEADS, DSA_HEAD_DIM)
    v = proj[..., o2:o3].reshape(b, length, DSA_HEADS, DSA_HEAD_DIM)
    qi = proj[..., o3:o4].reshape(b, length, IDX_HEADS, IDX_DIM).astype(jnp.float32)
    ki = layernorm(proj[..., o4:o5], idx_k_g, idx_k_b).astype(jnp.float32)
    wi = proj[..., o5:].astype(jnp.float32) * (IDX_HEADS ** -0.5 * IDX_DIM ** -0.5)
    key_pos = jnp.arange(length, dtype=jnp.int32)
    scale = DSA_HEAD_DIM ** -0.5

    def block(t0, qb, qib, wib):
        qpos = t0 + jnp.arange(Q_BLOCK, dtype=jnp.int32)
        causal = key_pos[None, :] <= qpos[:, None]
        s = jax.nn.relu(jnp.einsum('bqhd,bsd->bqhs', qib, ki))
        score = jnp.einsum('bqh,bqhs->bqs', wib, s)
        score = jnp.where(causal[None], score, -jnp.inf)
        _, sel = lax.top_k(score, topk)
        valid = sel <= qpos[None, :, None]
        k_sel = jax.vmap(lambda kk, ii: kk[ii])(k, sel)
        v_sel = jax.vmap(lambda vv, ii: vv[ii])(v, sel)
        logits = jnp.einsum('bqhd,bqkhd->bqhk', qb, k_sel).astype(jnp.float32) * scale
        logits = jnp.where(valid[:, :, None, :], logits, -jnp.inf)
        p = jax.nn.softmax(logits, axis=-1)
        return jnp.einsum('bqhk,bqkhd->bqhd', p.astype(v.dtype), v_sel)

    o = blocked_queries(block, q, qi, wi)
    return o.reshape(b, length, D_MODEL) @ w_out


def conv_mixer(h, w_pw1, b_pw1, w_dw, b_dw, ln_g, ln_b, w_pw2, b_pw2):
    a = h @ w_pw1 + b_pw1
    u = a[..., :D_MODEL] * jax.nn.sigmoid(a[..., D_MODEL:])
    u = causal_depthwise_conv(u, w_dw) + b_dw
    u = jax.nn.silu(layernorm(u, ln_g, ln_b))
    return u @ w_pw2 + b_pw2


def mla_mixer(h, w_in, q_norm_g, w_uq, kv_norm_g, w_ukv, w_out):
    b, length, _ = h.shape
    proj = h @ w_in
    cq = rmsnorm(proj[..., :MLA_Q_LORA], q_norm_g)
    ckv = rmsnorm(proj[..., MLA_Q_LORA:MLA_Q_LORA + MLA_KV_LORA], kv_norm_g)
    kr = proj[..., MLA_Q_LORA + MLA_KV_LORA:]
    q = (cq @ w_uq).reshape(b, length, MLA_HEADS, MLA_NOPE + MLA_ROPE)
    kv = (ckv @ w_ukv).reshape(b, length, MLA_HEADS, MLA_NOPE + MLA_V)
    q_nope, q_rope = q[..., :MLA_NOPE], q[..., MLA_NOPE:]
    k_nope, v = kv[..., :MLA_NOPE], kv[..., MLA_NOPE:]
    cos, sin = rope_tables(length, MLA_ROPE)
    q_rope = apply_rope(q_rope, cos[None, :, None, :], sin[None, :, None, :])
    kr = apply_rope(kr, cos[None], sin[None])
    key_pos = jnp.arange(length, dtype=jnp.int32)
    scale = (MLA_NOPE + MLA_ROPE) ** -0.5

    def block(t0, qn, qr):
        qpos = t0 + jnp.arange(Q_BLOCK, dtype=jnp.int32)
        causal = key_pos[None, :] <= qpos[:, None]
        logits = (jnp.einsum('bqhd,bshd->bhqs', qn, k_nope)
                  + jnp.einsum('bqhr,bsr->bhqs', qr, kr)).astype(jnp.float32) * scale
        logits = jnp.where(causal[None, None], logits, -jnp.inf)
        p = jax.nn.softmax(logits, axis=-1)
        return jnp.einsum('bhqs,bshd->bqhd', p.astype(v.dtype), v)

    o = blocked_queries(block, q_nope, q_rope)
    return o.reshape(b, length, MLA_HEADS * MLA_V) @ w_out


def chunk_gated_delta_rule(q, k, v, g, beta):
    b, length, nh, dk = q.shape
    dv = v.shape[-1]
    c = GDN_CHUNK
    n = length // c

    def to_chunks(x):
        return jnp.moveaxis(x.reshape(b, n, c, nh, *x.shape[3:]), 3, 1)

    q, k, v, g, beta = to_chunks(q), to_chunks(k), to_chunks(v), to_chunks(g), to_chunks(beta)
    gc = jnp.cumsum(g, axis=-1)
    tril = jnp.tril(jnp.ones((c, c), dtype=bool))
    tril_strict = jnp.tril(jnp.ones((c, c), dtype=bool), -1)
    diff = gc[..., :, None] - gc[..., None, :]
    decay_mat = jnp.where(tril, jnp.exp(jnp.where(tril, diff, 0.0)), 0.0)
    kk = jnp.einsum('bhncd,bhnsd->bhncs', k, k)
    a_mat = jnp.where(tril_strict, beta[..., :, None] * kk * decay_mat, 0.0)
    eye = jnp.eye(c, dtype=jnp.float32)
    rhs = jnp.concatenate([v * beta[..., None], k * (beta * jnp.exp(gc))[..., None]], axis=-1)
    sol = lax.linalg.triangular_solve(eye + a_mat, rhs, left_side=True, lower=True, unit_diagonal=True)
    u, w = sol[..., :dv], sol[..., dv:]
    qk = jnp.einsum('bhncd,bhnsd->bhncs', q, k) * decay_mat
    q_dec = q * jnp.exp(gc)[..., None]
    k_dec = k * jnp.exp(gc[..., -1:] - gc)[..., None]
    g_last = jnp.exp(gc[..., -1])

    xs = (jnp.moveaxis(u, 2, 0), jnp.moveaxis(w, 2, 0), jnp.moveaxis(qk, 2, 0),
          jnp.moveaxis(q_dec, 2, 0), jnp.moveaxis(k_dec, 2, 0), jnp.moveaxis(g_last, 2, 0))

    def step(s, inp):
        u_c, w_c, qk_c, qd_c, kd_c, gl_c = inp
        v_new = u_c - jnp.einsum('bhcd,bhdv->bhcv', w_c, s)
        o_c = jnp.einsum('bhcd,bhdv->bhcv', qd_c, s) + jnp.einsum('bhcs,bhsv->bhcv', qk_c, v_new)
        s = s * gl_c[..., None, None] + jnp.einsum('bhcd,bhcv->bhdv', kd_c, v_new)
        return s, o_c

    s0 = jnp.zeros((b, nh, dk, dv), dtype=jnp.float32)
    _, o = lax.scan(step, s0, xs)
    return jnp.transpose(o, (1, 0, 3, 2, 4)).reshape(b, length, nh, dv)


def gdn_mixer(h, w_in, conv_w, a_log, dt_bias, o_norm_g, w_out):
    b, length, _ = h.shape
    proj = h @ w_in
    n_qkv = 2 * GDN_HK + GDN_HV
    qkv = jax.nn.silu(causal_depthwise_conv(proj[..., :n_qkv], conv_w))
    gate = proj[..., n_qkv:n_qkv + GDN_HV].reshape(b, length, GDN_HEADS, GDN_DV).astype(jnp.float32)
    b_raw = proj[..., n_qkv + GDN_HV:n_qkv + GDN_HV + GDN_HEADS].astype(jnp.float32)
    a_raw = proj[..., n_qkv + GDN_HV + GDN_HEADS:].astype(jnp.float32)
    q = l2norm(qkv[..., :GDN_HK].reshape(b, length, GDN_HEADS, GDN_DK).astype(jnp.float32))
    k = l2norm(qkv[..., GDN_HK:2 * GDN_HK].reshape(b, length, GDN_HEADS, GDN_DK).astype(jnp.float32))
    v = qkv[..., 2 * GDN_HK:].reshape(b, length, GDN_HEADS, GDN_DV).astype(jnp.float32)
    beta = jax.nn.sigmoid(b_raw)
    g = -jnp.exp(a_log.astype(jnp.float32)) * jax.nn.softplus(a_raw + dt_bias.astype(jnp.float32))
    o = chunk_gated_delta_rule(q * GDN_DK ** -0.5, k, v, g, beta)
    o = rmsnorm(o, o_norm_g) * jax.nn.silu(gate)
    return o.reshape(b, length, GDN_HV).astype(h.dtype) @ w_out


def sq_relu_mlp(h, w1, w2):
    return jnp.square(jax.nn.relu(h @ w1)) @ w2


def _num_layers_of(m):
    return len(range(m, DEPTH, N_MIXERS))


def setup_inputs(seed: int = 0) -> dict:
    key = jax.random.key(seed)
    keys = jax.random.split(key, 64)
    counter = [0]

    def nxt():
        kk = keys[counter[0]]
        counter[0] += 1
        return kk

    def nrm(shape, fan_in):
        return jax.random.normal(nxt(), shape, jnp.float32) * fan_in ** -0.5

    def gain(shape):
        return 1.0 + 0.02 * jax.random.normal(nxt(), shape, jnp.float32)

    def bias(shape):
        return 0.02 * jax.random.normal(nxt(), shape, jnp.float32)

    na, nb, nc, nd = (_num_layers_of(m) for m in range(N_MIXERS))
    x = jax.random.normal(nxt(), (BATCH, SEQ, D_MODEL), jnp.float32)
    dt = jnp.exp(jax.random.uniform(nxt(), (nd, GDN_HEADS), jnp.float32, math.log(1e-3), math.log(1e-1)))
    return {
        'x': x,
        'norm_mix_g': gain((DEPTH, D_MODEL)),
        'norm_mlp_g': gain((DEPTH, D_MODEL)),
        'final_g': gain((D_MODEL,)),
        'mlp_w1': nrm((DEPTH, D_MODEL, D_FF), D_MODEL),
        'mlp_w2': nrm((DEPTH, D_FF, D_MODEL), D_FF),
        'dsa_w_in': nrm((na, D_MODEL, DSA_IN), D_MODEL),
        'dsa_idx_k_g': gain((na, IDX_DIM)),
        'dsa_idx_k_b': bias((na, IDX_DIM)),
        'dsa_w_out': nrm((na, D_MODEL, D_MODEL), D_MODEL),
        'conv_w_pw1': nrm((nb, D_MODEL, 2 * D_MODEL), D_MODEL),
        'conv_b_pw1': bias((nb, 2 * D_MODEL)),
        'conv_w_dw': nrm((nb, CONV_WIDTH, D_MODEL), CONV_WIDTH),
        'conv_b_dw': bias((nb, D_MODEL)),
        'conv_ln_g': gain((nb, D_MODEL)),
        'conv_ln_b': bias((nb, D_MODEL)),
        'conv_w_pw2': nrm((nb, D_MODEL, D_MODEL), D_MODEL),
        'conv_b_pw2': bias((nb, D_MODEL)),
        'mla_w_in': nrm((nc, D_MODEL, MLA_IN), D_MODEL),
        'mla_q_norm_g': gain((nc, MLA_Q_LORA)),
        'mla_w_uq': nrm((nc, MLA_Q_LORA, MLA_HEADS * (MLA_NOPE + MLA_ROPE)), MLA_Q_LORA),
        'mla_kv_norm_g': gain((nc, MLA_KV_LORA)),
        'mla_w_ukv': nrm((nc, MLA_KV_LORA, MLA_HEADS * (MLA_NOPE + MLA_V)), MLA_KV_LORA),
        'mla_w_out': nrm((nc, MLA_HEADS * MLA_V, D_MODEL), MLA_HEADS * MLA_V),
        'gdn_w_in': nrm((nd, D_MODEL, GDN_IN), D_MODEL),
        'gdn_conv_w': nrm((nd, GDN_CONV, 2 * GDN_HK + GDN_HV), GDN_CONV),
        'gdn_a_log': jnp.log(jax.random.uniform(nxt(), (nd, GDN_HEADS), jnp.float32, 1.0, 16.0)),
        'gdn_dt_bias': dt + jnp.log(-jnp.expm1(-dt)),
        'gdn_o_norm_g': gain((nd, GDN_DV)),
        'gdn_w_out': nrm((nd, GDN_HV, D_MODEL), GDN_HV),
    }


def reference(x, norm_mix_g, norm_mlp_g, final_g, mlp_w1, mlp_w2,
              dsa_w_in, dsa_idx_k_g, dsa_idx_k_b, dsa_w_out,
              conv_w_pw1, conv_b_pw1, conv_w_dw, conv_b_dw, conv_ln_g, conv_ln_b, conv_w_pw2, conv_b_pw2,
              mla_w_in, mla_q_norm_g, mla_w_uq, mla_kv_norm_g, mla_w_ukv, mla_w_out,
              gdn_w_in, gdn_conv_w, gdn_a_log, gdn_dt_bias, gdn_o_norm_g, gdn_w_out):
    h = x
    for i in range(DEPTH):
        m = i % N_MIXERS
        j = i // N_MIXERS
        hn = rmsnorm(h, norm_mix_g[i])
        if m == 0:
            y = dsa_mixer(hn, dsa_w_in[j], dsa_idx_k_g[j], dsa_idx_k_b[j], dsa_w_out[j])
        elif m == 1:
            y = conv_mixer(hn, conv_w_pw1[j], conv_b_pw1[j], conv_w_dw[j], conv_b_dw[j],
                           conv_ln_g[j], conv_ln_b[j], conv_w_pw2[j], conv_b_pw2[j])
        elif m == 2:
            y = mla_mixer(hn, mla_w_in[j], mla_q_norm_g[j], mla_w_uq[j], mla_kv_norm_g[j],
                          mla_w_ukv[j], mla_w_out[j])
        else:
            y = gdn_mixer(hn, gdn_w_in[j], gdn_conv_w[j], gdn_a_log[j], gdn_dt_bias[j],
                          gdn_o_norm_g[j], gdn_w_out[j])
        h = h + y
        h = h + sq_relu_mlp(rmsnorm(h, norm_mlp_g[i]), mlp_w1[i], mlp_w2[i])
    return rmsnorm(h, final_g)
```

```python
import functools
import math

import jax
import jax.numpy as jnp
import numpy as np
from jax import lax
from jax.experimental import pallas as pl
from jax.experimental.pallas import tpu as pltpu

F32 = jnp.float32
MXU_DTYPE = jnp.bfloat16
EPS = 1e-6
LANES = 128
VMEM_LIMIT_BYTES = 56 * 1024 * 1024
NEG_BIG = -1e30
INT_MIN = -(2 ** 31)

Q_BLOCK = 128
IDX_HEADS = 8
IDX_DIM = 64
TOPK_MAX = 256
DSA_HEADS = 8
CONV_WIDTH = 31
MLA_HEADS = 16
MLA_Q_LORA = 384
MLA_KV_LORA = 256
MLA_NOPE = 64
MLA_ROPE = 32
MLA_V = 64
ROPE_THETA = 10000.0
GDN_HEADS = 8
GDN_DK = 128
GDN_DV = 128
GDN_CONV = 4
GDN_CHUNK = 64


def _params(*sem):
    return pltpu.CompilerParams(dimension_semantics=sem, vmem_limit_bytes=VMEM_LIMIT_BYTES)


def _dot(a, b):
    return jnp.dot(a.astype(MXU_DTYPE), b.astype(MXU_DTYPE), preferred_element_type=F32)


def _dot_nt(a, b):
    return lax.dot_general(a.astype(MXU_DTYPE), b.astype(MXU_DTYPE),
                           (((1,), (1,)), ((), ())), preferred_element_type=F32)


def _rmsnorm_rows(x, g):
    return x * lax.rsqrt(jnp.mean(x * x, axis=-1, keepdims=True) + EPS) * g


def _sigmoid(x):
    return 1.0 / (1.0 + jnp.exp(-x))


def _silu(x):
    return x * _sigmoid(x)


def _const_spec(shape):
    return pl.BlockSpec(shape, lambda *_: (0,) * len(shape))


def _mlp_kernel(h_ref, g_ref, w1_ref, w2_ref, gf_ref, o_ref, xn_ref, acc_ref, *, final_norm):
    f = pl.program_id(1)

    @pl.when(f == 0)
    def _():
        xn_ref[...] = _rmsnorm_rows(h_ref[...], g_ref[...]).astype(xn_ref.dtype)
        acc_ref[...] = jnp.zeros_like(acc_ref)

    a = jnp.dot(xn_ref[...], w1_ref[...], preferred_element_type=F32)
    a = jnp.square(jnp.maximum(a, 0.0))
    acc_ref[...] += _dot(a, w2_ref[...])

    @pl.when(f == pl.num_programs(1) - 1)
    def _():
        y = h_ref[...] + acc_ref[...]
        if final_norm:
            y = _rmsnorm_rows(y, gf_ref[...])
        o_ref[...] = y


def _mlp(h, g, w1, w2, gf, final_norm, tm=512, tf=1024):
    L, D = h.shape
    dff = w1.shape[1]
    return pl.pallas_call(
        functools.partial(_mlp_kernel, final_norm=final_norm),
        grid=(L // tm, dff // tf),
        in_specs=[
            pl.BlockSpec((tm, D), lambda i, f: (i, 0)),
            _const_spec((1, D)),
            pl.BlockSpec((D, tf), lambda i, f: (0, f)),
            pl.BlockSpec((tf, D), lambda i, f: (f, 0)),
            _const_spec((1, D)),
        ],
        out_specs=pl.BlockSpec((tm, D), lambda i, f: (i, 0)),
        out_shape=jax.ShapeDtypeStruct((L, D), F32),
        scratch_shapes=[pltpu.VMEM((tm, D), MXU_DTYPE), pltpu.VMEM((tm, D), F32)],
        compiler_params=_params("arbitrary", "arbitrary"),
        name="mlp",
    )(h, g.reshape(1, D), w1, w2, gf.reshape(1, D))


def _out_proj_kernel(a_ref, w_ref, b_ref, h_ref, o_ref):
    o_ref[...] = h_ref[...] + _dot(a_ref[...], w_ref[...]) + b_ref[...]


def _out_proj(a, w, b, h, tm=512):
    L, K = a.shape
    D = w.shape[1]
    return pl.pallas_call(
        _out_proj_kernel,
        grid=(L // tm,),
        in_specs=[
            pl.BlockSpec((tm, K), lambda i: (i, 0)),
            _const_spec((K, D)),
            _const_spec((1, D)),
            pl.BlockSpec((tm, D), lambda i: (i, 0)),
        ],
        out_specs=pl.BlockSpec((tm, D), lambda i: (i, 0)),
        out_shape=jax.ShapeDtypeStruct((L, D), F32),
        compiler_params=_params("arbitrary"),
        name="out_proj",
    )(a, w, b.reshape(1, D), h)


def _norm_proj_kernel(h_ref, g_ref, w_ref, o_ref, xn_ref):
    @pl.when(pl.program_id(1) == 0)
    def _():
        xn_ref[...] = _rmsnorm_rows(h_ref[...], g_ref[...]).astype(xn_ref.dtype)

    o_ref[...] = jnp.dot(xn_ref[...], w_ref[...], preferred_element_type=F32).astype(o_ref.dtype)


def _norm_proj(h, g, w, tn, out_dtype=F32, tm=512):
    L, D = h.shape
    N = w.shape[1]
    return pl.pallas_call(
        _norm_proj_kernel,
        grid=(L // tm, N // tn),
        in_specs=[
            pl.BlockSpec((tm, D), lambda i, j: (i, 0)),
            _const_spec((1, D)),
            pl.BlockSpec((D, tn), lambda i, j: (0, j)),
        ],
        out_specs=pl.BlockSpec((tm, tn), lambda i, j: (i, j)),
        out_shape=jax.ShapeDtypeStruct((L, N), out_dtype),
        scratch_shapes=[pltpu.VMEM((tm, D), MXU_DTYPE)],
        compiler_params=_params("arbitrary", "arbitrary"),
        name="norm_proj",
    )(h, g.reshape(1, D), w)


def _dsa_proj_kernel(h_ref, g_ref, w_ref, lg_ref, lb_ref,
                     q_ref, k_ref, v_ref, qi_ref, kia_ref, kib_ref, wi_ref, *, d, scale, wi_scale):
    xn = _rmsnorm_rows(h_ref[...], g_ref[...]).astype(MXU_DTYPE)

    def mm(c0, n):
        return jnp.dot(xn, w_ref[:, c0:c0 + n], preferred_element_type=F32)

    cw = 512
    for c in range(0, d, cw):
        q_ref[:, c:c + cw] = (mm(c, cw) * scale).astype(q_ref.dtype)
        k_ref[:, c:c + cw] = mm(d + c, cw).astype(k_ref.dtype)
        v_ref[:, c:c + cw] = mm(2 * d + c, cw).astype(v_ref.dtype)
    nqi = IDX_HEADS * IDX_DIM
    qi_ref[...] = mm(3 * d, nqi).astype(qi_ref.dtype)
    xa = mm(3 * d + nqi, LANES)
    xb = mm(3 * d + nqi + LANES, LANES)
    lane = lax.broadcasted_iota(jnp.int32, xa.shape, 1)

    def masked_ln(x, m, gain, bias):
        mu = jnp.sum(jnp.where(m, x, 0.0), axis=-1, keepdims=True) * (1.0 / IDX_DIM)
        dlt = jnp.where(m, x - mu, 0.0)
        var = jnp.sum(dlt * dlt, axis=-1, keepdims=True) * (1.0 / IDX_DIM)
        return dlt * lax.rsqrt(var + EPS) * gain + bias

    kia_ref[...] = masked_ln(xa, lane < IDX_DIM, lg_ref[0:1, :], lb_ref[0:1, :]).astype(kia_ref.dtype)
    kib_ref[...] = masked_ln(xb, lane >= IDX_DIM, lg_ref[1:2, :], lb_ref[1:2, :]).astype(kib_ref.dtype)
    wi_ref[...] = jnp.where(lane < IDX_HEADS, xb, 0.0) * wi_scale


def _dsa_proj(h, g, w_cat, lg2, lb2, tm=256):
    L, D = h.shape
    N = w_cat.shape[1]
    nqi = IDX_HEADS * IDX_DIM
    row = lambda n: pl.BlockSpec((tm, n), lambda i: (i, 0))
    return pl.pallas_call(
        functools.partial(_dsa_proj_kernel, d=D, scale=(D // DSA_HEADS) ** -0.5,
                          wi_scale=IDX_HEADS ** -0.5 * IDX_DIM ** -0.5),
        grid=(L // tm,),
        in_specs=[row(D), _const_spec((1, D)), _const_spec((D, N)),
                  _const_spec((2, LANES)), _const_spec((2, LANES))],
        out_specs=[row(D), row(D), row(D), row(nqi), row(LANES), row(LANES), row(LANES)],
        out_shape=[jax.ShapeDtypeStruct((L, D), MXU_DTYPE)] * 3
        + [jax.ShapeDtypeStruct((L, nqi), MXU_DTYPE)]
        + [jax.ShapeDtypeStruct((L, LANES), MXU_DTYPE)] * 2
        + [jax.ShapeDtypeStruct((L, LANES), F32)],
        compiler_params=_params("arbitrary"),
        name="dsa_proj",
    )(h, g.reshape(1, D), w_cat, lg2, lb2)


def _dsa_attn_kernel(q_ref, qi_ref, wi_ref, kia_ref, kib_ref, k_ref, v_ref, o_ref,
                     keys_ref, thr_ref, cut_ref, m_ref, l_ref, acc_ref,
                     *, tq, tk, topk, nheads, dh, rb):
    i = pl.program_id(0)
    j = pl.program_id(1)
    n_kt = ((i + 1) * tq + tk - 1) // tk
    q0 = i * tq

    @pl.when(j == 0)
    def _select():
        qpos = q0 + lax.broadcasted_iota(jnp.int32, (tq, tk), 0)
        lane_pos = lax.broadcasted_iota(jnp.int32, (tq, tk), 1)

        def score_tile(c, carry):
            k0 = pl.multiple_of(c * tk, tk)
            ka = kia_ref[pl.ds(k0, tk), :]
            kb = kib_ref[pl.ds(k0, tk), :]
            acc = jnp.zeros((tq, tk), F32)
            for p in range(IDX_HEADS // 2):
                lhs = qi_ref[:, p * LANES:(p + 1) * LANES]
                sa = jnp.maximum(_dot_nt(lhs, ka), 0.0)
                sb = jnp.maximum(_dot_nt(lhs, kb), 0.0)
                acc = acc + wi_ref[:, 2 * p:2 * p + 1] * sa
                acc = acc + wi_ref[:, 2 * p + 1:2 * p + 2] * sb
            bits = pltpu.bitcast(acc, jnp.int32)
            key = bits ^ ((bits >> 31) & jnp.int32(0x7FFFFFFF))
            key = jnp.where(k0 + lane_pos <= qpos, key, jnp.int32(INT_MIN))
            keys_ref[c] = key
            return carry

        lax.fori_loop(0, n_kt, score_tile, 0)

        def count_rows(r0, pred):
            def body(c, acc):
                hit = pred(keys_ref[c, r0:r0 + rb, :], c).astype(jnp.int32)
                part = hit[:, 0:LANES]
                for t in range(1, tk // LANES):
                    part = part + hit[:, t * LANES:(t + 1) * LANES]
                return acc + part
            acc = lax.fori_loop(0, n_kt, body, jnp.zeros((rb, LANES), jnp.int32))
            return jnp.sum(acc.astype(F32), axis=1, keepdims=True)

        for r0 in range(0, tq, rb):
            def bit_step(b, thr):
                cand = thr + lax.shift_left(jnp.int32(1), 31 - b)
                cnt = count_rows(r0, lambda kk, c: kk >= cand)
                return jnp.where(cnt >= topk, cand, thr)

            thr = lax.fori_loop(0, 32, bit_step, jnp.full((rb, 1), INT_MIN, jnp.int32))
            n_gt = count_rows(r0, lambda kk, c: kk > thr)
            n_ge = count_rows(r0, lambda kk, c: kk >= thr)
            need = topk - n_gt
            thr_ref[r0:r0 + rb, :] = jnp.broadcast_to(thr, (rb, LANES))
            no_cut = jnp.where(thr == INT_MIN, -1, jnp.int32(2 ** 30))
            cut_ref[r0:r0 + rb, :] = jnp.broadcast_to(no_cut, (rb, LANES))
            excess = jnp.max(jnp.where(thr == INT_MIN, 0.0, n_ge - n_gt - need))

            @pl.when(excess > 0.0)
            def _ties():
                lane_idx = lax.broadcasted_iota(jnp.int32, (rb, tk), 1)

                def idx_step(b, cut):
                    cand = cut + lax.shift_left(jnp.int32(1), 30 - b)
                    cnt = count_rows(
                        r0, lambda kk, c: jnp.logical_and(kk == thr, c * tk + lane_idx < cand))
                    return jnp.where(cnt < need, cand, cut)

                cut = lax.fori_loop(0, 31, idx_step, jnp.zeros((rb, 1), jnp.int32))
                cut = jnp.where(thr == INT_MIN, -1, cut)
                cut_ref[r0:r0 + rb, :] = jnp.broadcast_to(cut, (rb, LANES))

        m_ref[...] = jnp.full_like(m_ref, NEG_BIG)
        l_ref[...] = jnp.zeros_like(l_ref)
        acc_ref[...] = jnp.zeros_like(acc_ref)

    @pl.when(j < n_kt)
    def _attend():
        keyt = keys_ref[j]
        thr = thr_ref[:, 0:1]
        cut = cut_ref[:, 0:1]
        kidx = j * tk + lax.broadcasted_iota(jnp.int32, (tq, tk), 1)
        sel = jnp.logical_or(keyt > thr, jnp.logical_and(keyt == thr, kidx <= cut))
        bias = jnp.where(sel, 0.0, NEG_BIG)
        for h in range(nheads):
            cs = slice(h * dh, (h + 1) * dh)
            s = _dot_nt(q_ref[:, cs], k_ref[:, cs]) + bias
            m_prev = m_ref[h]
            m_next = jnp.maximum(m_prev, jnp.max(s, axis=1, keepdims=True))
            alpha = jnp.exp(m_prev - m_next)
            p = jnp.exp(s - m_next[:, 0:1])
            l_ref[h] = alpha * l_ref[h] + jnp.sum(p, axis=1, keepdims=True)
            m_ref[h] = m_next
            acc_ref[:, cs] = alpha * acc_ref[:, cs] + _dot(p, v_ref[:, cs])

    @pl.when(j == n_kt - 1)
    def _finish():
        for h in range(nheads):
            cs = slice(h * dh, (h + 1) * dh)
            o_ref[:, cs] = (acc_ref[:, cs] / l_ref[h]).astype(o_ref.dtype)


def _dsa_attn(q, k, v, qi, kia, kib, wi, topk, tq=256, tk=512, rb=64):
    L, D = q.shape
    nheads = DSA_HEADS
    dh = D // nheads
    nqi = qi.shape[1]
    last_tile = lambda i: ((i + 1) * tq - 1) // tk
    kv_spec = pl.BlockSpec((tk, D), lambda i, j: (jnp.minimum(j, last_tile(i)), 0))
    return pl.pallas_call(
        functools.partial(_dsa_attn_kernel, tq=tq, tk=tk, topk=topk, nheads=nheads, dh=dh, rb=rb),
        grid=(L // tq, L // tk),
        in_specs=[
            pl.BlockSpec((tq, D), lambda i, j: (i, 0)),
            pl.BlockSpec((tq, nqi), lambda i, j: (i, 0)),
            pl.BlockSpec((tq, LANES), lambda i, j: (i, 0)),
            _const_spec((L, LANES)),
            _const_spec((L, LANES)),
            kv_spec, kv_spec,
        ],
        out_specs=pl.BlockSpec((tq, D), lambda i, j: (i, 0)),
        out_shape=jax.ShapeDtypeStruct((L, D), MXU_DTYPE),
        scratch_shapes=[
            pltpu.VMEM((L // tk, tq, tk), jnp.int32),
            pltpu.VMEM((tq, LANES), jnp.int32),
            pltpu.VMEM((tq, LANES), jnp.int32),
            pltpu.VMEM((nheads, tq, LANES), F32),
            pltpu.VMEM((nheads, tq, LANES), F32),
            pltpu.VMEM((tq, D), F32),
        ],
        compiler_params=_params("arbitrary", "arbitrary"),
        name="dsa_attn",
    )(q, qi, wi, kia, kib, k, v)


def _dsa_layer(h, g, w_in, idx_k_g, idx_k_b, w_out):
    L, D = h.shape
    nqi = IDX_HEADS * IDX_DIM
    o3 = 3 * D
    o4 = o3 + nqi
    o5 = o4 + IDX_DIM
    w_ki = w_in[:, o4:o5]
    w_wi = w_in[:, o5:]
    zpad = lambda n: jnp.zeros((D, n), w_in.dtype)
    w_cat = jnp.concatenate(
        [w_in[:, :o4],
         w_ki, w_wi, zpad(LANES - IDX_DIM - IDX_HEADS),
         w_wi, zpad(LANES - IDX_DIM - IDX_HEADS), w_ki], axis=1).astype(MXU_DTYPE)
    z = jnp.zeros((IDX_DIM,), F32)
    lg2 = jnp.stack([jnp.concatenate([idx_k_g, z]), jnp.concatenate([z, idx_k_g])])
    lb2 = jnp.stack([jnp.concatenate([idx_k_b, z]), jnp.concatenate([z, idx_k_b])])
    q, k, v, qi, kia, kib, wi = _dsa_proj(h, g, w_cat, lg2, lb2)
    o = _dsa_attn(q, k, v, qi, kia, kib, wi, topk=min(TOPK_MAX, L // 4))
    return _out_proj(o, w_out.astype(MXU_DTYPE), jnp.zeros((D,), F32), h)


def _glu_proj_kernel(h_ref, g_ref, w_ref, b_ref, u_ref, *, d):
    xn = _rmsnorm_rows(h_ref[...], g_ref[...]).astype(MXU_DTYPE)
    cw = 512
    for c in range(0, d, cw):
        a = jnp.dot(xn, w_ref[:, c:c + cw], preferred_element_type=F32) + b_ref[:, c:c + cw]
        gt = jnp.dot(xn, w_ref[:, d + c:d + c + cw], preferred_element_type=F32) + b_ref[:, d + c:d + c + cw]
        u_ref[:, c:c + cw] = a * _sigmoid(gt)


def _glu_proj(h, g, w, b, tm=512):
    L, D = h.shape
    return pl.pallas_call(
        functools.partial(_glu_proj_kernel, d=D),
        grid=(L // tm,),
        in_specs=[pl.BlockSpec((tm, D), lambda i: (i, 0)), _const_spec((1, D)),
                  _const_spec((D, 2 * D)), _const_spec((1, 2 * D))],
        out_specs=pl.BlockSpec((tm, D), lambda i: (i, 0)),
        out_shape=jax.ShapeDtypeStruct((L, D), F32),
        compiler_params=_params("arbitrary"),
        name="conv_glu_proj",
    )(h, g.reshape(1, D), w, b.reshape(1, 2 * D))


CONV_HALO = 32


def _conv_out_kernel(u_ref, up_ref, wdw_ref, bdw_ref, lg_ref, lb_ref, w2_ref, b2_ref, h_ref, o_ref,
                     x_ref, *, tm):
    i = pl.program_id(0)
    x_ref[CONV_HALO:, :] = u_ref[...]
    x_ref[:CONV_HALO, :] = jnp.where(i > 0, up_ref[...], 0.0)
    off = CONV_HALO - (CONV_WIDTH - 1)
    acc = jnp.zeros(u_ref.shape, F32)
    for t in range(CONV_WIDTH):
        acc = acc + wdw_ref[t:t + 1, :] * x_ref[off + t:off + t + tm, :]
    y = acc + bdw_ref[...]
    mu = jnp.mean(y, axis=-1, keepdims=True)
    dlt = y - mu
    var = jnp.mean(dlt * dlt, axis=-1, keepdims=True)
    y = _silu(dlt * lax.rsqrt(var + EPS) * lg_ref[...] + lb_ref[...])
    o_ref[...] = h_ref[...] + _dot(y, w2_ref[...]) + b2_ref[...]


def _conv_out(u, w_dw, b_dw, ln_g, ln_b, w2, b2, h, tm=256):
    L, D = u.shape
    r = tm // CONV_HALO
    vec = lambda a: a.reshape(1, D)
    return pl.pallas_call(
        functools.partial(_conv_out_kernel, tm=tm),
        grid=(L // tm,),
        in_specs=[
            pl.BlockSpec((tm, D), lambda i: (i, 0)),
            pl.BlockSpec((CONV_HALO, D), lambda i: (jnp.maximum(i * r - 1, 0), 0)),
            _const_spec((CONV_WIDTH, D)),
            _const_spec((1, D)), _const_spec((1, D)), _const_spec((1, D)),
            _const_spec((D, D)), _const_spec((1, D)),
            pl.BlockSpec((tm, D), lambda i: (i, 0)),
        ],
        out_specs=pl.BlockSpec((tm, D), lambda i: (i, 0)),
        out_shape=jax.ShapeDtypeStruct((L, D), F32),
        scratch_shapes=[pltpu.VMEM((tm + CONV_HALO, D), F32)],
        compiler_params=_params("arbitrary"),
        name="conv_out",
    )(u, u, w_dw, vec(b_dw), vec(ln_g), vec(ln_b), w2, vec(b2), h)


def _conv_layer(h, g, w_pw1, b_pw1, w_dw, b_dw, ln_g, ln_b, w_pw2, b_pw2):
    u = _glu_proj(h, g, w_pw1.astype(MXU_DTYPE), b_pw1)
    return _conv_out(u, w_dw, b_dw, ln_g, ln_b, w_pw2.astype(MXU_DTYPE), b_pw2, h)


def _mla_prep_kernel(h_ref, g_ref, w1_ref, qg_ref, kvg_ref, wqa_ref, wqb_ref, wka_ref, wv_ref,
                     c_ref, s_ref, q_ref, k_ref, v_ref, *, scale):
    xn = _rmsnorm_rows(h_ref[...], g_ref[...]).astype(MXU_DTYPE)
    proj = jnp.dot(xn, w1_ref[...], preferred_element_type=F32)
    o1 = MLA_Q_LORA
    o2 = o1 + MLA_KV_LORA
    cq = _rmsnorm_rows(proj[:, :o1], qg_ref[...]).astype(MXU_DTYPE)
    ckv = _rmsnorm_rows(proj[:, o1:o2], kvg_ref[...]).astype(MXU_DTYPE)
    cos = c_ref[...]
    sin = s_ref[...]
    kr = proj[:, o2:o2 + LANES] * cos + proj[:, o2 + LANES:o2 + 2 * LANES] * sin
    for hd in range(MLA_HEADS):
        cs = slice(hd * LANES, (hd + 1) * LANES)
        qa = jnp.dot(cq, wqa_ref[:, cs], preferred_element_type=F32)
        qb = jnp.dot(cq, wqb_ref[:, cs], preferred_element_type=F32)
        q_ref[:, cs] = ((qa * cos + qb * sin) * scale).astype(q_ref.dtype)
        ka = jnp.dot(ckv, wka_ref[:, cs], preferred_element_type=F32)
        k_ref[:, cs] = (ka + kr).astype(k_ref.dtype)
    v_ref[...] = jnp.dot(ckv, wv_ref[...], preferred_element_type=F32).astype(v_ref.dtype)


def _mla_prep(h, g, w1, qg, kvg, wqa, wqb, wka, wv, ctab, stab, tm=256):
    L, D = h.shape
    HP = MLA_HEADS * LANES
    NV = MLA_HEADS * MLA_V
    row = lambda n: pl.BlockSpec((tm, n), lambda i: (i, 0))
    full = lambda a: _const_spec(a.shape)
    return pl.pallas_call(
        functools.partial(_mla_prep_kernel, scale=(MLA_NOPE + MLA_ROPE) ** -0.5),
        grid=(L // tm,),
        in_specs=[row(D), _const_spec((1, D)), full(w1), _const_spec((1, MLA_Q_LORA)),
                  _const_spec((1, MLA_KV_LORA)), full(wqa), full(wqb), full(wka), full(wv),
                  row(LANES), row(LANES)],
        out_specs=[row(HP), row(HP), row(NV)],
        out_shape=[jax.ShapeDtypeStruct((L, HP), MXU_DTYPE), jax.ShapeDtypeStruct((L, HP), MXU_DTYPE),
                   jax.ShapeDtypeStruct((L, NV), MXU_DTYPE)],
        compiler_params=_params("arbitrary"),
        name="mla_prep",
    )(h, g.reshape(1, D), w1, qg.reshape(1, -1), kvg.reshape(1, -1), wqa, wqb, wka, wv, ctab, stab)


def _mla_attn_kernel(q_ref, k_ref, v_ref, o_ref, m_ref, l_ref, acc_ref, *, tq, tk, nheads):
    i = pl.program_id(0)
    j = pl.program_id(1)
    n_kt = ((i + 1) * tq + tk - 1) // tk

    @pl.when(j == 0)
    def _init():
        m_ref[...] = jnp.full_like(m_ref, NEG_BIG)
        l_ref[...] = jnp.zeros_like(l_ref)
        acc_ref[...] = jnp.zeros_like(acc_ref)

    @pl.when(j < n_kt)
    def _attend():
        qpos = i * tq + lax.broadcasted_iota(jnp.int32, (tq, tk), 0)
        kpos = j * tk + lax.broadcasted_iota(jnp.int32, (tq, tk), 1)
        bias = jnp.where(kpos <= qpos, 0.0, NEG_BIG)
        for h in range(nheads):
            cs = slice(h * LANES, (h + 1) * LANES)
            vs = slice((h // 2) * LANES, (h // 2 + 1) * LANES)
            s = _dot_nt(q_ref[:, cs], k_ref[:, cs]) + bias
            m_prev = m_ref[h]
            m_next = jnp.maximum(m_prev, jnp.max(s, axis=1, keepdims=True))
            alpha = jnp.exp(m_prev - m_next)
            p = jnp.exp(s - m_next[:, 0:1])
            l_ref[h] = alpha * l_ref[h] + jnp.sum(p, axis=1, keepdims=True)
            m_ref[h] = m_next
            acc_ref[:, cs] = alpha * acc_ref[:, cs] + _dot(p, v_ref[:, vs])

    @pl.when(j == n_kt - 1)
    def _finish():
        for h in range(nheads):
            cs = slice(h * LANES, (h + 1) * LANES)
            o_ref[:, cs] = (acc_ref[:, cs] / l_ref[h]).astype(o_ref.dtype)


def _mla_attn(q, k, v, tq=256, tk=512):
    L, HP = q.shape
    NV = v.shape[1]
    last_tile = lambda i: ((i + 1) * tq - 1) // tk
    kmap = lambda i, j: (jnp.minimum(j, last_tile(i)), 0)
    return pl.pallas_call(
        functools.partial(_mla_attn_kernel, tq=tq, tk=tk, nheads=MLA_HEADS),
        grid=(L // tq, L // tk),
        in_specs=[pl.BlockSpec((tq, HP), lambda i, j: (i, 0)),
                  pl.BlockSpec((tk, HP), kmap), pl.BlockSpec((tk, NV), kmap)],
        out_specs=pl.BlockSpec((tq, HP), lambda i, j: (i, 0)),
        out_shape=jax.ShapeDtypeStruct((L, HP), MXU_DTYPE),
        scratch_shapes=[pltpu.VMEM((MLA_HEADS, tq, LANES), F32),
                        pltpu.VMEM((MLA_HEADS, tq, LANES), F32),
                        pltpu.VMEM((tq, HP), F32)],
        compiler_params=_params("arbitrary", "arbitrary"),
        name="mla_attn",
    )(q, k, v)


def _place_heads(w, nheads, src_w, lane0, swap_half=0):
    K = w.shape[0]
    w3 = w.reshape(K, nheads, src_w)
    if swap_half:
        w3 = jnp.concatenate([w3[..., swap_half:], w3[..., :swap_half]], axis=-1)
    out = jnp.zeros((K, nheads, LANES), w.dtype)
    out = out.at[:, :, lane0:lane0 + src_w].set(w3)
    return out.reshape(K, nheads * LANES)


def _mla_layer(h, g, w_in, q_norm_g, w_uq, kv_norm_g, w_ukv, w_out):
    L, D = h.shape
    half = MLA_ROPE // 2
    o2 = MLA_Q_LORA + MLA_KV_LORA
    w_kr = w_in[:, o2:]
    w1 = jnp.concatenate(
        [w_in[:, :o2], _place_heads(w_kr, 1, MLA_ROPE, MLA_NOPE),
         _place_heads(w_kr, 1, MLA_ROPE, MLA_NOPE, swap_half=half)], axis=1).astype(MXU_DTYPE)
    uq = w_uq.reshape(MLA_Q_LORA, MLA_HEADS, MLA_NOPE + MLA_ROPE)
    uq_rope = uq[..., MLA_NOPE:].reshape(MLA_Q_LORA, MLA_HEADS * MLA_ROPE)
    wqa = _place_heads(w_uq, MLA_HEADS, MLA_NOPE + MLA_ROPE, 0).astype(MXU_DTYPE)
    wqb = _place_heads(uq_rope, MLA_HEADS, MLA_ROPE, MLA_NOPE, swap_half=half).astype(MXU_DTYPE)
    ukv = w_ukv.reshape(MLA_KV_LORA, MLA_HEADS, MLA_NOPE + MLA_V)
    wka = _place_heads(ukv[..., :MLA_NOPE].reshape(MLA_KV_LORA, -1), MLA_HEADS, MLA_NOPE, 0).astype(MXU_DTYPE)
    wv = ukv[..., MLA_NOPE:].reshape(MLA_KV_LORA, MLA_HEADS * MLA_V).astype(MXU_DTYPE)
    pos = jnp.arange(L, dtype=F32)
    inv_freq = ROPE_THETA ** (-jnp.arange(0, MLA_ROPE, 2, dtype=F32) / MLA_ROPE)
    ang = pos[:, None] * inv_freq[None, :]
    cos, sin = jnp.cos(ang), jnp.sin(ang)
    zeros = jnp.zeros((L, LANES - MLA_NOPE - MLA_ROPE), F32)
    ctab = jnp.concatenate([jnp.ones((L, MLA_NOPE), F32), cos, cos, zeros], axis=1)
    stab = jnp.concatenate([jnp.zeros((L, MLA_NOPE), F32), -sin, sin, zeros], axis=1)
    q, k, v = _mla_prep(h, g, w1, q_norm_g, kv_norm_g, wqa, wqb, wka, wv, ctab, stab)
    o = _mla_attn(q, k, v)
    wo = w_out.reshape(MLA_HEADS // 2, 2, MLA_V, D)
    zo = jnp.zeros_like(wo[:, 0])
    wo_pad = jnp.stack([jnp.concatenate([wo[:, 0], zo], axis=1),
                        jnp.concatenate([zo, wo[:, 1]], axis=1)], axis=1)
    wo_pad = wo_pad.reshape(MLA_HEADS * LANES, D).astype(MXU_DTYPE)
    return _out_proj(o, wo_pad, jnp.zeros((D,), F32), h)


def _gdn_scan_kernel(qkv_ref, ba_ref, cw_ref, alog_ref, dtb_ref, o_ref, x_ref, s_ref,
                     *, nheads, dk, dv, c):
    n = pl.program_id(0)
    halo = 8
    nqk = nheads * dk

    @pl.when(n == 0)
    def _init():
        x_ref[:halo, :] = jnp.zeros((halo, x_ref.shape[1]), F32)
        s_ref[...] = jnp.zeros_like(s_ref)

    x_ref[halo:, :] = qkv_ref[...]
    off = halo - (GDN_CONV - 1)
    acc = jnp.zeros(qkv_ref.shape, F32)
    for t in range(GDN_CONV):
        acc = acc + cw_ref[t:t + 1, :] * x_ref[off + t:off + t + c, :]
    x_ref[:halo, :] = qkv_ref[c - halo:, :]
    qkv = _silu(acc)

    beta_all = _sigmoid(ba_ref[:, :LANES])
    a_raw = ba_ref[:, LANES:] + dtb_ref[...]
    softplus = jnp.maximum(a_raw, 0.0) + jnp.log(1.0 + jnp.exp(-jnp.abs(a_raw)))
    g_all = -jnp.exp(alog_ref[...]) * softplus
    row = lax.broadcasted_iota(jnp.int32, (c, c), 0)
    col = lax.broadcasted_iota(jnp.int32, (c, c), 1)
    tril = row >= col
    tril_strict = row > col
    gc_all = jnp.dot(tril.astype(F32), g_all, preferred_element_type=F32,
                     precision=lax.Precision.HIGHEST)
    gc_t = gc_all.T

    for hd in range(nheads):
        qs = slice(hd * dk, (hd + 1) * dk)
        q = qkv[:, qs]
        k = qkv[:, nqk + hd * dk:nqk + (hd + 1) * dk]
        v = qkv[:, 2 * nqk + hd * dv:2 * nqk + (hd + 1) * dv]
        q = q * lax.rsqrt(jnp.sum(q * q, axis=-1, keepdims=True) + EPS) * (dk ** -0.5)
        k = k * lax.rsqrt(jnp.sum(k * k, axis=-1, keepdims=True) + EPS)
        beta = beta_all[:, hd:hd + 1]
        gc = gc_all[:, hd:hd + 1]
        gc_row = gc_t[hd:hd + 1, :]
        gc_last = gc_all[c - 1:c, hd:hd + 1]
        decay = jnp.where(tril, jnp.exp(jnp.where(tril, gc - gc_row, 0.0)), 0.0)
        a_mat = jnp.where(tril_strict, beta * _dot_nt(k, k) * decay, 0.0)
        t_inv = jnp.where(row == col, 1.0, 0.0) - jnp.where((row >> 1) == (col >> 1), a_mat, 0.0)
        for lvl in range(1, int(math.log2(c))):
            m_blk = jnp.where(jnp.logical_and((row >> (lvl + 1)) == (col >> (lvl + 1)),
                                              (row >> lvl) != (col >> lvl)), a_mat, 0.0)
            t_inv = t_inv - _dot(_dot(t_inv, m_blk), t_inv)
        e_gc = jnp.exp(gc)
        u = _dot(t_inv, v * beta)
        w = _dot(t_inv, k * (beta * e_gc))
        qk = _dot_nt(q, k) * decay
        q_dec = q * e_gc
        k_dec = k * jnp.exp(gc_last - gc)
        s = s_ref[hd]
        v_new = u - _dot(w, s)
        o_ref[:, hd * dv:(hd + 1) * dv] = _dot(q_dec, s) + _dot(qk, v_new)
        s_ref[hd] = s * jnp.exp(gc_last) + _dot(k_dec.T, v_new)


def _gdn_scan(proj, conv_w, a_log_pad, dt_bias_pad):
    L = proj.shape[0]
    c = GDN_CHUNK
    nqkv = 2 * GDN_HEADS * GDN_DK + GDN_HEADS * GDN_DV
    nv = GDN_HEADS * GDN_DV
    ba_block = (nqkv + nv) // (2 * LANES)
    return pl.pallas_call(
        functools.partial(_gdn_scan_kernel, nheads=GDN_HEADS, dk=GDN_DK, dv=GDN_DV, c=c),
        grid=(L // c,),
        in_specs=[pl.BlockSpec((c, nqkv), lambda n: (n, 0)),
                  pl.BlockSpec((c, 2 * LANES), lambda n: (n, ba_block)),
                  _const_spec((GDN_CONV, nqkv)), _const_spec((1, LANES)), _const_spec((1, LANES))],
        out_specs=pl.BlockSpec((c, nv), lambda n: (n, 0)),
        out_shape=jax.ShapeDtypeStruct((L, nv), F32),
        scratch_shapes=[pltpu.VMEM((c + 8, nqkv), F32),
                        pltpu.VMEM((GDN_HEADS, GDN_DK, GDN_DV), F32)],
        compiler_params=_params("arbitrary"),
        name="gdn_scan",
    )(proj, proj, conv_w, a_log_pad, dt_bias_pad)


def _gdn_out_kernel(o_ref, gate_ref, og_ref, w_ref, h_ref, y_ref, *, nheads, dv):
    parts = []
    for hd in range(nheads):
        cs = slice(hd * dv, (hd + 1) * dv)
        on = _rmsnorm_rows(o_ref[:, cs], og_ref[...])
        parts.append((on * _silu(gate_ref[:, cs])).astype(MXU_DTYPE))
    y_ref[...] = h_ref[...] + jnp.dot(jnp.concatenate(parts, axis=1), w_ref[...],
                                      preferred_element_type=F32)


def _gdn_out(o, proj, o_norm_g, w_out, h, tm=512):
    L, D = h.shape
    nv = GDN_HEADS * GDN_DV
    gate_block = (2 * GDN_HEADS * GDN_DK + nv) // nv
    return pl.pallas_call(
        functools.partial(_gdn_out_kernel, nheads=GDN_HEADS, dv=GDN_DV),
        grid=(L // tm,),
        in_specs=[pl.BlockSpec((tm, nv), lambda i: (i, 0)),
                  pl.BlockSpec((tm, nv), lambda i: (i, gate_block)),
                  _const_spec((1, GDN_DV)), _const_spec((nv, D)),
                  pl.BlockSpec((tm, D), lambda i: (i, 0))],
        out_specs=pl.BlockSpec((tm, D), lambda i: (i, 0)),
        out_shape=jax.ShapeDtypeStruct((L, D), F32),
        compiler_params=_params("arbitrary"),
        name="gdn_out",
    )(o, proj, o_norm_g.reshape(1, GDN_DV), w_out, h)


def _gdn_layer(h, g, w_in, conv_w, a_log, dt_bias, o_norm_g, w_out):
    L, D = h.shape
    nmain = 2 * GDN_HEADS * GDN_DK + 2 * GDN_HEADS * GDN_DV
    zpad = jnp.zeros((D, LANES - GDN_HEADS), w_in.dtype)
    w_cat = jnp.concatenate([w_in[:, :nmain], w_in[:, nmain:nmain + GDN_HEADS], zpad,
                             w_in[:, nmain + GDN_HEADS:], zpad], axis=1).astype(MXU_DTYPE)
    proj = _norm_proj(h, g, w_cat, tn=w_cat.shape[1] // 2)
    vpad = lambda a: jnp.concatenate([a.astype(F32), jnp.zeros((LANES - GDN_HEADS,), F32)]).reshape(1, LANES)
    o = _gdn_scan(proj, conv_w, vpad(a_log), vpad(dt_bias))
    return _gdn_out(o, proj, o_norm_g, w_out.astype(MXU_DTYPE), h)


def kernel(x, norm_mix_g, norm_mlp_g, final_g, mlp_w1, mlp_w2, dsa_w_in, dsa_idx_k_g, dsa_idx_k_b, dsa_w_out, conv_w_pw1, conv_b_pw1, conv_w_dw, conv_b_dw, conv_ln_g, conv_ln_b, conv_w_pw2, conv_b_pw2, mla_w_in, mla_q_norm_g, mla_w_uq, mla_kv_norm_g, mla_w_ukv, mla_w_out, gdn_w_in, gdn_conv_w, gdn_a_log, gdn_dt_bias, gdn_o_norm_g, gdn_w_out):
    b, L, D = x.shape
    depth = norm_mix_g.shape[0]
    outs = []
    for bi in range(b):
        h = x[bi]
        for i in range(depth):
            m = i % 4
            jl = i // 4
            g = norm_mix_g[i]
            if m == 0:
                h = _dsa_layer(h, g, dsa_w_in[jl], dsa_idx_k_g[jl], dsa_idx_k_b[jl], dsa_w_out[jl])
            elif m == 1:
                h = _conv_layer(h, g, conv_w_pw1[jl], conv_b_pw1[jl], conv_w_dw[jl], conv_b_dw[jl],
                                conv_ln_g[jl], conv_ln_b[jl], conv_w_pw2[jl], conv_b_pw2[jl])
            elif m == 2:
                h = _mla_layer(h, g, mla_w_in[jl], mla_q_norm_g[jl], mla_w_uq[jl], mla_kv_norm_g[jl],
                               mla_w_ukv[jl], mla_w_out[jl])
            else:
                h = _gdn_layer(h, g, gdn_w_in[jl], gdn_conv_w[jl], gdn_a_log[jl], gdn_dt_bias[jl],
                               gdn_o_norm_g[jl], gdn_w_out[jl])
            h = _mlp(h, norm_mlp_g[i], mlp_w1[i].astype(MXU_DTYPE), mlp_w2[i].astype(MXU_DTYPE),
                     final_g, final_norm=(i == depth - 1))
        outs.append(h)
    return jnp.stack(outs)
```

```python
import functools
import math

import jax
import jax.numpy as jnp
import numpy as np
from jax import lax
from jax.experimental import pallas as pl
from jax.experimental.pallas import tpu as pltpu

F32 = jnp.float32
MXU_DTYPE = jnp.bfloat16
EPS = 1e-6
LANES = 128
VMEM_LIMIT_BYTES = 56 * 1024 * 1024
NEG_BIG = -1e30
INT_MIN = -(2 ** 31)

Q_BLOCK = 128
IDX_HEADS = 8
IDX_DIM = 64
TOPK_MAX = 256
DSA_HEADS = 8
CONV_WIDTH = 31
MLA_HEADS = 16
MLA_Q_LORA = 384
MLA_KV_LORA = 256
MLA_NOPE = 64
MLA_ROPE = 32
MLA_V = 64
ROPE_THETA = 10000.0
GDN_HEADS = 8
GDN_DK = 128
GDN_DV = 128
GDN_CONV = 4
GDN_CHUNK = 64


def _params(*sem):
    return pltpu.CompilerParams(dimension_semantics=sem, vmem_limit_bytes=VMEM_LIMIT_BYTES)


def _dot(a, b):
    return jnp.dot(a.astype(MXU_DTYPE), b.astype(MXU_DTYPE), preferred_element_type=F32)


def _dot_nt(a, b):
    return lax.dot_general(a.astype(MXU_DTYPE), b.astype(MXU_DTYPE),
                           (((1,), (1,)), ((), ())), preferred_element_type=F32)


def _rmsnorm_rows(x, g):
    return x * lax.rsqrt(jnp.mean(x * x, axis=-1, keepdims=True) + EPS) * g


def _sigmoid(x):
    return 1.0 / (1.0 + jnp.exp(-x))


def _silu(x):
    return x * _sigmoid(x)


def _const_spec(shape):
    return pl.BlockSpec(shape, lambda *_: (0,) * len(shape))


LOG2E = math.log2(math.e)


def _causal_steps(n_q, tq, tk):
    qi, kj = [], []
    for i in range(n_q):
        for j in range(((i + 1) * tq - 1) // tk + 1):
            qi.append(i)
            kj.append(j)
    return jnp.asarray(np.array(qi, np.int32)), jnp.asarray(np.array(kj, np.int32))


def _flash_heads(q_ref, k_ref, v_ref, m_ref, l_ref, acc_ref, bias, nheads, v_cols):
    tk = k_ref.shape[0]
    ones = jnp.ones((tk, LANES), MXU_DTYPE)

    def logits(h):
        cs = slice(h * LANES, (h + 1) * LANES)
        return _dot_nt(q_ref[:, cs], k_ref[:, cs])

    s_next = logits(0)
    for h in range(nheads):
        cs = slice(h * LANES, (h + 1) * LANES)
        s = s_next
        if h + 1 < nheads:
            s_next = logits(h + 1)
        s = s + bias
        m_prev = m_ref[h]
        m_next = jnp.maximum(m_prev, jnp.max(s, axis=1, keepdims=True))
        alpha = jnp.exp2(m_prev - m_next)
        p = jnp.exp2(s - m_next[:, 0:1]).astype(MXU_DTYPE)
        v_ext = jnp.concatenate([v_ref[:, v_cols(h)].astype(MXU_DTYPE), ones], axis=1)
        pv = jnp.dot(p, v_ext, preferred_element_type=F32)
        l_ref[h] = alpha * l_ref[h] + pv[:, LANES:]
        m_ref[h] = m_next
        acc_ref[:, cs] = alpha * acc_ref[:, cs] + pv[:, :LANES]


def _flash_finish(o_ref, l_ref, acc_ref, nheads):
    for h in range(nheads):
        cs = slice(h * LANES, (h + 1) * LANES)
        o_ref[:, cs] = (acc_ref[:, cs] / l_ref[h]).astype(o_ref.dtype)


def _mlp_kernel(h_ref, g_ref, w1_ref, w2_ref, gf_ref, o_ref, xn_ref, acc_ref, *, final_norm):
    f = pl.program_id(1)

    @pl.when(f == 0)
    def _():
        xn_ref[...] = _rmsnorm_rows(h_ref[...], g_ref[...]).astype(xn_ref.dtype)
        acc_ref[...] = jnp.zeros_like(acc_ref)

    a = jnp.dot(xn_ref[...], w1_ref[...], preferred_element_type=F32)
    a = jnp.square(jnp.maximum(a, 0.0))
    acc_ref[...] += _dot(a, w2_ref[...])

    @pl.when(f == pl.num_programs(1) - 1)
    def _():
        y = h_ref[...] + acc_ref[...]
        if final_norm:
            y = _rmsnorm_rows(y, gf_ref[...])
        o_ref[...] = y


def _mlp(h, g, w1, w2, gf, final_norm, tm=512, tf=1024):
    L, D = h.shape
    dff = w1.shape[1]
    return pl.pallas_call(
        functools.partial(_mlp_kernel, final_norm=final_norm),
        grid=(L // tm, dff // tf),
        in_specs=[
            pl.BlockSpec((tm, D), lambda i, f: (i, 0)),
            _const_spec((1, D)),
            pl.BlockSpec((D, tf), lambda i, f: (0, f)),
            pl.BlockSpec((tf, D), lambda i, f: (f, 0)),
            _const_spec((1, D)),
        ],
        out_specs=pl.BlockSpec((tm, D), lambda i, f: (i, 0)),
        out_shape=jax.ShapeDtypeStruct((L, D), F32),
        scratch_shapes=[pltpu.VMEM((tm, D), MXU_DTYPE), pltpu.VMEM((tm, D), F32)],
        compiler_params=_params("arbitrary", "arbitrary"),
        name="mlp",
    )(h, g.reshape(1, D), w1, w2, gf.reshape(1, D))


def _out_proj_kernel(a_ref, w_ref, b_ref, h_ref, o_ref):
    o_ref[...] = h_ref[...] + _dot(a_ref[...], w_ref[...]) + b_ref[...]


def _out_proj(a, w, b, h, tm=512):
    L, K = a.shape
    D = w.shape[1]
    return pl.pallas_call(
        _out_proj_kernel,
        grid=(L // tm,),
        in_specs=[
            pl.BlockSpec((tm, K), lambda i: (i, 0)),
            _const_spec((K, D)),
            _const_spec((1, D)),
            pl.BlockSpec((tm, D), lambda i: (i, 0)),
        ],
        out_specs=pl.BlockSpec((tm, D), lambda i: (i, 0)),
        out_shape=jax.ShapeDtypeStruct((L, D), F32),
        compiler_params=_params("arbitrary"),
        name="out_proj",
    )(a, w, b.reshape(1, D), h)


def _norm_proj_kernel(h_ref, g_ref, w_ref, o_ref, xn_ref):
    @pl.when(pl.program_id(1) == 0)
    def _():
        xn_ref[...] = _rmsnorm_rows(h_ref[...], g_ref[...]).astype(xn_ref.dtype)

    o_ref[...] = jnp.dot(xn_ref[...], w_ref[...], preferred_element_type=F32).astype(o_ref.dtype)


def _norm_proj(h, g, w, tn, out_dtype=F32, tm=512):
    L, D = h.shape
    N = w.shape[1]
    return pl.pallas_call(
        _norm_proj_kernel,
        grid=(L // tm, N // tn),
        in_specs=[
            pl.BlockSpec((tm, D), lambda i, j: (i, 0)),
            _const_spec((1, D)),
            pl.BlockSpec((D, tn), lambda i, j: (0, j)),
        ],
        out_specs=pl.BlockSpec((tm, tn), lambda i, j: (i, j)),
        out_shape=jax.ShapeDtypeStruct((L, N), out_dtype),
        scratch_shapes=[pltpu.VMEM((tm, D), MXU_DTYPE)],
        compiler_params=_params("arbitrary", "arbitrary"),
        name="norm_proj",
    )(h, g.reshape(1, D), w)


def _dsa_proj_kernel(h_ref, g_ref, w_ref, lg_ref, lb_ref,
                     q_ref, k_ref, v_ref, qi_ref, kia_ref, kib_ref, wi_ref, *, d, scale, wi_scale):
    xn = _rmsnorm_rows(h_ref[...], g_ref[...]).astype(MXU_DTYPE)

    def mm(c0, n):
        return jnp.dot(xn, w_ref[:, c0:c0 + n], preferred_element_type=F32)

    cw = 512
    for c in range(0, d, cw):
        q_ref[:, c:c + cw] = (mm(c, cw) * scale).astype(q_ref.dtype)
        k_ref[:, c:c + cw] = mm(d + c, cw).astype(k_ref.dtype)
        v_ref[:, c:c + cw] = mm(2 * d + c, cw).astype(v_ref.dtype)
    nqi = IDX_HEADS * IDX_DIM
    qi_ref[...] = mm(3 * d, nqi).astype(qi_ref.dtype)
    xa = mm(3 * d + nqi, LANES)
    xb = mm(3 * d + nqi + LANES, LANES)
    lane = lax.broadcasted_iota(jnp.int32, xa.shape, 1)

    def masked_ln(x, m, gain, bias):
        mu = jnp.sum(jnp.where(m, x, 0.0), axis=-1, keepdims=True) * (1.0 / IDX_DIM)
        dlt = jnp.where(m, x - mu, 0.0)
        var = jnp.sum(dlt * dlt, axis=-1, keepdims=True) * (1.0 / IDX_DIM)
        return dlt * lax.rsqrt(var + EPS) * gain + bias

    kia_ref[...] = masked_ln(xa, lane < IDX_DIM, lg_ref[0:1, :], lb_ref[0:1, :]).astype(kia_ref.dtype)
    kib_ref[...] = masked_ln(xb, lane >= IDX_DIM, lg_ref[1:2, :], lb_ref[1:2, :]).astype(kib_ref.dtype)
    wi_ref[...] = jnp.where(lane < IDX_HEADS, xb, 0.0) * wi_scale


def _dsa_proj(h, g, w_cat, lg2, lb2, tm=256):
    L, D = h.shape
    N = w_cat.shape[1]
    nqi = IDX_HEADS * IDX_DIM
    row = lambda n: pl.BlockSpec((tm, n), lambda i: (i, 0))
    return pl.pallas_call(
        functools.partial(_dsa_proj_kernel, d=D, scale=(D // DSA_HEADS) ** -0.5 * LOG2E,
                          wi_scale=IDX_HEADS ** -0.5 * IDX_DIM ** -0.5),
        grid=(L // tm,),
        in_specs=[row(D), _const_spec((1, D)), _const_spec((D, N)),
                  _const_spec((2, LANES)), _const_spec((2, LANES))],
        out_specs=[row(D), row(D), row(D), row(nqi), row(LANES), row(LANES), row(LANES)],
        out_shape=[jax.ShapeDtypeStruct((L, D), MXU_DTYPE)] * 3
        + [jax.ShapeDtypeStruct((L, nqi), MXU_DTYPE)]
        + [jax.ShapeDtypeStruct((L, LANES), MXU_DTYPE)] * 2
        + [jax.ShapeDtypeStruct((L, LANES), F32)],
        compiler_params=_params("arbitrary"),
        name="dsa_proj",
    )(h, g.reshape(1, D), w_cat, lg2, lb2)


def _dsa_attn_kernel(qi_s, kj_s, q_ref, qi_ref, wi_ref, kia_ref, kib_ref, k_ref, v_ref, o_ref,
                     keys_ref, cand_ref, thr_ref, cut_ref, m_ref, l_ref, acc_ref,
                     *, tq, tk, topk, nheads, rb):
    step = pl.program_id(0)
    i = qi_s[step]
    j = kj_s[step]
    n_kt = ((i + 1) * tq + tk - 1) // tk
    q0 = i * tq

    @pl.when(j == 0)
    def _select():
        qpos = q0 + lax.broadcasted_iota(jnp.int32, (tq, tk), 0)
        lane_pos = lax.broadcasted_iota(jnp.int32, (tq, tk), 1)

        def score_tile(c, carry):
            k0 = pl.multiple_of(c * tk, tk)
            ka = kia_ref[pl.ds(k0, tk), :]
            kb = kib_ref[pl.ds(k0, tk), :]
            acc = jnp.zeros((tq, tk), F32)
            for p in range(IDX_HEADS // 2):
                lhs = qi_ref[:, p * LANES:(p + 1) * LANES]
                sa = jnp.maximum(_dot_nt(lhs, ka), 0.0)
                sb = jnp.maximum(_dot_nt(lhs, kb), 0.0)
                acc = acc + wi_ref[:, 2 * p:2 * p + 1] * sa
                acc = acc + wi_ref[:, 2 * p + 1:2 * p + 2] * sb
            bits = pltpu.bitcast(acc, jnp.int32)
            key = bits ^ ((bits >> 31) & jnp.int32(0x7FFFFFFF))
            key = jnp.where(k0 + lane_pos <= qpos, key, jnp.int32(INT_MIN))
            keys_ref[c] = key
            return carry

        lax.fori_loop(0, n_kt, score_tile, 0)

        def count(pred):
            def body(c, acc):
                parts = []
                for r0 in range(0, tq, rb):
                    cand = cand_ref[r0:r0 + rb, :]
                    part = jnp.zeros((rb, LANES), jnp.int32)
                    for t in range(tk // LANES):
                        kk = keys_ref[c, r0:r0 + rb, t * LANES:(t + 1) * LANES]
                        part = part + pred(kk, cand, c * tk + t * LANES, r0).astype(jnp.int32)
                    parts.append(part)
                return acc + jnp.concatenate(parts, axis=0)
            acc = lax.fori_loop(0, n_kt, body, jnp.zeros((tq, LANES), jnp.int32))
            return jnp.sum(acc.astype(F32), axis=1, keepdims=True)

        def set_cand(x):
            cand_ref[...] = jnp.broadcast_to(x, (tq, LANES))

        def bit_step(b, carry):
            thr, n_ge = carry
            cand = thr + lax.shift_left(jnp.int32(1), 31 - b)
            set_cand(cand)
            cnt = count(lambda kk, cd, k0, r0: kk >= cd)
            keep = cnt >= topk
            return jnp.where(keep, cand, thr), jnp.where(keep, cnt, n_ge)

        thr, n_ge = lax.fori_loop(
            0, 32, bit_step,
            (jnp.full((tq, 1), INT_MIN, jnp.int32),
             jnp.broadcast_to((n_kt * tk).astype(F32), (tq, 1))))
        thr_ref[...] = jnp.broadcast_to(thr, (tq, LANES))
        cut_ref[...] = jnp.broadcast_to(jnp.where(thr == INT_MIN, -1, jnp.int32(2 ** 30)), (tq, LANES))
        excess = jnp.max(jnp.where(thr == INT_MIN, 0.0, n_ge - topk))

        @pl.when(excess > 0.0)
        def _ties():
            lane_idx = lax.broadcasted_iota(jnp.int32, (rb, LANES), 1)
            set_cand(thr)
            need = topk - count(lambda kk, cd, k0, r0: kk > cd)

            def idx_step(b, cut):
                cand = cut + lax.shift_left(jnp.int32(1), 30 - b)
                set_cand(cand)
                cnt = count(lambda kk, cd, k0, r0: jnp.logical_and(
                    kk == thr_ref[r0:r0 + rb, :], k0 + lane_idx < cd))
                return jnp.where(cnt < need, cand, cut)

            cut = lax.fori_loop(0, 31, idx_step, jnp.zeros((tq, 1), jnp.int32))
            cut = jnp.where(thr == INT_MIN, -1, cut)
            cut_ref[...] = jnp.broadcast_to(cut, (tq, LANES))

        m_ref[...] = jnp.full_like(m_ref, NEG_BIG)
        l_ref[...] = jnp.zeros_like(l_ref)
        acc_ref[...] = jnp.zeros_like(acc_ref)

    keyt = keys_ref[j]
    thr = thr_ref[:, 0:1]
    cut = cut_ref[:, 0:1]
    kidx = j * tk + lax.broadcasted_iota(jnp.int32, (tq, tk), 1)
    sel = jnp.logical_or(keyt > thr, jnp.logical_and(keyt == thr, kidx <= cut))
    bias = jnp.where(sel, 0.0, NEG_BIG)
    _flash_heads(q_ref, k_ref, v_ref, m_ref, l_ref, acc_ref, bias, nheads,
                 lambda h: slice(h * LANES, (h + 1) * LANES))

    @pl.when(j == n_kt - 1)
    def _finish():
        _flash_finish(o_ref, l_ref, acc_ref, nheads)


def _dsa_attn(q, k, v, qi, kia, kib, wi, topk, tq=256, tk=512, rb=64):
    L, D = q.shape
    nheads = DSA_HEADS
    assert D == nheads * LANES
    nqi = qi.shape[1]
    qi_s, kj_s = _causal_steps(L // tq, tq, tk)
    qmap = lambda s, qs, ks: (qs[s], 0)
    kmap = lambda s, qs, ks: (ks[s], 0)
    cmap = lambda s, qs, ks: (0, 0)
    grid_spec = pltpu.PrefetchScalarGridSpec(
        num_scalar_prefetch=2,
        grid=(qi_s.shape[0],),
        in_specs=[
            pl.BlockSpec((tq, D), qmap),
            pl.BlockSpec((tq, nqi), qmap),
            pl.BlockSpec((tq, LANES), qmap),
            pl.BlockSpec((L, LANES), cmap),
            pl.BlockSpec((L, LANES), cmap),
            pl.BlockSpec((tk, D), kmap),
            pl.BlockSpec((tk, D), kmap),
        ],
        out_specs=pl.BlockSpec((tq, D), qmap),
        scratch_shapes=[
            pltpu.VMEM((L // tk, tq, tk), jnp.int32),
            pltpu.VMEM((tq, LANES), jnp.int32),
            pltpu.VMEM((tq, LANES), jnp.int32),
            pltpu.VMEM((tq, LANES), jnp.int32),
            pltpu.VMEM((nheads, tq, LANES), F32),
            pltpu.VMEM((nheads, tq, LANES), F32),
            pltpu.VMEM((tq, D), F32),
        ],
    )
    return pl.pallas_call(
        functools.partial(_dsa_attn_kernel, tq=tq, tk=tk, topk=topk, nheads=nheads, rb=rb),
        grid_spec=grid_spec,
        out_shape=jax.ShapeDtypeStruct((L, D), MXU_DTYPE),
        compiler_params=_params("arbitrary"),
        name="dsa_attn",
    )(qi_s, kj_s, q, qi, wi, kia, kib, k, v)


def _dsa_layer(h, g, w_in, idx_k_g, idx_k_b, w_out):
    L, D = h.shape
    nqi = IDX_HEADS * IDX_DIM
    o3 = 3 * D
    o4 = o3 + nqi
    o5 = o4 + IDX_DIM
    w_ki = w_in[:, o4:o5]
    w_wi = w_in[:, o5:]
    zpad = lambda n: jnp.zeros((D, n), w_in.dtype)
    w_cat = jnp.concatenate(
        [w_in[:, :o4],
         w_ki, w_wi, zpad(LANES - IDX_DIM - IDX_HEADS),
         w_wi, zpad(LANES - IDX_DIM - IDX_HEADS), w_ki], axis=1).astype(MXU_DTYPE)
    z = jnp.zeros((IDX_DIM,), F32)
    lg2 = jnp.stack([jnp.concatenate([idx_k_g, z]), jnp.concatenate([z, idx_k_g])])
    lb2 = jnp.stack([jnp.concatenate([idx_k_b, z]), jnp.concatenate([z, idx_k_b])])
    q, k, v, qi, kia, kib, wi = _dsa_proj(h, g, w_cat, lg2, lb2)
    o = _dsa_attn(q, k, v, qi, kia, kib, wi, topk=min(TOPK_MAX, L // 4))
    return _out_proj(o, w_out.astype(MXU_DTYPE), jnp.zeros((D,), F32), h)


def _glu_proj_kernel(h_ref, g_ref, w_ref, b_ref, u_ref, *, d):
    xn = _rmsnorm_rows(h_ref[...], g_ref[...]).astype(MXU_DTYPE)
    cw = 512
    for c in range(0, d, cw):
        a = jnp.dot(xn, w_ref[:, c:c + cw], preferred_element_type=F32) + b_ref[:, c:c + cw]
        gt = jnp.dot(xn, w_ref[:, d + c:d + c + cw], preferred_element_type=F32) + b_ref[:, d + c:d + c + cw]
        u_ref[:, c:c + cw] = a * _sigmoid(gt)


def _glu_proj(h, g, w, b, tm=512):
    L, D = h.shape
    return pl.pallas_call(
        functools.partial(_glu_proj_kernel, d=D),
        grid=(L // tm,),
        in_specs=[pl.BlockSpec((tm, D), lambda i: (i, 0)), _const_spec((1, D)),
                  _const_spec((D, 2 * D)), _const_spec((1, 2 * D))],
        out_specs=pl.BlockSpec((tm, D), lambda i: (i, 0)),
        out_shape=jax.ShapeDtypeStruct((L, D), F32),
        compiler_params=_params("arbitrary"),
        name="conv_glu_proj",
    )(h, g.reshape(1, D), w, b.reshape(1, 2 * D))


CONV_HALO = 32


def _conv_out_kernel(u_ref, up_ref, wdw_ref, bdw_ref, lg_ref, lb_ref, w2_ref, b2_ref, h_ref, o_ref,
                     x_ref, *, tm):
    i = pl.program_id(0)
    x_ref[CONV_HALO:, :] = u_ref[...]
    x_ref[:CONV_HALO, :] = jnp.where(i > 0, up_ref[...], 0.0)
    off = CONV_HALO - (CONV_WIDTH - 1)
    acc = jnp.zeros(u_ref.shape, F32)
    for t in range(CONV_WIDTH):
        acc = acc + wdw_ref[t:t + 1, :] * x_ref[off + t:off + t + tm, :]
    y = acc + bdw_ref[...]
    mu = jnp.mean(y, axis=-1, keepdims=True)
    dlt = y - mu
    var = jnp.mean(dlt * dlt, axis=-1, keepdims=True)
    y = _silu(dlt * lax.rsqrt(var + EPS) * lg_ref[...] + lb_ref[...])
    o_ref[...] = h_ref[...] + _dot(y, w2_ref[...]) + b2_ref[...]


def _conv_out(u, w_dw, b_dw, ln_g, ln_b, w2, b2, h, tm=256):
    L, D = u.shape
    r = tm // CONV_HALO
    vec = lambda a: a.reshape(1, D)
    return pl.pallas_call(
        functools.partial(_conv_out_kernel, tm=tm),
        grid=(L // tm,),
        in_specs=[
            pl.BlockSpec((tm, D), lambda i: (i, 0)),
            pl.BlockSpec((CONV_HALO, D), lambda i: (jnp.maximum(i * r - 1, 0), 0)),
            _const_spec((CONV_WIDTH, D)),
            _const_spec((1, D)), _const_spec((1, D)), _const_spec((1, D)),
            _const_spec((D, D)), _const_spec((1, D)),
            pl.BlockSpec((tm, D), lambda i: (i, 0)),
        ],
        out_specs=pl.BlockSpec((tm, D), lambda i: (i, 0)),
        out_shape=jax.ShapeDtypeStruct((L, D), F32),
        scratch_shapes=[pltpu.VMEM((tm + CONV_HALO, D), F32)],
        compiler_params=_params("arbitrary"),
        name="conv_out",
    )(u, u, w_dw, vec(b_dw), vec(ln_g), vec(ln_b), w2, vec(b2), h)


def _conv_layer(h, g, w_pw1, b_pw1, w_dw, b_dw, ln_g, ln_b, w_pw2, b_pw2):
    u = _glu_proj(h, g, w_pw1.astype(MXU_DTYPE), b_pw1)
    return _conv_out(u, w_dw, b_dw, ln_g, ln_b, w_pw2.astype(MXU_DTYPE), b_pw2, h)


def _mla_prep_kernel(h_ref, g_ref, w1_ref, qg_ref, kvg_ref, wqa_ref, wqb_ref, wka_ref, wv_ref,
                     c_ref, s_ref, q_ref, k_ref, v_ref, *, scale):
    xn = _rmsnorm_rows(h_ref[...], g_ref[...]).astype(MXU_DTYPE)
    proj = jnp.dot(xn, w1_ref[...], preferred_element_type=F32)
    o1 = MLA_Q_LORA
    o2 = o1 + MLA_KV_LORA
    cq = _rmsnorm_rows(proj[:, :o1], qg_ref[...]).astype(MXU_DTYPE)
    ckv = _rmsnorm_rows(proj[:, o1:o2], kvg_ref[...]).astype(MXU_DTYPE)
    cos = c_ref[...]
    sin = s_ref[...]
    kr = proj[:, o2:o2 + LANES] * cos + proj[:, o2 + LANES:o2 + 2 * LANES] * sin
    for hd in range(MLA_HEADS):
        cs = slice(hd * LANES, (hd + 1) * LANES)
        qa = jnp.dot(cq, wqa_ref[:, cs], preferred_element_type=F32)
        qb = jnp.dot(cq, wqb_ref[:, cs], preferred_element_type=F32)
        q_ref[:, cs] = ((qa * cos + qb * sin) * scale).astype(q_ref.dtype)
        ka = jnp.dot(ckv, wka_ref[:, cs], preferred_element_type=F32)
        k_ref[:, cs] = (ka + kr).astype(k_ref.dtype)
    v_ref[...] = jnp.dot(ckv, wv_ref[...], preferred_element_type=F32).astype(v_ref.dtype)


def _mla_prep(h, g, w1, qg, kvg, wqa, wqb, wka, wv, ctab, stab, tm=256):
    L, D = h.shape
    HP = MLA_HEADS * LANES
    NV = MLA_HEADS * MLA_V
    row = lambda n: pl.BlockSpec((tm, n), lambda i: (i, 0))
    full = lambda a: _const_spec(a.shape)
    return pl.pallas_call(
        functools.partial(_mla_prep_kernel, scale=(MLA_NOPE + MLA_ROPE) ** -0.5 * LOG2E),
        grid=(L // tm,),
        in_specs=[row(D), _const_spec((1, D)), full(w1), _const_spec((1, MLA_Q_LORA)),
                  _const_spec((1, MLA_KV_LORA)), full(wqa), full(wqb), full(wka), full(wv),
                  row(LANES), row(LANES)],
        out_specs=[row(HP), row(HP), row(NV)],
        out_shape=[jax.ShapeDtypeStruct((L, HP), MXU_DTYPE), jax.ShapeDtypeStruct((L, HP), MXU_DTYPE),
                   jax.ShapeDtypeStruct((L, NV), MXU_DTYPE)],
        compiler_params=_params("arbitrary"),
        name="mla_prep",
    )(h, g.reshape(1, D), w1, qg.reshape(1, -1), kvg.reshape(1, -1), wqa, wqb, wka, wv, ctab, stab)


def _mla_attn_kernel(qi_s, kj_s, q_ref, k_ref, v_ref, o_ref, m_ref, l_ref, acc_ref, *, tq, tk, nheads):
    step = pl.program_id(0)
    i = qi_s[step]
    j = kj_s[step]
    n_kt = ((i + 1) * tq + tk - 1) // tk

    @pl.when(j == 0)
    def _init():
        m_ref[...] = jnp.full_like(m_ref, NEG_BIG)
        l_ref[...] = jnp.zeros_like(l_ref)
        acc_ref[...] = jnp.zeros_like(acc_ref)

    qpos = i * tq + lax.broadcasted_iota(jnp.int32, (tq, tk), 0)
    kpos = j * tk + lax.broadcasted_iota(jnp.int32, (tq, tk), 1)
    bias = jnp.where(kpos <= qpos, 0.0, NEG_BIG)
    _flash_heads(q_ref, k_ref, v_ref, m_ref, l_ref, acc_ref, bias, nheads,
                 lambda h: slice((h // 2) * LANES, (h // 2 + 1) * LANES))

    @pl.when(j == n_kt - 1)
    def _finish():
        _flash_finish(o_ref, l_ref, acc_ref, nheads)


def _mla_attn(q, k, v, tq=256, tk=512):
    L, HP = q.shape
    NV = v.shape[1]
    qi_s, kj_s = _causal_steps(L // tq, tq, tk)
    qmap = lambda s, qs, ks: (qs[s], 0)
    kmap = lambda s, qs, ks: (ks[s], 0)
    grid_spec = pltpu.PrefetchScalarGridSpec(
        num_scalar_prefetch=2,
        grid=(qi_s.shape[0],),
        in_specs=[pl.BlockSpec((tq, HP), qmap),
                  pl.BlockSpec((tk, HP), kmap), pl.BlockSpec((tk, NV), kmap)],
        out_specs=pl.BlockSpec((tq, HP), qmap),
        scratch_shapes=[pltpu.VMEM((MLA_HEADS, tq, LANES), F32),
                        pltpu.VMEM((MLA_HEADS, tq, LANES), F32),
                        pltpu.VMEM((tq, HP), F32)],
    )
    return pl.pallas_call(
        functools.partial(_mla_attn_kernel, tq=tq, tk=tk, nheads=MLA_HEADS),
        grid_spec=grid_spec,
        out_shape=jax.ShapeDtypeStruct((L, HP), MXU_DTYPE),
        compiler_params=_params("arbitrary"),
        name="mla_attn",
    )(qi_s, kj_s, q, k, v)


def _place_heads(w, nheads, src_w, lane0, swap_half=0):
    K = w.shape[0]
    w3 = w.reshape(K, nheads, src_w)
    if swap_half:
        w3 = jnp.concatenate([w3[..., swap_half:], w3[..., :swap_half]], axis=-1)
    out = jnp.zeros((K, nheads, LANES), w.dtype)
    out = out.at[:, :, lane0:lane0 + src_w].set(w3)
    return out.reshape(K, nheads * LANES)


def _mla_layer(h, g, w_in, q_norm_g, w_uq, kv_norm_g, w_ukv, w_out):
    L, D = h.shape
    half = MLA_ROPE // 2
    o2 = MLA_Q_LORA + MLA_KV_LORA
    w_kr = w_in[:, o2:]
    w1 = jnp.concatenate(
        [w_in[:, :o2], _place_heads(w_kr, 1, MLA_ROPE, MLA_NOPE),
         _place_heads(w_kr, 1, MLA_ROPE, MLA_NOPE, swap_half=half)], axis=1).astype(MXU_DTYPE)
    uq = w_uq.reshape(MLA_Q_LORA, MLA_HEADS, MLA_NOPE + MLA_ROPE)
    uq_rope = uq[..., MLA_NOPE:].reshape(MLA_Q_LORA, MLA_HEADS * MLA_ROPE)
    wqa = _place_heads(w_uq, MLA_HEADS, MLA_NOPE + MLA_ROPE, 0).astype(MXU_DTYPE)
    wqb = _place_heads(uq_rope, MLA_HEADS, MLA_ROPE, MLA_NOPE, swap_half=half).astype(MXU_DTYPE)
    ukv = w_ukv.reshape(MLA_KV_LORA, MLA_HEADS, MLA_NOPE + MLA_V)
    wka = _place_heads(ukv[..., :MLA_NOPE].reshape(MLA_KV_LORA, -1), MLA_HEADS, MLA_NOPE, 0).astype(MXU_DTYPE)
    wv = ukv[..., MLA_NOPE:].reshape(MLA_KV_LORA, MLA_HEADS * MLA_V).astype(MXU_DTYPE)
    pos = jnp.arange(L, dtype=F32)
    inv_freq = ROPE_THETA ** (-jnp.arange(0, MLA_ROPE, 2, dtype=F32) / MLA_ROPE)
    ang = pos[:, None] * inv_freq[None, :]
    cos, sin = jnp.cos(ang), jnp.sin(ang)
    zeros = jnp.zeros((L, LANES - MLA_NOPE - MLA_ROPE), F32)
    ctab = jnp.concatenate([jnp.ones((L, MLA_NOPE), F32), cos, cos, zeros], axis=1)
    stab = jnp.concatenate([jnp.zeros((L, MLA_NOPE), F32), -sin, sin, zeros], axis=1)
    q, k, v = _mla_prep(h, g, w1, q_norm_g, kv_norm_g, wqa, wqb, wka, wv, ctab, stab)
    o = _mla_attn(q, k, v)
    wo = w_out.reshape(MLA_HEADS // 2, 2, MLA_V, D)
    zo = jnp.zeros_like(wo[:, 0])
    wo_pad = jnp.stack([jnp.concatenate([wo[:, 0], zo], axis=1),
                        jnp.concatenate([zo, wo[:, 1]], axis=1)], axis=1)
    wo_pad = wo_pad.reshape(MLA_HEADS * LANES, D).astype(MXU_DTYPE)
    return _out_proj(o, wo_pad, jnp.zeros((D,), F32), h)


def _gdn_scan_kernel(qkv_ref, ba_ref, cw_ref, alog_ref, dtb_ref, o_ref, x_ref, s_ref,
                     *, nheads, dk, dv, c):
    n = pl.program_id(0)
    halo = 8
    nqk = nheads * dk

    @pl.when(n == 0)
    def _init():
        x_ref[:halo, :] = jnp.zeros((halo, x_ref.shape[1]), F32)
        s_ref[...] = jnp.zeros_like(s_ref)

    x_ref[halo:, :] = qkv_ref[...]
    off = halo - (GDN_CONV - 1)
    acc = jnp.zeros(qkv_ref.shape, F32)
    for t in range(GDN_CONV):
        acc = acc + cw_ref[t:t + 1, :] * x_ref[off + t:off + t + c, :]
    x_ref[:halo, :] = qkv_ref[c - halo:, :]
    qkv = _silu(acc)

    beta_all = _sigmoid(ba_ref[:, :LANES])
    a_raw = ba_ref[:, LANES:] + dtb_ref[...]
    softplus = jnp.maximum(a_raw, 0.0) + jnp.log(1.0 + jnp.exp(-jnp.abs(a_raw)))
    g_all = -jnp.exp(alog_ref[...]) * softplus
    row = lax.broadcasted_iota(jnp.int32, (c, c), 0)
    col = lax.broadcasted_iota(jnp.int32, (c, c), 1)
    tril = row >= col
    tril_strict = row > col
    gc_all = jnp.dot(tril.astype(F32), g_all, preferred_element_type=F32,
                     precision=lax.Precision.HIGHEST)
    gc_t = gc_all.T

    heads = range(nheads)
    q = [qkv[:, hd * dk:(hd + 1) * dk] for hd in heads]
    k = [qkv[:, nqk + hd * dk:nqk + (hd + 1) * dk] for hd in heads]
    v = [qkv[:, 2 * nqk + hd * dv:2 * nqk + (hd + 1) * dv] for hd in heads]
    q = [x * lax.rsqrt(jnp.sum(x * x, axis=-1, keepdims=True) + EPS) * (dk ** -0.5) for x in q]
    k = [x * lax.rsqrt(jnp.sum(x * x, axis=-1, keepdims=True) + EPS) for x in k]
    beta = [beta_all[:, hd:hd + 1] for hd in heads]
    gc = [gc_all[:, hd:hd + 1] for hd in heads]
    gc_last = [gc_all[c - 1:c, hd:hd + 1] for hd in heads]
    decay = [jnp.where(tril, jnp.exp(jnp.where(tril, gc[hd] - gc_t[hd:hd + 1, :], 0.0)), 0.0)
             for hd in heads]
    kk = [_dot_nt(k[hd], k[hd]) for hd in heads]
    qk = [_dot_nt(q[hd], k[hd]) for hd in heads]
    a_mat = [jnp.where(tril_strict, beta[hd] * kk[hd] * decay[hd], 0.0) for hd in heads]
    eye = jnp.where(row == col, 1.0, 0.0)
    t_inv = [eye - jnp.where((row >> 1) == (col >> 1), a_mat[hd], 0.0) for hd in heads]
    for lvl in range(1, int(math.log2(c))):
        blk_mask = jnp.logical_and((row >> (lvl + 1)) == (col >> (lvl + 1)), (row >> lvl) != (col >> lvl))
        tm = [_dot(t_inv[hd], jnp.where(blk_mask, a_mat[hd], 0.0)) for hd in heads]
        tmt = [_dot(tm[hd], t_inv[hd]) for hd in heads]
        t_inv = [t_inv[hd] - tmt[hd] for hd in heads]
    e_gc = [jnp.exp(gc[hd]) for hd in heads]
    u = [_dot(t_inv[hd], v[hd] * beta[hd]) for hd in heads]
    w = [_dot(t_inv[hd], k[hd] * (beta[hd] * e_gc[hd])) for hd in heads]
    s = [s_ref[hd] for hd in heads]
    ws = [_dot(w[hd], s[hd]) for hd in heads]
    qs = [_dot(q[hd] * e_gc[hd], s[hd]) for hd in heads]
    v_new = [u[hd] - ws[hd] for hd in heads]
    k_dec_t = [(k[hd] * jnp.exp(gc_last[hd] - gc[hd])).T for hd in heads]
    o_intra = [_dot(qk[hd] * decay[hd], v_new[hd]) for hd in heads]
    s_upd = [_dot(k_dec_t[hd], v_new[hd]) for hd in heads]
    for hd in heads:
        o_ref[:, hd * dv:(hd + 1) * dv] = qs[hd] + o_intra[hd]
        s_ref[hd] = s[hd] * jnp.exp(gc_last[hd]) + s_upd[hd]


def _gdn_scan(proj, conv_w, a_log_pad, dt_bias_pad):
    L = proj.shape[0]
    c = GDN_CHUNK
    nqkv = 2 * GDN_HEADS * GDN_DK + GDN_HEADS * GDN_DV
    nv = GDN_HEADS * GDN_DV
    ba_block = (nqkv + nv) // (2 * LANES)
    return pl.pallas_call(
        functools.partial(_gdn_scan_kernel, nheads=GDN_HEADS, dk=GDN_DK, dv=GDN_DV, c=c),
        grid=(L // c,),
        in_specs=[pl.BlockSpec((c, nqkv), lambda n: (n, 0)),
                  pl.BlockSpec((c, 2 * LANES), lambda n: (n, ba_block)),
                  _const_spec((GDN_CONV, nqkv)), _const_spec((1, LANES)), _const_spec((1, LANES))],
        out_specs=pl.BlockSpec((c, nv), lambda n: (n, 0)),
        out_shape=jax.ShapeDtypeStruct((L, nv), F32),
        scratch_shapes=[pltpu.VMEM((c + 8, nqkv), F32),
                        pltpu.VMEM((GDN_HEADS, GDN_DK, GDN_DV), F32)],
        compiler_params=_params("arbitrary"),
        name="gdn_scan",
    )(proj, proj, conv_w, a_log_pad, dt_bias_pad)


def _gdn_out_kernel(o_ref, gate_ref, og_ref, w_ref, h_ref, y_ref, *, nheads, dv):
    parts = []
    for hd in range(nheads):
        cs = slice(hd * dv, (hd + 1) * dv)
        on = _rmsnorm_rows(o_ref[:, cs], og_ref[...])
        parts.append((on * _silu(gate_ref[:, cs])).astype(MXU_DTYPE))
    y_ref[...] = h_ref[...] + jnp.dot(jnp.concatenate(parts, axis=1), w_ref[...],
                                      preferred_element_type=F32)


def _gdn_out(o, proj, o_norm_g, w_out, h, tm=512):
    L, D = h.shape
    nv = GDN_HEADS * GDN_DV
    gate_block = (2 * GDN_HEADS * GDN_DK + nv) // nv
    return pl.pallas_call(
        functools.partial(_gdn_out_kernel, nheads=GDN_HEADS, dv=GDN_DV),
        grid=(L // tm,),
        in_specs=[pl.BlockSpec((tm, nv), lambda i: (i, 0)),
                  pl.BlockSpec((tm, nv), lambda i: (i, gate_block)),
                  _const_spec((1, GDN_DV)), _const_spec((nv, D)),
                  pl.BlockSpec((tm, D), lambda i: (i, 0))],
        out_specs=pl.BlockSpec((tm, D), lambda i: (i, 0)),
        out_shape=jax.ShapeDtypeStruct((L, D), F32),
        compiler_params=_params("arbitrary"),
        name="gdn_out",
    )(o, proj, o_norm_g.reshape(1, GDN_DV), w_out, h)


def _gdn_layer(h, g, w_in, conv_w, a_log, dt_bias, o_norm_g, w_out):
    L, D = h.shape
    nmain = 2 * GDN_HEADS * GDN_DK + 2 * GDN_HEADS * GDN_DV
    zpad = jnp.zeros((D, LANES - GDN_HEADS), w_in.dtype)
    w_cat = jnp.concatenate([w_in[:, :nmain], w_in[:, nmain:nmain + GDN_HEADS], zpad,
                             w_in[:, nmain + GDN_HEADS:], zpad], axis=1).astype(MXU_DTYPE)
    proj = _norm_proj(h, g, w_cat, tn=w_cat.shape[1] // 2)
    vpad = lambda a: jnp.concatenate([a.astype(F32), jnp.zeros((LANES - GDN_HEADS,), F32)]).reshape(1, LANES)
    o = _gdn_scan(proj, conv_w, vpad(a_log), vpad(dt_bias))
    return _gdn_out(o, proj, o_norm_g, w_out.astype(MXU_DTYPE), h)


def kernel(x, norm_mix_g, norm_mlp_g, final_g, mlp_w1, mlp_w2, dsa_w_in, dsa_idx_k_g, dsa_idx_k_b, dsa_w_out, conv_w_pw1, conv_b_pw1, conv_w_dw, conv_b_dw, conv_ln_g, conv_ln_b, conv_w_pw2, conv_b_pw2, mla_w_in, mla_q_norm_g, mla_w_uq, mla_kv_norm_g, mla_w_ukv, mla_w_out, gdn_w_in, gdn_conv_w, gdn_a_log, gdn_dt_bias, gdn_o_norm_g, gdn_w_out):
    b, L, D = x.shape
    depth = norm_mix_g.shape[0]
    outs = []
    for bi in range(b):
        h = x[bi]
        for i in range(depth):
            m = i % 4
            jl = i // 4
            g = norm_mix_g[i]
            if m == 0:
                h = _dsa_layer(h, g, dsa_w_in[jl], dsa_idx_k_g[jl], dsa_idx_k_b[jl], dsa_w_out[jl])
            elif m == 1:
                h = _conv_layer(h, g, conv_w_pw1[jl], conv_b_pw1[jl], conv_w_dw[jl], conv_b_dw[jl],
                                conv_ln_g[jl], conv_ln_b[jl], conv_w_pw2[jl], conv_b_pw2[jl])
            elif m == 2:
                h = _mla_layer(h, g, mla_w_in[jl], mla_q_norm_g[jl], mla_w_uq[jl], mla_kv_norm_g[jl],
                               mla_w_ukv[jl], mla_w_out[jl])
            else:
                h = _gdn_layer(h, g, gdn_w_in[jl], gdn_conv_w[jl], gdn_a_log[jl], gdn_dt_bias[jl],
                               gdn_o_norm_g[jl], gdn_w_out[jl])
            h = _mlp(h, norm_mlp_g[i], mlp_w1[i].astype(MXU_DTYPE), mlp_w2[i].astype(MXU_DTYPE),
                     final_g, final_norm=(i == depth - 1))
        outs.append(h)
    return jnp.stack(outs)
```

```python
import functools
import math

import jax
import jax.numpy as jnp
import numpy as np
from jax import lax
from jax.experimental import pallas as pl
from jax.experimental.pallas import tpu as pltpu

F32 = jnp.float32
MXU_DTYPE = jnp.bfloat16
EPS = 1e-6
LANES = 128
VMEM_LIMIT_BYTES = 56 * 1024 * 1024
SOFTMAX_DTYPE = jnp.bfloat16
NEG_BIG = -(2.0 ** 100)
INT_MIN = -(2 ** 31)

Q_BLOCK = 128
IDX_HEADS = 8
IDX_DIM = 64
TOPK_MAX = 256
DSA_HEADS = 8
CONV_WIDTH = 31
MLA_HEADS = 16
MLA_Q_LORA = 384
MLA_KV_LORA = 256
MLA_NOPE = 64
MLA_ROPE = 32
MLA_V = 64
ROPE_THETA = 10000.0
GDN_HEADS = 8
GDN_DK = 128
GDN_DV = 128
GDN_CONV = 4
GDN_CHUNK = 64


def _params(*sem):
    return pltpu.CompilerParams(dimension_semantics=sem, vmem_limit_bytes=VMEM_LIMIT_BYTES)


def _dot(a, b):
    return jnp.dot(a.astype(MXU_DTYPE), b.astype(MXU_DTYPE), preferred_element_type=F32)


def _dot_nt(a, b):
    return lax.dot_general(a.astype(MXU_DTYPE), b.astype(MXU_DTYPE),
                           (((1,), (1,)), ((), ())), preferred_element_type=F32)


def _rmsnorm_rows(x, g):
    return x * lax.rsqrt(jnp.mean(x * x, axis=-1, keepdims=True) + EPS) * g


def _sigmoid(x):
    return 1.0 / (1.0 + jnp.exp(-x))


def _silu(x):
    return x * _sigmoid(x)


def _const_spec(shape):
    return pl.BlockSpec(shape, lambda *_: (0,) * len(shape))


LOG2E = math.log2(math.e)
LOGITS_AHEAD = 3


def _causal_steps(n_q, tq, tk):
    qi, kj = [], []
    for i in range(n_q):
        for j in range(((i + 1) * tq - 1) // tk + 1):
            qi.append(i)
            kj.append(j)
    return jnp.asarray(np.array(qi, np.int32)), jnp.asarray(np.array(kj, np.int32))


def _flash_heads(q_ref, k_ref, v_ref, m_ref, l_ref, acc_ref, bias, nheads, v_cols):
    tk = k_ref.shape[0]
    ones = jnp.ones((tk, LANES), MXU_DTYPE)

    def logits(h):
        cs = slice(h * LANES, (h + 1) * LANES)
        return _dot_nt(q_ref[:, cs], k_ref[:, cs])

    ahead = [logits(h) for h in range(min(LOGITS_AHEAD, nheads))]
    for h in range(nheads):
        cs = slice(h * LANES, (h + 1) * LANES)
        s = ahead.pop(0)
        if h + LOGITS_AHEAD < nheads:
            ahead.append(logits(h + LOGITS_AHEAD))
        s = s.astype(SOFTMAX_DTYPE) + bias
        m_prev = m_ref[h]
        m_next = jnp.maximum(m_prev, jnp.max(s, axis=1, keepdims=True).astype(F32))
        alpha = jnp.exp2(m_prev - m_next)
        p = jnp.exp2(s - m_next[:, 0:1].astype(SOFTMAX_DTYPE)).astype(MXU_DTYPE)
        v_ext = jnp.concatenate([v_ref[:, v_cols(h)].astype(MXU_DTYPE), ones], axis=1)
        pv = jnp.dot(p, v_ext, preferred_element_type=F32)
        l_ref[h] = alpha * l_ref[h] + pv[:, LANES:]
        m_ref[h] = m_next
        acc_ref[:, cs] = alpha * acc_ref[:, cs] + pv[:, :LANES]


def _flash_finish(o_ref, l_ref, acc_ref, nheads):
    for h in range(nheads):
        cs = slice(h * LANES, (h + 1) * LANES)
        o_ref[:, cs] = (acc_ref[:, cs] / l_ref[h]).astype(o_ref.dtype)


def _mlp_kernel(h_ref, g_ref, w1_ref, w2_ref, gf_ref, o_ref, xn_ref, acc_ref, *, final_norm):
    f = pl.program_id(1)

    @pl.when(f == 0)
    def _():
        xn_ref[...] = _rmsnorm_rows(h_ref[...], g_ref[...]).astype(xn_ref.dtype)
        acc_ref[...] = jnp.zeros_like(acc_ref)

    a = jnp.dot(xn_ref[...], w1_ref[...], preferred_element_type=F32)
    a = jnp.square(jnp.maximum(a, 0.0))
    acc_ref[...] += _dot(a, w2_ref[...])

    @pl.when(f == pl.num_programs(1) - 1)
    def _():
        y = h_ref[...] + acc_ref[...]
        if final_norm:
            y = _rmsnorm_rows(y, gf_ref[...])
        o_ref[...] = y


def _mlp(h, g, w1, w2, gf, final_norm, tm=512, tf=1024):
    L, D = h.shape
    dff = w1.shape[1]
    return pl.pallas_call(
        functools.partial(_mlp_kernel, final_norm=final_norm),
        grid=(L // tm, dff // tf),
        in_specs=[
            pl.BlockSpec((tm, D), lambda i, f: (i, 0)),
            _const_spec((1, D)),
            pl.BlockSpec((D, tf), lambda i, f: (0, f)),
            pl.BlockSpec((tf, D), lambda i, f: (f, 0)),
            _const_spec((1, D)),
        ],
        out_specs=pl.BlockSpec((tm, D), lambda i, f: (i, 0)),
        out_shape=jax.ShapeDtypeStruct((L, D), F32),
        scratch_shapes=[pltpu.VMEM((tm, D), MXU_DTYPE), pltpu.VMEM((tm, D), F32)],
        compiler_params=_params("arbitrary", "arbitrary"),
        name="mlp",
    )(h, g.reshape(1, D), w1, w2, gf.reshape(1, D))


def _out_proj_kernel(a_ref, w_ref, b_ref, h_ref, o_ref):
    o_ref[...] = h_ref[...] + _dot(a_ref[...], w_ref[...]) + b_ref[...]


def _out_proj(a, w, b, h, tm=512):
    L, K = a.shape
    D = w.shape[1]
    return pl.pallas_call(
        _out_proj_kernel,
        grid=(L // tm,),
        in_specs=[
            pl.BlockSpec((tm, K), lambda i: (i, 0)),
            _const_spec((K, D)),
            _const_spec((1, D)),
            pl.BlockSpec((tm, D), lambda i: (i, 0)),
        ],
        out_specs=pl.BlockSpec((tm, D), lambda i: (i, 0)),
        out_shape=jax.ShapeDtypeStruct((L, D), F32),
        compiler_params=_params("arbitrary"),
        name="out_proj",
    )(a, w, b.reshape(1, D), h)


def _norm_proj_kernel(h_ref, g_ref, w_ref, o_ref, xn_ref):
    @pl.when(pl.program_id(1) == 0)
    def _():
        xn_ref[...] = _rmsnorm_rows(h_ref[...], g_ref[...]).astype(xn_ref.dtype)

    o_ref[...] = jnp.dot(xn_ref[...], w_ref[...], preferred_element_type=F32).astype(o_ref.dtype)


def _norm_proj(h, g, w, tn, out_dtype=F32, tm=512):
    L, D = h.shape
    N = w.shape[1]
    return pl.pallas_call(
        _norm_proj_kernel,
        grid=(L // tm, N // tn),
        in_specs=[
            pl.BlockSpec((tm, D), lambda i, j: (i, 0)),
            _const_spec((1, D)),
            pl.BlockSpec((D, tn), lambda i, j: (0, j)),
        ],
        out_specs=pl.BlockSpec((tm, tn), lambda i, j: (i, j)),
        out_shape=jax.ShapeDtypeStruct((L, N), out_dtype),
        scratch_shapes=[pltpu.VMEM((tm, D), MXU_DTYPE)],
        compiler_params=_params("arbitrary", "arbitrary"),
        name="norm_proj",
    )(h, g.reshape(1, D), w)


def _dsa_proj_kernel(h_ref, g_ref, w_ref, lg_ref, lb_ref,
                     q_ref, k_ref, v_ref, qi_ref, kia_ref, kib_ref, wi_ref, *, d, scale, wi_scale):
    xn = _rmsnorm_rows(h_ref[...], g_ref[...]).astype(MXU_DTYPE)

    def mm(c0, n):
        return jnp.dot(xn, w_ref[:, c0:c0 + n], preferred_element_type=F32)

    cw = 512
    for c in range(0, d, cw):
        q_ref[:, c:c + cw] = (mm(c, cw) * scale).astype(q_ref.dtype)
        k_ref[:, c:c + cw] = mm(d + c, cw).astype(k_ref.dtype)
        v_ref[:, c:c + cw] = mm(2 * d + c, cw).astype(v_ref.dtype)
    nqi = IDX_HEADS * IDX_DIM
    qi_ref[...] = mm(3 * d, nqi).astype(qi_ref.dtype)
    xa = mm(3 * d + nqi, LANES)
    xb = mm(3 * d + nqi + LANES, LANES)
    lane = lax.broadcasted_iota(jnp.int32, xa.shape, 1)

    def masked_ln(x, m, gain, bias):
        mu = jnp.sum(jnp.where(m, x, 0.0), axis=-1, keepdims=True) * (1.0 / IDX_DIM)
        dlt = jnp.where(m, x - mu, 0.0)
        var = jnp.sum(dlt * dlt, axis=-1, keepdims=True) * (1.0 / IDX_DIM)
        return dlt * lax.rsqrt(var + EPS) * gain + bias

    kia_ref[...] = masked_ln(xa, lane < IDX_DIM, lg_ref[0:1, :], lb_ref[0:1, :]).astype(kia_ref.dtype)
    kib_ref[...] = masked_ln(xb, lane >= IDX_DIM, lg_ref[1:2, :], lb_ref[1:2, :]).astype(kib_ref.dtype)
    wi_ref[...] = jnp.where(lane < IDX_HEADS, xb, 0.0) * wi_scale


def _dsa_proj(h, g, w_cat, lg2, lb2, tm=256):
    L, D = h.shape
    N = w_cat.shape[1]
    nqi = IDX_HEADS * IDX_DIM
    row = lambda n: pl.BlockSpec((tm, n), lambda i: (i, 0))
    return pl.pallas_call(
        functools.partial(_dsa_proj_kernel, d=D, scale=(D // DSA_HEADS) ** -0.5 * LOG2E,
                          wi_scale=IDX_HEADS ** -0.5 * IDX_DIM ** -0.5),
        grid=(L // tm,),
        in_specs=[row(D), _const_spec((1, D)), _const_spec((D, N)),
                  _const_spec((2, LANES)), _const_spec((2, LANES))],
        out_specs=[row(D), row(D), row(D), row(nqi), row(LANES), row(LANES), row(LANES)],
        out_shape=[jax.ShapeDtypeStruct((L, D), MXU_DTYPE)] * 3
        + [jax.ShapeDtypeStruct((L, nqi), MXU_DTYPE)]
        + [jax.ShapeDtypeStruct((L, LANES), MXU_DTYPE)] * 2
        + [jax.ShapeDtypeStruct((L, LANES), F32)],
        compiler_params=_params("arbitrary"),
        name="dsa_proj",
    )(h, g.reshape(1, D), w_cat, lg2, lb2)


I16 = jnp.int16
I16_MIN = -(2 ** 15)


def _dsa_select_kernel(qi_ref, wi_ref, kia_ref, kib_ref, bias_ref, hi_ref, lo_ref, cand_ref,
                       *, tq, tk, topk, rb):
    i = pl.program_id(0)
    n_tiles = hi_ref.shape[0]
    n_kt = ((i + 1) * tq + tk - 1) // tk
    q0 = i * tq
    qpos = q0 + lax.broadcasted_iota(jnp.int32, (tq, tk), 0)
    lane_pos = lax.broadcasted_iota(jnp.int32, (tq, tk), 1)

    def score_tile(c, carry):
        k0 = pl.multiple_of(c * tk, tk)
        ka = kia_ref[pl.ds(k0, tk), :]
        kb = kib_ref[pl.ds(k0, tk), :]
        acc = jnp.zeros((tq, tk), F32)
        for p in range(IDX_HEADS // 2):
            lhs = qi_ref[:, p * LANES:(p + 1) * LANES]
            sa = jnp.maximum(_dot_nt(lhs, ka), 0.0)
            sb = jnp.maximum(_dot_nt(lhs, kb), 0.0)
            acc = acc + wi_ref[:, 2 * p:2 * p + 1] * sa
            acc = acc + wi_ref[:, 2 * p + 1:2 * p + 2] * sb
        bits = pltpu.bitcast(acc, jnp.int32)
        key = bits ^ ((bits >> 31) & jnp.int32(0x7FFFFFFF))
        key = jnp.where(k0 + lane_pos <= qpos, key, jnp.int32(INT_MIN))
        hi_ref[c] = (key >> 16).astype(I16)
        lo_ref[c] = ((key & 0xFFFF) + I16_MIN).astype(I16)
        return carry

    lax.fori_loop(0, n_kt, score_tile, 0)

    def count(pred):
        def body(c, acc):
            parts = []
            for r0 in range(0, tq, rb):
                cand = cand_ref[r0:r0 + rb, :]
                part = jnp.zeros((rb, LANES), I16)
                for t in range(tk // LANES):
                    ls = slice(t * LANES, (t + 1) * LANES)
                    hit = pred(hi_ref[c, r0:r0 + rb, ls], lo_ref[c, r0:r0 + rb, ls], cand,
                               c * tk + t * LANES, r0)
                    part = part + hit.astype(I16)
                parts.append(part)
            return acc + jnp.concatenate(parts, axis=0)
        acc = lax.fori_loop(0, n_kt, body, jnp.zeros((tq, LANES), I16))
        return jnp.sum(acc.astype(F32), axis=1, keepdims=True)

    def set_cand(x):
        cand_ref[...] = jnp.broadcast_to(x, (tq, LANES)).astype(I16)

    def bisect16(pred, n_ge0):
        def bit_step(b, carry):
            c_best, n_best = carry
            cand = c_best + lax.shift_left(jnp.int32(1), 15 - b)
            set_cand(cand)
            cnt = count(pred)
            keep = cnt >= topk
            return jnp.where(keep, cand, c_best), jnp.where(keep, cnt, n_best)
        return lax.fori_loop(0, 16, bit_step, (jnp.full((tq, 1), I16_MIN, jnp.int32), n_ge0))

    n_all = jnp.broadcast_to((n_kt * tk).astype(F32), (tq, 1))
    thr_hi, n_hi = bisect16(lambda hi, lo, cd, k0, r0: hi >= cd, n_all)
    thr_hi16 = jnp.broadcast_to(thr_hi, (tq, LANES)).astype(I16)

    def clamp_low(c, carry):
        for r0 in range(0, tq, rb):
            hi = hi_ref[c, r0:r0 + rb, :]
            th = jnp.concatenate([thr_hi16[r0:r0 + rb]] * (tk // LANES), axis=1)
            side = jnp.where(hi > th, jnp.asarray(2 ** 15 - 1, I16), jnp.asarray(I16_MIN, I16))
            lo_ref[c, r0:r0 + rb, :] = jnp.where(hi == th, lo_ref[c, r0:r0 + rb, :], side)
        return carry

    lax.fori_loop(0, n_kt, clamp_low, 0)
    thr_lo, n_ge = bisect16(lambda hi, lo, cd, k0, r0: lo >= cd, n_hi)
    thr_lo16 = jnp.broadcast_to(thr_lo, (tq, LANES)).astype(I16)
    is_floor = jnp.logical_and(thr_hi == I16_MIN, thr_lo == I16_MIN)

    def gt_thr(hi, lo, r0):
        th, tl = thr_hi16[r0:r0 + rb], thr_lo16[r0:r0 + rb]
        return jnp.logical_or(hi > th, jnp.logical_and(hi == th, lo > tl))

    def eq_thr(hi, lo, r0):
        return jnp.logical_and(hi == thr_hi16[r0:r0 + rb], lo == thr_lo16[r0:r0 + rb])

    cand_ref[...] = jnp.broadcast_to(jnp.where(is_floor, -1, 2 ** 15 - 1), (tq, LANES)).astype(I16)
    excess = jnp.max(jnp.where(is_floor, 0.0, n_ge - topk))

    @pl.when(excess > 0.0)
    def _ties():
        lane_idx = lax.broadcasted_iota(jnp.int32, (rb, LANES), 1)
        need = topk - count(lambda hi, lo, cd, k0, r0: gt_thr(hi, lo, r0))

        def idx_step(b, cut):
            cand = cut + lax.shift_left(jnp.int32(1), 14 - b)
            set_cand(cand)
            cnt = count(lambda hi, lo, cd, k0, r0: jnp.logical_and(
                eq_thr(hi, lo, r0), (k0 + lane_idx).astype(I16) < cd))
            return jnp.where(cnt < need, cand, cut)

        cut = lax.fori_loop(0, 15, idx_step, jnp.zeros((tq, 1), jnp.int32))
        set_cand(jnp.where(is_floor, -1, cut))

    lane_idx = lax.broadcasted_iota(jnp.int32, (rb, tk), 1)

    def write_tile(c, carry):
        for r0 in range(0, tq, rb):
            hi = hi_ref[c, r0:r0 + rb, :]
            lo = lo_ref[c, r0:r0 + rb, :]
            th = jnp.concatenate([thr_hi16[r0:r0 + rb]] * (tk // LANES), axis=1)
            tl = jnp.concatenate([thr_lo16[r0:r0 + rb]] * (tk // LANES), axis=1)
            cut = jnp.concatenate([cand_ref[r0:r0 + rb, :]] * (tk // LANES), axis=1)
            gt = jnp.logical_or(hi > th, jnp.logical_and(hi == th, lo > tl))
            eq = jnp.logical_and(jnp.logical_and(hi == th, lo == tl),
                                 (c * tk + lane_idx).astype(I16) <= cut)
            keep = jnp.logical_or(gt, eq)
            bias_ref[c, r0:r0 + rb, :] = jnp.where(
                keep, jnp.zeros((), SOFTMAX_DTYPE), jnp.asarray(NEG_BIG, SOFTMAX_DTYPE))
        return carry

    lax.fori_loop(0, n_kt, write_tile, 0)

    def fill_tile(c, carry):
        bias_ref[c] = jnp.full((tq, tk), NEG_BIG, SOFTMAX_DTYPE)
        return carry

    lax.fori_loop(n_kt, n_tiles, fill_tile, 0)


def _dsa_select(qi, kia, kib, wi, topk, tq=256, tk=512, rb=64):
    L, nqi = qi.shape
    assert L // LANES < 2 ** 15 and L < 2 ** 15
    n_tiles = L // tk
    return pl.pallas_call(
        functools.partial(_dsa_select_kernel, tq=tq, tk=tk, topk=topk, rb=rb),
        grid=(L // tq,),
        in_specs=[
            pl.BlockSpec((tq, nqi), lambda i: (i, 0)),
            pl.BlockSpec((tq, LANES), lambda i: (i, 0)),
            _const_spec((L, LANES)),
            _const_spec((L, LANES)),
        ],
        out_specs=pl.BlockSpec((n_tiles, tq, tk), lambda i: (0, i, 0)),
        out_shape=jax.ShapeDtypeStruct((n_tiles, L, tk), SOFTMAX_DTYPE),
        scratch_shapes=[
            pltpu.VMEM((n_tiles, tq, tk), I16),
            pltpu.VMEM((n_tiles, tq, tk), I16),
            pltpu.VMEM((tq, LANES), I16),
        ],
        compiler_params=_params("arbitrary"),
        name="dsa_select",
    )(qi, wi, kia, kib)


def _dsa_attn_kernel(qi_s, kj_s, q_ref, k_ref, v_ref, bias_ref, o_ref, m_ref, l_ref, acc_ref,
                     *, tq, tk, nheads):
    step = pl.program_id(0)
    i = qi_s[step]
    j = kj_s[step]
    n_kt = ((i + 1) * tq + tk - 1) // tk

    @pl.when(j == 0)
    def _init():
        m_ref[...] = jnp.full_like(m_ref, NEG_BIG)
        l_ref[...] = jnp.zeros_like(l_ref)
        acc_ref[...] = jnp.zeros_like(acc_ref)

    _flash_heads(q_ref, k_ref, v_ref, m_ref, l_ref, acc_ref, bias_ref[0], nheads,
                 lambda h: slice(h * LANES, (h + 1) * LANES))

    @pl.when(j == n_kt - 1)
    def _finish():
        _flash_finish(o_ref, l_ref, acc_ref, nheads)


def _dsa_attn(q, k, v, bias, tq=512, tk=512):
    L, D = q.shape
    nheads = DSA_HEADS
    assert D == nheads * LANES and bias.shape == (L // tk, L, tk)
    qi_s, kj_s = _causal_steps(L // tq, tq, tk)
    qmap = lambda s, qs, ks: (qs[s], 0)
    kmap = lambda s, qs, ks: (ks[s], 0)
    grid_spec = pltpu.PrefetchScalarGridSpec(
        num_scalar_prefetch=2,
        grid=(qi_s.shape[0],),
        in_specs=[
            pl.BlockSpec((tq, D), qmap),
            pl.BlockSpec((tk, D), kmap),
            pl.BlockSpec((tk, D), kmap),
            pl.BlockSpec((1, tq, tk), lambda s, qs, ks: (ks[s], qs[s], 0)),
        ],
        out_specs=pl.BlockSpec((tq, D), qmap),
        scratch_shapes=[
            pltpu.VMEM((nheads, tq, LANES), F32),
            pltpu.VMEM((nheads, tq, LANES), F32),
            pltpu.VMEM((tq, D), F32),
        ],
    )
    return pl.pallas_call(
        functools.partial(_dsa_attn_kernel, tq=tq, tk=tk, nheads=nheads),
        grid_spec=grid_spec,
        out_shape=jax.ShapeDtypeStruct((L, D), MXU_DTYPE),
        compiler_params=_params("arbitrary"),
        name="dsa_attn",
    )(qi_s, kj_s, q, k, v, bias)


def _dsa_layer(h, g, w_in, idx_k_g, idx_k_b, w_out):
    L, D = h.shape
    nqi = IDX_HEADS * IDX_DIM
    o3 = 3 * D
    o4 = o3 + nqi
    o5 = o4 + IDX_DIM
    w_ki = w_in[:, o4:o5]
    w_wi = w_in[:, o5:]
    zpad = lambda n: jnp.zeros((D, n), w_in.dtype)
    w_cat = jnp.concatenate(
        [w_in[:, :o4],
         w_ki, w_wi, zpad(LANES - IDX_DIM - IDX_HEADS),
         w_wi, zpad(LANES - IDX_DIM - IDX_HEADS), w_ki], axis=1).astype(MXU_DTYPE)
    z = jnp.zeros((IDX_DIM,), F32)
    lg2 = jnp.stack([jnp.concatenate([idx_k_g, z]), jnp.concatenate([z, idx_k_g])])
    lb2 = jnp.stack([jnp.concatenate([idx_k_b, z]), jnp.concatenate([z, idx_k_b])])
    q, k, v, qi, kia, kib, wi = _dsa_proj(h, g, w_cat, lg2, lb2)
    bias = _dsa_select(qi, kia, kib, wi, topk=min(TOPK_MAX, L // 4))
    o = _dsa_attn(q, k, v, bias)
    return _out_proj(o, w_out.astype(MXU_DTYPE), jnp.zeros((D,), F32), h)


def _glu_proj_kernel(h_ref, g_ref, w_ref, b_ref, u_ref, *, d):
    xn = _rmsnorm_rows(h_ref[...], g_ref[...]).astype(MXU_DTYPE)
    cw = 512
    for c in range(0, d, cw):
        a = jnp.dot(xn, w_ref[:, c:c + cw], preferred_element_type=F32) + b_ref[:, c:c + cw]
        gt = jnp.dot(xn, w_ref[:, d + c:d + c + cw], preferred_element_type=F32) + b_ref[:, d + c:d + c + cw]
        u_ref[:, c:c + cw] = a * _sigmoid(gt)


def _glu_proj(h, g, w, b, tm=512):
    L, D = h.shape
    return pl.pallas_call(
        functools.partial(_glu_proj_kernel, d=D),
        grid=(L // tm,),
        in_specs=[pl.BlockSpec((tm, D), lambda i: (i, 0)), _const_spec((1, D)),
                  _const_spec((D, 2 * D)), _const_spec((1, 2 * D))],
        out_specs=pl.BlockSpec((tm, D), lambda i: (i, 0)),
        out_shape=jax.ShapeDtypeStruct((L, D), F32),
        compiler_params=_params("arbitrary"),
        name="conv_glu_proj",
    )(h, g.reshape(1, D), w, b.reshape(1, 2 * D))


CONV_HALO = 32


def _conv_out_kernel(u_ref, up_ref, wdw_ref, bdw_ref, lg_ref, lb_ref, w2_ref, b2_ref, h_ref, o_ref,
                     x_ref, *, tm):
    i = pl.program_id(0)
    x_ref[CONV_HALO:, :] = u_ref[...]
    x_ref[:CONV_HALO, :] = jnp.where(i > 0, up_ref[...], 0.0)
    off = CONV_HALO - (CONV_WIDTH - 1)
    acc = jnp.zeros(u_ref.shape, F32)
    for t in range(CONV_WIDTH):
        acc = acc + wdw_ref[t:t + 1, :] * x_ref[off + t:off + t + tm, :]
    y = acc + bdw_ref[...]
    mu = jnp.mean(y, axis=-1, keepdims=True)
    dlt = y - mu
    var = jnp.mean(dlt * dlt, axis=-1, keepdims=True)
    y = _silu(dlt * lax.rsqrt(var + EPS) * lg_ref[...] + lb_ref[...])
    o_ref[...] = h_ref[...] + _dot(y, w2_ref[...]) + b2_ref[...]


def _conv_out(u, w_dw, b_dw, ln_g, ln_b, w2, b2, h, tm=256):
    L, D = u.shape
    r = tm // CONV_HALO
    vec = lambda a: a.reshape(1, D)
    return pl.pallas_call(
        functools.partial(_conv_out_kernel, tm=tm),
        grid=(L // tm,),
        in_specs=[
            pl.BlockSpec((tm, D), lambda i: (i, 0)),
            pl.BlockSpec((CONV_HALO, D), lambda i: (jnp.maximum(i * r - 1, 0), 0)),
            _const_spec((CONV_WIDTH, D)),
            _const_spec((1, D)), _const_spec((1, D)), _const_spec((1, D)),
            _const_spec((D, D)), _const_spec((1, D)),
            pl.BlockSpec((tm, D), lambda i: (i, 0)),
        ],
        out_specs=pl.BlockSpec((tm, D), lambda i: (i, 0)),
        out_shape=jax.ShapeDtypeStruct((L, D), F32),
        scratch_shapes=[pltpu.VMEM((tm + CONV_HALO, D), F32)],
        compiler_params=_params("arbitrary"),
        name="conv_out",
    )(u, u, w_dw, vec(b_dw), vec(ln_g), vec(ln_b), w2, vec(b2), h)


def _conv_layer(h, g, w_pw1, b_pw1, w_dw, b_dw, ln_g, ln_b, w_pw2, b_pw2):
    u = _glu_proj(h, g, w_pw1.astype(MXU_DTYPE), b_pw1)
    return _conv_out(u, w_dw, b_dw, ln_g, ln_b, w_pw2.astype(MXU_DTYPE), b_pw2, h)


def _mla_prep_kernel(h_ref, g_ref, w1_ref, qg_ref, kvg_ref, wqa_ref, wqb_ref, wka_ref, wv_ref,
                     c_ref, s_ref, q_ref, k_ref, v_ref, *, scale):
    xn = _rmsnorm_rows(h_ref[...], g_ref[...]).astype(MXU_DTYPE)
    proj = jnp.dot(xn, w1_ref[...], preferred_element_type=F32)
    o1 = MLA_Q_LORA
    o2 = o1 + MLA_KV_LORA
    cq = _rmsnorm_rows(proj[:, :o1], qg_ref[...]).astype(MXU_DTYPE)
    ckv = _rmsnorm_rows(proj[:, o1:o2], kvg_ref[...]).astype(MXU_DTYPE)
    cos = c_ref[...]
    sin = s_ref[...]
    kr = proj[:, o2:o2 + LANES] * cos + proj[:, o2 + LANES:o2 + 2 * LANES] * sin
    for hd in range(MLA_HEADS):
        cs = slice(hd * LANES, (hd + 1) * LANES)
        qa = jnp.dot(cq, wqa_ref[:, cs], preferred_element_type=F32)
        qb = jnp.dot(cq, wqb_ref[:, cs], preferred_element_type=F32)
        q_ref[:, cs] = ((qa * cos + qb * sin) * scale).astype(q_ref.dtype)
        ka = jnp.dot(ckv, wka_ref[:, cs], preferred_element_type=F32)
        k_ref[:, cs] = (ka + kr).astype(k_ref.dtype)
    v_ref[...] = jnp.dot(ckv, wv_ref[...], preferred_element_type=F32).astype(v_ref.dtype)


def _mla_prep(h, g, w1, qg, kvg, wqa, wqb, wka, wv, ctab, stab, tm=256):
    L, D = h.shape
    HP = MLA_HEADS * LANES
    NV = MLA_HEADS * MLA_V
    row = lambda n: pl.BlockSpec((tm, n), lambda i: (i, 0))
    full = lambda a: _const_spec(a.shape)
    return pl.pallas_call(
        functools.partial(_mla_prep_kernel, scale=(MLA_NOPE + MLA_ROPE) ** -0.5 * LOG2E),
        grid=(L // tm,),
        in_specs=[row(D), _const_spec((1, D)), full(w1), _const_spec((1, MLA_Q_LORA)),
                  _const_spec((1, MLA_KV_LORA)), full(wqa), full(wqb), full(wka), full(wv),
                  row(LANES), row(LANES)],
        out_specs=[row(HP), row(HP), row(NV)],
        out_shape=[jax.ShapeDtypeStruct((L, HP), MXU_DTYPE), jax.ShapeDtypeStruct((L, HP), MXU_DTYPE),
                   jax.ShapeDtypeStruct((L, NV), MXU_DTYPE)],
        compiler_params=_params("arbitrary"),
        name="mla_prep",
    )(h, g.reshape(1, D), w1, qg.reshape(1, -1), kvg.reshape(1, -1), wqa, wqb, wka, wv, ctab, stab)


def _mla_attn_kernel(qi_s, kj_s, q_ref, k_ref, v_ref, o_ref, m_ref, l_ref, acc_ref, *, tq, tk, nheads):
    step = pl.program_id(0)
    i = qi_s[step]
    j = kj_s[step]
    n_kt = ((i + 1) * tq + tk - 1) // tk

    @pl.when(j == 0)
    def _init():
        m_ref[...] = jnp.full_like(m_ref, NEG_BIG)
        l_ref[...] = jnp.zeros_like(l_ref)
        acc_ref[...] = jnp.zeros_like(acc_ref)

    qpos = i * tq + lax.broadcasted_iota(jnp.int32, (tq, tk), 0)
    kpos = j * tk + lax.broadcasted_iota(jnp.int32, (tq, tk), 1)
    bias = jnp.where(kpos <= qpos, 0.0, NEG_BIG).astype(SOFTMAX_DTYPE)
    _flash_heads(q_ref, k_ref, v_ref, m_ref, l_ref, acc_ref, bias, nheads,
                 lambda h: slice((h // 2) * LANES, (h // 2 + 1) * LANES))

    @pl.when(j == n_kt - 1)
    def _finish():
        _flash_finish(o_ref, l_ref, acc_ref, nheads)


def _mla_attn(q, k, v, tq=512, tk=512):
    L, HP = q.shape
    NV = v.shape[1]
    qi_s, kj_s = _causal_steps(L // tq, tq, tk)
    qmap = lambda s, qs, ks: (qs[s], 0)
    kmap = lambda s, qs, ks: (ks[s], 0)
    grid_spec = pltpu.PrefetchScalarGridSpec(
        num_scalar_prefetch=2,
        grid=(qi_s.shape[0],),
        in_specs=[pl.BlockSpec((tq, HP), qmap),
                  pl.BlockSpec((tk, HP), kmap), pl.BlockSpec((tk, NV), kmap)],
        out_specs=pl.BlockSpec((tq, HP), qmap),
        scratch_shapes=[pltpu.VMEM((MLA_HEADS, tq, LANES), F32),
                        pltpu.VMEM((MLA_HEADS, tq, LANES), F32),
                        pltpu.VMEM((tq, HP), F32)],
    )
    return pl.pallas_call(
        functools.partial(_mla_attn_kernel, tq=tq, tk=tk, nheads=MLA_HEADS),
        grid_spec=grid_spec,
        out_shape=jax.ShapeDtypeStruct((L, HP), MXU_DTYPE),
        compiler_params=_params("arbitrary"),
        name="mla_attn",
    )(qi_s, kj_s, q, k, v)


def _place_heads(w, nheads, src_w, lane0, swap_half=0):
    K = w.shape[0]
    w3 = w.reshape(K, nheads, src_w)
    if swap_half:
        w3 = jnp.concatenate([w3[..., swap_half:], w3[..., :swap_half]], axis=-1)
    out = jnp.zeros((K, nheads, LANES), w.dtype)
    out = out.at[:, :, lane0:lane0 + src_w].set(w3)
    return out.reshape(K, nheads * LANES)


def _mla_layer(h, g, w_in, q_norm_g, w_uq, kv_norm_g, w_ukv, w_out):
    L, D = h.shape
    half = MLA_ROPE // 2
    o2 = MLA_Q_LORA + MLA_KV_LORA
    w_kr = w_in[:, o2:]
    w1 = jnp.concatenate(
        [w_in[:, :o2], _place_heads(w_kr, 1, MLA_ROPE, MLA_NOPE),
         _place_heads(w_kr, 1, MLA_ROPE, MLA_NOPE, swap_half=half)], axis=1).astype(MXU_DTYPE)
    uq = w_uq.reshape(MLA_Q_LORA, MLA_HEADS, MLA_NOPE + MLA_ROPE)
    uq_rope = uq[..., MLA_NOPE:].reshape(MLA_Q_LORA, MLA_HEADS * MLA_ROPE)
    wqa = _place_heads(w_uq, MLA_HEADS, MLA_NOPE + MLA_ROPE, 0).astype(MXU_DTYPE)
    wqb = _place_heads(uq_rope, MLA_HEADS, MLA_ROPE, MLA_NOPE, swap_half=half).astype(MXU_DTYPE)
    ukv = w_ukv.reshape(MLA_KV_LORA, MLA_HEADS, MLA_NOPE + MLA_V)
    wka = _place_heads(ukv[..., :MLA_NOPE].reshape(MLA_KV_LORA, -1), MLA_HEADS, MLA_NOPE, 0).astype(MXU_DTYPE)
    wv = ukv[..., MLA_NOPE:].reshape(MLA_KV_LORA, MLA_HEADS * MLA_V).astype(MXU_DTYPE)
    pos = jnp.arange(L, dtype=F32)
    inv_freq = ROPE_THETA ** (-jnp.arange(0, MLA_ROPE, 2, dtype=F32) / MLA_ROPE)
    ang = pos[:, None] * inv_freq[None, :]
    cos, sin = jnp.cos(ang), jnp.sin(ang)
    zeros = jnp.zeros((L, LANES - MLA_NOPE - MLA_ROPE), F32)
    ctab = jnp.concatenate([jnp.ones((L, MLA_NOPE), F32), cos, cos, zeros], axis=1)
    stab = jnp.concatenate([jnp.zeros((L, MLA_NOPE), F32), -sin, sin, zeros], axis=1)
    q, k, v = _mla_prep(h, g, w1, q_norm_g, kv_norm_g, wqa, wqb, wka, wv, ctab, stab)
    o = _mla_attn(q, k, v)
    wo = w_out.reshape(MLA_HEADS // 2, 2, MLA_V, D)
    zo = jnp.zeros_like(wo[:, 0])
    wo_pad = jnp.stack([jnp.concatenate([wo[:, 0], zo], axis=1),
                        jnp.concatenate([zo, wo[:, 1]], axis=1)], axis=1)
    wo_pad = wo_pad.reshape(MLA_HEADS * LANES, D).astype(MXU_DTYPE)
    return _out_proj(o, wo_pad, jnp.zeros((D,), F32), h)


def _gdn_scan_kernel(qkv_ref, ba_ref, cw_ref, alog_ref, dtb_ref, o_ref, x_ref, s_ref,
                     *, nheads, dk, dv, c):
    n = pl.program_id(0)
    halo = 8
    nqk = nheads * dk

    @pl.when(n == 0)
    def _init():
        x_ref[:halo, :] = jnp.zeros((halo, x_ref.shape[1]), F32)
        s_ref[...] = jnp.zeros_like(s_ref)

    x_ref[halo:, :] = qkv_ref[...]
    off = halo - (GDN_CONV - 1)
    acc = jnp.zeros(qkv_ref.shape, F32)
    for t in range(GDN_CONV):
        acc = acc + cw_ref[t:t + 1, :] * x_ref[off + t:off + t + c, :]
    x_ref[:halo, :] = qkv_ref[c - halo:, :]
    qkv = _silu(acc)

    beta_all = _sigmoid(ba_ref[:, :LANES])
    a_raw = ba_ref[:, LANES:] + dtb_ref[...]
    softplus = jnp.maximum(a_raw, 0.0) + jnp.log(1.0 + jnp.exp(-jnp.abs(a_raw)))
    g_all = -jnp.exp(alog_ref[...]) * softplus
    row = lax.broadcasted_iota(jnp.int32, (c, c), 0)
    col = lax.broadcasted_iota(jnp.int32, (c, c), 1)
    tril = row >= col
    tril_strict = row > col
    gc_all = jnp.dot(tril.astype(F32), g_all, preferred_element_type=F32,
                     precision=lax.Precision.HIGHEST)
    gc_t = gc_all.T

    heads = range(nheads)
    q = [qkv[:, hd * dk:(hd + 1) * dk] for hd in heads]
    k = [qkv[:, nqk + hd * dk:nqk + (hd + 1) * dk] for hd in heads]
    v = [qkv[:, 2 * nqk + hd * dv:2 * nqk + (hd + 1) * dv] for hd in heads]
    q = [x * lax.rsqrt(jnp.sum(x * x, axis=-1, keepdims=True) + EPS) * (dk ** -0.5) for x in q]
    k = [x * lax.rsqrt(jnp.sum(x * x, axis=-1, keepdims=True) + EPS) for x in k]
    beta = [beta_all[:, hd:hd + 1] for hd in heads]
    gc = [gc_all[:, hd:hd + 1] for hd in heads]
    gc_last = [gc_all[c - 1:c, hd:hd + 1] for hd in heads]
    decay = [jnp.where(tril, jnp.exp(jnp.where(tril, gc[hd] - gc_t[hd:hd + 1, :], 0.0)), 0.0)
             for hd in heads]
    kk = [_dot_nt(k[hd], k[hd]) for hd in heads]
    qk = [_dot_nt(q[hd], k[hd]) for hd in heads]
    a_mat = [jnp.where(tril_strict, beta[hd] * kk[hd] * decay[hd], 0.0) for hd in heads]
    eye = jnp.where(row == col, 1.0, 0.0)
    t_inv = [eye - jnp.where((row >> 1) == (col >> 1), a_mat[hd], 0.0) for hd in heads]
    for lvl in range(1, int(math.log2(c))):
        blk_mask = jnp.logical_and((row >> (lvl + 1)) == (col >> (lvl + 1)), (row >> lvl) != (col >> lvl))
        tm = [_dot(t_inv[hd], jnp.where(blk_mask, a_mat[hd], 0.0)) for hd in heads]
        tmt = [_dot(tm[hd], t_inv[hd]) for hd in heads]
        t_inv = [t_inv[hd] - tmt[hd] for hd in heads]
    e_gc = [jnp.exp(gc[hd]) for hd in heads]
    u = [_dot(t_inv[hd], v[hd] * beta[hd]) for hd in heads]
    w = [_dot(t_inv[hd], k[hd] * (beta[hd] * e_gc[hd])) for hd in heads]
    s = [s_ref[hd] for hd in heads]
    ws = [_dot(w[hd], s[hd]) for hd in heads]
    qs = [_dot(q[hd] * e_gc[hd], s[hd]) for hd in heads]
    v_new = [u[hd] - ws[hd] for hd in heads]
    k_dec_t = [(k[hd] * jnp.exp(gc_last[hd] - gc[hd])).T for hd in heads]
    o_intra = [_dot(qk[hd] * decay[hd], v_new[hd]) for hd in heads]
    s_upd = [_dot(k_dec_t[hd], v_new[hd]) for hd in heads]
    for hd in heads:
        o_ref[:, hd * dv:(hd + 1) * dv] = qs[hd] + o_intra[hd]
        s_ref[hd] = s[hd] * jnp.exp(gc_last[hd]) + s_upd[hd]


def _gdn_scan(proj, conv_w, a_log_pad, dt_bias_pad):
    L = proj.shape[0]
    c = GDN_CHUNK
    nqkv = 2 * GDN_HEADS * GDN_DK + GDN_HEADS * GDN_DV
    nv = GDN_HEADS * GDN_DV
    ba_block = (nqkv + nv) // (2 * LANES)
    return pl.pallas_call(
        functools.partial(_gdn_scan_kernel, nheads=GDN_HEADS, dk=GDN_DK, dv=GDN_DV, c=c),
        grid=(L // c,),
        in_specs=[pl.BlockSpec((c, nqkv), lambda n: (n, 0)),
                  pl.BlockSpec((c, 2 * LANES), lambda n: (n, ba_block)),
                  _const_spec((GDN_CONV, nqkv)), _const_spec((1, LANES)), _const_spec((1, LANES))],
        out_specs=pl.BlockSpec((c, nv), lambda n: (n, 0)),
        out_shape=jax.ShapeDtypeStruct((L, nv), F32),
        scratch_shapes=[pltpu.VMEM((c + 8, nqkv), F32),
                        pltpu.VMEM((GDN_HEADS, GDN_DK, GDN_DV), F32)],
        compiler_params=_params("arbitrary"),
        name="gdn_scan",
    )(proj, proj, conv_w, a_log_pad, dt_bias_pad)


def _gdn_out_kernel(o_ref, gate_ref, og_ref, w_ref, h_ref, y_ref, *, nheads, dv):
    parts = []
    for hd in range(nheads):
        cs = slice(hd * dv, (hd + 1) * dv)
        on = _rmsnorm_rows(o_ref[:, cs], og_ref[...])
        parts.append((on * _silu(gate_ref[:, cs])).astype(MXU_DTYPE))
    y_ref[...] = h_ref[...] + jnp.dot(jnp.concatenate(parts, axis=1), w_ref[...],
                                      preferred_element_type=F32)


def _gdn_out(o, proj, o_norm_g, w_out, h, tm=512):
    L, D = h.shape
    nv = GDN_HEADS * GDN_DV
    gate_block = (2 * GDN_HEADS * GDN_DK + nv) // nv
    return pl.pallas_call(
        functools.partial(_gdn_out_kernel, nheads=GDN_HEADS, dv=GDN_DV),
        grid=(L // tm,),
        in_specs=[pl.BlockSpec((tm, nv), lambda i: (i, 0)),
                  pl.BlockSpec((tm, nv), lambda i: (i, gate_block)),
                  _const_spec((1, GDN_DV)), _const_spec((nv, D)),
                  pl.BlockSpec((tm, D), lambda i: (i, 0))],
        out_specs=pl.BlockSpec((tm, D), lambda i: (i, 0)),
        out_shape=jax.ShapeDtypeStruct((L, D), F32),
        compiler_params=_params("arbitrary"),
        name="gdn_out",
    )(o, proj, o_norm_g.reshape(1, GDN_DV), w_out, h)


def _gdn_layer(h, g, w_in, conv_w, a_log, dt_bias, o_norm_g, w_out):
    L, D = h.shape
    nmain = 2 * GDN_HEADS * GDN_DK + 2 * GDN_HEADS * GDN_DV
    zpad = jnp.zeros((D, LANES - GDN_HEADS), w_in.dtype)
    w_cat = jnp.concatenate([w_in[:, :nmain], w_in[:, nmain:nmain + GDN_HEADS], zpad,
                             w_in[:, nmain + GDN_HEADS:], zpad], axis=1).astype(MXU_DTYPE)
    proj = _norm_proj(h, g, w_cat, tn=w_cat.shape[1] // 2)
    vpad = lambda a: jnp.concatenate([a.astype(F32), jnp.zeros((LANES - GDN_HEADS,), F32)]).reshape(1, LANES)
    o = _gdn_scan(proj, conv_w, vpad(a_log), vpad(dt_bias))
    return _gdn_out(o, proj, o_norm_g, w_out.astype(MXU_DTYPE), h)


def kernel(x, norm_mix_g, norm_mlp_g, final_g, mlp_w1, mlp_w2, dsa_w_in, dsa_idx_k_g, dsa_idx_k_b, dsa_w_out, conv_w_pw1, conv_b_pw1, conv_w_dw, conv_b_dw, conv_ln_g, conv_ln_b, conv_w_pw2, conv_b_pw2, mla_w_in, mla_q_norm_g, mla_w_uq, mla_kv_norm_g, mla_w_ukv, mla_w_out, gdn_w_in, gdn_conv_w, gdn_a_log, gdn_dt_bias, gdn_o_norm_g, gdn_w_out):
    b, L, D = x.shape
    depth = norm_mix_g.shape[0]
    outs = []
    for bi in range(b):
        h = x[bi]
        for i in range(depth):
            m = i % 4
            jl = i // 4
            g = norm_mix_g[i]
            if m == 0:
                h = _dsa_layer(h, g, dsa_w_in[jl], dsa_idx_k_g[jl], dsa_idx_k_b[jl], dsa_w_out[jl])
            elif m == 1:
                h = _conv_layer(h, g, conv_w_pw1[jl], conv_b_pw1[jl], conv_w_dw[jl], conv_b_dw[jl],
                                conv_ln_g[jl], conv_ln_b[jl], conv_w_pw2[jl], conv_b_pw2[jl])
            elif m == 2:
                h = _mla_layer(h, g, mla_w_in[jl], mla_q_norm_g[jl], mla_w_uq[jl], mla_kv_norm_g[jl],
                               mla_w_ukv[jl], mla_w_out[jl])
            else:
                h = _gdn_layer(h, g, gdn_w_in[jl], gdn_conv_w[jl], gdn_a_log[jl], gdn_dt_bias[jl],
                               gdn_o_norm_g[jl], gdn_w_out[jl])
            h = _mlp(h, norm_mlp_g[i], mlp_w1[i].astype(MXU_DTYPE), mlp_w2[i].astype(MXU_DTYPE),
                     final_g, final_norm=(i == depth - 1))
        outs.append(h)
    return jnp.stack(outs)
```

```python
import functools
import math

import jax
import jax.numpy as jnp
import numpy as np
from jax import lax
from jax.experimental import pallas as pl
from jax.experimental.pallas import tpu as pltpu

F32 = jnp.float32
MXU_DTYPE = jnp.bfloat16
EPS = 1e-6
LANES = 128
VMEM_LIMIT_BYTES = 56 * 1024 * 1024
SOFTMAX_DTYPE = jnp.bfloat16
NEG_BIG = -(2.0 ** 100)
INT_MIN = -(2 ** 31)

Q_BLOCK = 128
IDX_HEADS = 8
IDX_DIM = 64
TOPK_MAX = 256
DSA_HEADS = 8
CONV_WIDTH = 31
MLA_HEADS = 16
MLA_Q_LORA = 384
MLA_KV_LORA = 256
MLA_NOPE = 64
MLA_ROPE = 32
MLA_V = 64
ROPE_THETA = 10000.0
GDN_HEADS = 8
GDN_DK = 128
GDN_DV = 128
GDN_CONV = 4
GDN_CHUNK = 64


def _params(*sem):
    return pltpu.CompilerParams(dimension_semantics=sem, vmem_limit_bytes=VMEM_LIMIT_BYTES)


def _dot(a, b):
    return jnp.dot(a.astype(MXU_DTYPE), b.astype(MXU_DTYPE), preferred_element_type=F32)


def _dot_nt(a, b):
    return lax.dot_general(a.astype(MXU_DTYPE), b.astype(MXU_DTYPE),
                           (((1,), (1,)), ((), ())), preferred_element_type=F32)


def _rmsnorm_rows(x, g):
    return x * lax.rsqrt(jnp.mean(x * x, axis=-1, keepdims=True) + EPS) * g


def _sigmoid(x):
    return 1.0 / (1.0 + jnp.exp(-x))


def _silu(x):
    return x * _sigmoid(x)


def _const_spec(shape):
    return pl.BlockSpec(shape, lambda *_: (0,) * len(shape))


LOG2E = math.log2(math.e)
LOGITS_AHEAD = 3


def _causal_steps(n_q, tq, tk):
    qi, kj = [], []
    for i in range(n_q):
        for j in range(((i + 1) * tq - 1) // tk + 1):
            qi.append(i)
            kj.append(j)
    return jnp.asarray(np.array(qi, np.int32)), jnp.asarray(np.array(kj, np.int32))


def _flash_heads(q_ref, k_ref, v_ref, m_ref, l_ref, acc_ref, bias, nheads, v_cols):
    tk = k_ref.shape[0]
    ones = jnp.ones((tk, LANES), MXU_DTYPE)

    def logits(h):
        cs = slice(h * LANES, (h + 1) * LANES)
        return _dot_nt(q_ref[:, cs], k_ref[:, cs])

    ahead = [logits(h) for h in range(min(LOGITS_AHEAD, nheads))]
    for h in range(nheads):
        cs = slice(h * LANES, (h + 1) * LANES)
        s = ahead.pop(0)
        if h + LOGITS_AHEAD < nheads:
            ahead.append(logits(h + LOGITS_AHEAD))
        s = s.astype(SOFTMAX_DTYPE) + bias
        m_prev = m_ref[h]
        m_next = jnp.maximum(m_prev, jnp.max(s, axis=1, keepdims=True).astype(F32))
        alpha = jnp.exp2(m_prev - m_next)
        p = jnp.exp2(s - m_next[:, 0:1].astype(SOFTMAX_DTYPE)).astype(MXU_DTYPE)
        v_ext = jnp.concatenate([v_ref[:, v_cols(h)].astype(MXU_DTYPE), ones], axis=1)
        pv = jnp.dot(p, v_ext, preferred_element_type=F32)
        l_ref[h] = alpha * l_ref[h] + pv[:, LANES:]
        m_ref[h] = m_next
        acc_ref[:, cs] = alpha * acc_ref[:, cs] + pv[:, :LANES]


def _flash_finish(o_ref, l_ref, acc_ref, nheads):
    for h in range(nheads):
        cs = slice(h * LANES, (h + 1) * LANES)
        o_ref[:, cs] = (acc_ref[:, cs] / l_ref[h]).astype(o_ref.dtype)


def _mlp_kernel(h_ref, g_ref, w1_ref, w2_ref, gf_ref, o_ref, xn_ref, acc_ref, *, final_norm):
    f = pl.program_id(1)

    @pl.when(f == 0)
    def _():
        xn_ref[...] = _rmsnorm_rows(h_ref[...], g_ref[...]).astype(xn_ref.dtype)
        acc_ref[...] = jnp.zeros_like(acc_ref)

    a = jnp.dot(xn_ref[...], w1_ref[...], preferred_element_type=F32)
    a = jnp.square(jnp.maximum(a, 0.0))
    acc_ref[...] += _dot(a, w2_ref[...])

    @pl.when(f == pl.num_programs(1) - 1)
    def _():
        y = h_ref[...] + acc_ref[...]
        if final_norm:
            y = _rmsnorm_rows(y, gf_ref[...])
        o_ref[...] = y


def _mlp(h, g, w1, w2, gf, final_norm, tm=1024, tf=2048):
    L, D = h.shape
    dff = w1.shape[1]
    return pl.pallas_call(
        functools.partial(_mlp_kernel, final_norm=final_norm),
        grid=(L // tm, dff // tf),
        in_specs=[
            pl.BlockSpec((tm, D), lambda i, f: (i, 0)),
            _const_spec((1, D)),
            pl.BlockSpec((D, tf), lambda i, f: (0, f)),
            pl.BlockSpec((tf, D), lambda i, f: (f, 0)),
            _const_spec((1, D)),
        ],
        out_specs=pl.BlockSpec((tm, D), lambda i, f: (i, 0)),
        out_shape=jax.ShapeDtypeStruct((L, D), F32),
        scratch_shapes=[pltpu.VMEM((tm, D), MXU_DTYPE), pltpu.VMEM((tm, D), F32)],
        compiler_params=_params("arbitrary", "arbitrary"),
        name="mlp",
    )(h, g.reshape(1, D), w1, w2, gf.reshape(1, D))


def _out_proj_kernel(a_ref, w_ref, b_ref, h_ref, o_ref):
    o_ref[...] = h_ref[...] + _dot(a_ref[...], w_ref[...]) + b_ref[...]


def _out_proj(a, w, b, h, tm=512):
    L, K = a.shape
    D = w.shape[1]
    return pl.pallas_call(
        _out_proj_kernel,
        grid=(L // tm,),
        in_specs=[
            pl.BlockSpec((tm, K), lambda i: (i, 0)),
            _const_spec((K, D)),
            _const_spec((1, D)),
            pl.BlockSpec((tm, D), lambda i: (i, 0)),
        ],
        out_specs=pl.BlockSpec((tm, D), lambda i: (i, 0)),
        out_shape=jax.ShapeDtypeStruct((L, D), F32),
        compiler_params=_params("arbitrary"),
        name="out_proj",
    )(a, w, b.reshape(1, D), h)


def _norm_proj_kernel(h_ref, g_ref, w_ref, o_ref, xn_ref):
    @pl.when(pl.program_id(1) == 0)
    def _():
        xn_ref[...] = _rmsnorm_rows(h_ref[...], g_ref[...]).astype(xn_ref.dtype)

    o_ref[...] = jnp.dot(xn_ref[...], w_ref[...], preferred_element_type=F32).astype(o_ref.dtype)


def _norm_proj(h, g, w, tn, out_dtype=F32, tm=512):
    L, D = h.shape
    N = w.shape[1]
    return pl.pallas_call(
        _norm_proj_kernel,
        grid=(L // tm, N // tn),
        in_specs=[
            pl.BlockSpec((tm, D), lambda i, j: (i, 0)),
            _const_spec((1, D)),
            pl.BlockSpec((D, tn), lambda i, j: (0, j)),
        ],
        out_specs=pl.BlockSpec((tm, tn), lambda i, j: (i, j)),
        out_shape=jax.ShapeDtypeStruct((L, N), out_dtype),
        scratch_shapes=[pltpu.VMEM((tm, D), MXU_DTYPE)],
        compiler_params=_params("arbitrary", "arbitrary"),
        name="norm_proj",
    )(h, g.reshape(1, D), w)


def _dsa_proj_kernel(h_ref, g_ref, w_ref, lg_ref, lb_ref,
                     q_ref, k_ref, v_ref, qi_ref, kia_ref, kib_ref, wi_ref, *, d, scale, wi_scale):
    xn = _rmsnorm_rows(h_ref[...], g_ref[...]).astype(MXU_DTYPE)

    def mm(c0, n):
        return jnp.dot(xn, w_ref[:, c0:c0 + n], preferred_element_type=F32)

    cw = 512
    for c in range(0, d, cw):
        q_ref[:, c:c + cw] = (mm(c, cw) * scale).astype(q_ref.dtype)
        k_ref[:, c:c + cw] = mm(d + c, cw).astype(k_ref.dtype)
        v_ref[:, c:c + cw] = mm(2 * d + c, cw).astype(v_ref.dtype)
    nqi = IDX_HEADS * IDX_DIM
    qi_ref[...] = mm(3 * d, nqi).astype(qi_ref.dtype)
    xa = mm(3 * d + nqi, LANES)
    xb = mm(3 * d + nqi + LANES, LANES)
    lane = lax.broadcasted_iota(jnp.int32, xa.shape, 1)

    def masked_ln(x, m, gain, bias):
        mu = jnp.sum(jnp.where(m, x, 0.0), axis=-1, keepdims=True) * (1.0 / IDX_DIM)
        dlt = jnp.where(m, x - mu, 0.0)
        var = jnp.sum(dlt * dlt, axis=-1, keepdims=True) * (1.0 / IDX_DIM)
        return dlt * lax.rsqrt(var + EPS) * gain + bias

    kia_ref[...] = masked_ln(xa, lane < IDX_DIM, lg_ref[0:1, :], lb_ref[0:1, :]).astype(kia_ref.dtype)
    kib_ref[...] = masked_ln(xb, lane >= IDX_DIM, lg_ref[1:2, :], lb_ref[1:2, :]).astype(kib_ref.dtype)
    wi_ref[...] = jnp.where(lane < IDX_HEADS, xb, 0.0) * wi_scale


def _dsa_proj(h, g, w_cat, lg2, lb2, tm=256):
    L, D = h.shape
    N = w_cat.shape[1]
    nqi = IDX_HEADS * IDX_DIM
    row = lambda n: pl.BlockSpec((tm, n), lambda i: (i, 0))
    return pl.pallas_call(
        functools.partial(_dsa_proj_kernel, d=D, scale=(D // DSA_HEADS) ** -0.5 * LOG2E,
                          wi_scale=IDX_HEADS ** -0.5 * IDX_DIM ** -0.5),
        grid=(L // tm,),
        in_specs=[row(D), _const_spec((1, D)), _const_spec((D, N)),
                  _const_spec((2, LANES)), _const_spec((2, LANES))],
        out_specs=[row(D), row(D), row(D), row(nqi), row(LANES), row(LANES), row(LANES)],
        out_shape=[jax.ShapeDtypeStruct((L, D), MXU_DTYPE)] * 3
        + [jax.ShapeDtypeStruct((L, nqi), MXU_DTYPE)]
        + [jax.ShapeDtypeStruct((L, LANES), MXU_DTYPE)] * 2
        + [jax.ShapeDtypeStruct((L, LANES), F32)],
        compiler_params=_params("arbitrary"),
        name="dsa_proj",
    )(h, g.reshape(1, D), w_cat, lg2, lb2)


I16 = jnp.int16
I16_MIN = -(2 ** 15)


def _dsa_select_kernel(qi_ref, wi_ref, kia_ref, kib_ref, bias_ref, hi_ref, lo_ref, cand_ref,
                       *, tq, tk, topk, rb):
    i = pl.program_id(0)
    n_tiles = hi_ref.shape[0]
    n_kt = ((i + 1) * tq + tk - 1) // tk
    q0 = i * tq
    qpos = q0 + lax.broadcasted_iota(jnp.int32, (tq, tk), 0)
    lane_pos = lax.broadcasted_iota(jnp.int32, (tq, tk), 1)

    def score_tile(c, carry):
        k0 = pl.multiple_of(c * tk, tk)
        ka = kia_ref[pl.ds(k0, tk), :]
        kb = kib_ref[pl.ds(k0, tk), :]
        acc = jnp.zeros((tq, tk), F32)
        for p in range(IDX_HEADS // 2):
            lhs = qi_ref[:, p * LANES:(p + 1) * LANES]
            sa = jnp.maximum(_dot_nt(lhs, ka), 0.0)
            sb = jnp.maximum(_dot_nt(lhs, kb), 0.0)
            acc = acc + wi_ref[:, 2 * p:2 * p + 1] * sa
            acc = acc + wi_ref[:, 2 * p + 1:2 * p + 2] * sb
        bits = pltpu.bitcast(acc, jnp.int32)
        key = bits ^ ((bits >> 31) & jnp.int32(0x7FFFFFFF))
        key = jnp.where(k0 + lane_pos <= qpos, key, jnp.int32(INT_MIN))
        hi_ref[c] = (key >> 16).astype(I16)
        lo_ref[c] = ((key & 0xFFFF) + I16_MIN).astype(I16)
        return carry

    lax.fori_loop(0, n_kt, score_tile, 0)

    def count(pred, row_starts=tuple(range(0, tq, rb))):
        def body(c, acc):
            parts = []
            for r0 in row_starts:
                cand = cand_ref[r0:r0 + rb, :]
                part = jnp.zeros((rb, LANES), I16)
                for t in range(tk // LANES):
                    ls = slice(t * LANES, (t + 1) * LANES)
                    hit = pred(hi_ref[c, r0:r0 + rb, ls], lo_ref[c, r0:r0 + rb, ls], cand,
                               c * tk + t * LANES, r0)
                    part = part + hit.astype(I16)
                parts.append(part)
            return acc + jnp.concatenate(parts, axis=0)
        acc = lax.fori_loop(0, n_kt, body, jnp.zeros((len(row_starts) * rb, LANES), I16))
        return jnp.sum(acc.astype(F32), axis=1, keepdims=True)

    def set_cand(x):
        cand_ref[...] = jnp.broadcast_to(x, (tq, LANES)).astype(I16)

    def bisect16(pred, n_ge0):
        def bit_step(b, carry):
            c_best, n_best = carry
            cand = c_best + lax.shift_left(jnp.int32(1), 15 - b)
            set_cand(cand)
            cnt = count(pred)
            keep = cnt >= topk
            return jnp.where(keep, cand, c_best), jnp.where(keep, cnt, n_best)
        return lax.fori_loop(0, 16, bit_step, (jnp.full((tq, 1), I16_MIN, jnp.int32), n_ge0))

    n_all = jnp.broadcast_to((n_kt * tk).astype(F32), (tq, 1))
    thr_hi, n_hi = bisect16(lambda hi, lo, cd, k0, r0: hi >= cd, n_all)
    thr_hi16 = jnp.broadcast_to(thr_hi, (tq, LANES)).astype(I16)

    def clamp_low(c, carry):
        for r0 in range(0, tq, rb):
            hi = hi_ref[c, r0:r0 + rb, :]
            th = jnp.concatenate([thr_hi16[r0:r0 + rb]] * (tk // LANES), axis=1)
            side = jnp.where(hi > th, jnp.asarray(2 ** 15 - 1, I16), jnp.asarray(I16_MIN, I16))
            lo_ref[c, r0:r0 + rb, :] = jnp.where(hi == th, lo_ref[c, r0:r0 + rb, :], side)
        return carry

    lax.fori_loop(0, n_kt, clamp_low, 0)
    thr_lo, n_ge = bisect16(lambda hi, lo, cd, k0, r0: lo >= cd, n_hi)
    thr_lo16 = jnp.broadcast_to(thr_lo, (tq, LANES)).astype(I16)
    is_floor = jnp.logical_and(thr_hi == I16_MIN, thr_lo == I16_MIN)

    def gt_thr(hi, lo, r0):
        th, tl = thr_hi16[r0:r0 + rb], thr_lo16[r0:r0 + rb]
        return jnp.logical_or(hi > th, jnp.logical_and(hi == th, lo > tl))

    def eq_thr(hi, lo, r0):
        return jnp.logical_and(hi == thr_hi16[r0:r0 + rb], lo == thr_lo16[r0:r0 + rb])

    cand_ref[...] = jnp.broadcast_to(jnp.where(is_floor, -1, 2 ** 15 - 1), (tq, LANES)).astype(I16)
    lane_idx = lax.broadcasted_iota(jnp.int32, (rb, LANES), 1)
    for r0 in range(0, tq, rb):
        rs = slice(r0, r0 + rb)
        excess = jnp.max(jnp.where(is_floor[rs], 0.0, n_ge[rs] - topk))

        @pl.when(excess > 0.0)
        def _ties(r0=r0, rs=rs):
            need = topk - count(lambda hi, lo, cd, k0, r: gt_thr(hi, lo, r), (r0,))

            def idx_step(b, cut):
                cand = cut + lax.shift_left(jnp.int32(1), 14 - b)
                cand_ref[rs, :] = jnp.broadcast_to(cand, (rb, LANES)).astype(I16)
                cnt = count(lambda hi, lo, cd, k0, r: jnp.logical_and(
                    eq_thr(hi, lo, r), (k0 + lane_idx).astype(I16) < cd), (r0,))
                return jnp.where(cnt < need, cand, cut)

            cut = lax.fori_loop(0, 15, idx_step, jnp.zeros((rb, 1), jnp.int32))
            cand_ref[rs, :] = jnp.broadcast_to(jnp.where(is_floor[rs], -1, cut), (rb, LANES)).astype(I16)

    lane_idx = lax.broadcasted_iota(jnp.int32, (rb, tk), 1)

    def write_tile(c, carry):
        for r0 in range(0, tq, rb):
            hi = hi_ref[c, r0:r0 + rb, :]
            lo = lo_ref[c, r0:r0 + rb, :]
            th = jnp.concatenate([thr_hi16[r0:r0 + rb]] * (tk // LANES), axis=1)
            tl = jnp.concatenate([thr_lo16[r0:r0 + rb]] * (tk // LANES), axis=1)
            cut = jnp.concatenate([cand_ref[r0:r0 + rb, :]] * (tk // LANES), axis=1)
            gt = jnp.logical_or(hi > th, jnp.logical_and(hi == th, lo > tl))
            eq = jnp.logical_and(jnp.logical_and(hi == th, lo == tl),
                                 (c * tk + lane_idx).astype(I16) <= cut)
            keep = jnp.logical_or(gt, eq)
            bias_ref[c, r0:r0 + rb, :] = jnp.where(
                keep, jnp.zeros((), SOFTMAX_DTYPE), jnp.asarray(NEG_BIG, SOFTMAX_DTYPE))
        return carry

    lax.fori_loop(0, n_kt, write_tile, 0)

    def fill_tile(c, carry):
        bias_ref[c] = jnp.full((tq, tk), NEG_BIG, SOFTMAX_DTYPE)
        return carry

    lax.fori_loop(n_kt, n_tiles, fill_tile, 0)


def _dsa_select(qi, kia, kib, wi, topk, tq=256, tk=512, rb=64):
    L, nqi = qi.shape
    assert L // LANES < 2 ** 15 and L < 2 ** 15
    n_tiles = L // tk
    return pl.pallas_call(
        functools.partial(_dsa_select_kernel, tq=tq, tk=tk, topk=topk, rb=rb),
        grid=(L // tq,),
        in_specs=[
            pl.BlockSpec((tq, nqi), lambda i: (i, 0)),
            pl.BlockSpec((tq, LANES), lambda i: (i, 0)),
            _const_spec((L, LANES)),
            _const_spec((L, LANES)),
        ],
        out_specs=pl.BlockSpec((n_tiles, tq, tk), lambda i: (0, i, 0)),
        out_shape=jax.ShapeDtypeStruct((n_tiles, L, tk), SOFTMAX_DTYPE),
        scratch_shapes=[
            pltpu.VMEM((n_tiles, tq, tk), I16),
            pltpu.VMEM((n_tiles, tq, tk), I16),
            pltpu.VMEM((tq, LANES), I16),
        ],
        compiler_params=_params("arbitrary"),
        name="dsa_select",
    )(qi, wi, kia, kib)


def _dsa_attn_kernel(qi_s, kj_s, q_ref, k_ref, v_ref, bias_ref, o_ref, m_ref, l_ref, acc_ref,
                     *, tq, tk, nheads):
    step = pl.program_id(0)
    i = qi_s[step]
    j = kj_s[step]
    n_kt = ((i + 1) * tq + tk - 1) // tk

    @pl.when(j == 0)
    def _init():
        m_ref[...] = jnp.full_like(m_ref, NEG_BIG)
        l_ref[...] = jnp.zeros_like(l_ref)
        acc_ref[...] = jnp.zeros_like(acc_ref)

    bias = jnp.concatenate([bias_ref[t] for t in range(bias_ref.shape[0])], axis=1)
    _flash_heads(q_ref, k_ref, v_ref, m_ref, l_ref, acc_ref, bias, nheads,
                 lambda h: slice(h * LANES, (h + 1) * LANES))

    @pl.when(j == n_kt - 1)
    def _finish():
        _flash_finish(o_ref, l_ref, acc_ref, nheads)


def _dsa_attn(q, k, v, bias, tq=512, tk=1024):
    L, D = q.shape
    nheads = DSA_HEADS
    tb = bias.shape[2]
    assert D == nheads * LANES and bias.shape == (L // tb, L, tb) and tk % tb == 0
    qi_s, kj_s = _causal_steps(L // tq, tq, tk)
    qmap = lambda s, qs, ks: (qs[s], 0)
    kmap = lambda s, qs, ks: (ks[s], 0)
    grid_spec = pltpu.PrefetchScalarGridSpec(
        num_scalar_prefetch=2,
        grid=(qi_s.shape[0],),
        in_specs=[
            pl.BlockSpec((tq, D), qmap),
            pl.BlockSpec((tk, D), kmap),
            pl.BlockSpec((tk, D), kmap),
            pl.BlockSpec((tk // tb, tq, tb), lambda s, qs, ks: (ks[s], qs[s], 0)),
        ],
        out_specs=pl.BlockSpec((tq, D), qmap),
        scratch_shapes=[
            pltpu.VMEM((nheads, tq, LANES), F32),
            pltpu.VMEM((nheads, tq, LANES), F32),
            pltpu.VMEM((tq, D), F32),
        ],
    )
    return pl.pallas_call(
        functools.partial(_dsa_attn_kernel, tq=tq, tk=tk, nheads=nheads),
        grid_spec=grid_spec,
        out_shape=jax.ShapeDtypeStruct((L, D), MXU_DTYPE),
        compiler_params=_params("arbitrary"),
        name="dsa_attn",
    )(qi_s, kj_s, q, k, v, bias)


def _dsa_layer(h, g, w_in, idx_k_g, idx_k_b, w_out):
    L, D = h.shape
    nqi = IDX_HEADS * IDX_DIM
    o3 = 3 * D
    o4 = o3 + nqi
    o5 = o4 + IDX_DIM
    w_ki = w_in[:, o4:o5]
    w_wi = w_in[:, o5:]
    zpad = lambda n: jnp.zeros((D, n), w_in.dtype)
    w_cat = jnp.concatenate(
        [w_in[:, :o4],
         w_ki, w_wi, zpad(LANES - IDX_DIM - IDX_HEADS),
         w_wi, zpad(LANES - IDX_DIM - IDX_HEADS), w_ki], axis=1).astype(MXU_DTYPE)
    z = jnp.zeros((IDX_DIM,), F32)
    lg2 = jnp.stack([jnp.concatenate([idx_k_g, z]), jnp.concatenate([z, idx_k_g])])
    lb2 = jnp.stack([jnp.concatenate([idx_k_b, z]), jnp.concatenate([z, idx_k_b])])
    q, k, v, qi, kia, kib, wi = _dsa_proj(h, g, w_cat, lg2, lb2)
    bias = _dsa_select(qi, kia, kib, wi, topk=min(TOPK_MAX, L // 4))
    o = _dsa_attn(q, k, v, bias)
    return _out_proj(o, w_out.astype(MXU_DTYPE), jnp.zeros((D,), F32), h)


def _glu_proj_kernel(h_ref, g_ref, w_ref, b_ref, u_ref, *, d):
    xn = _rmsnorm_rows(h_ref[...], g_ref[...]).astype(MXU_DTYPE)
    cw = 512
    for c in range(0, d, cw):
        a = jnp.dot(xn, w_ref[:, c:c + cw], preferred_element_type=F32) + b_ref[:, c:c + cw]
        gt = jnp.dot(xn, w_ref[:, d + c:d + c + cw], preferred_element_type=F32) + b_ref[:, d + c:d + c + cw]
        u_ref[:, c:c + cw] = a * _sigmoid(gt)


def _glu_proj(h, g, w, b, tm=512):
    L, D = h.shape
    return pl.pallas_call(
        functools.partial(_glu_proj_kernel, d=D),
        grid=(L // tm,),
        in_specs=[pl.BlockSpec((tm, D), lambda i: (i, 0)), _const_spec((1, D)),
                  _const_spec((D, 2 * D)), _const_spec((1, 2 * D))],
        out_specs=pl.BlockSpec((tm, D), lambda i: (i, 0)),
        out_shape=jax.ShapeDtypeStruct((L, D), F32),
        compiler_params=_params("arbitrary"),
        name="conv_glu_proj",
    )(h, g.reshape(1, D), w, b.reshape(1, 2 * D))


CONV_HALO = 32


def _conv_out_kernel(u_ref, up_ref, wdw_ref, bdw_ref, lg_ref, lb_ref, w2_ref, b2_ref, h_ref, o_ref,
                     x_ref, *, tm):
    i = pl.program_id(0)
    x_ref[CONV_HALO:, :] = u_ref[...]
    x_ref[:CONV_HALO, :] = jnp.where(i > 0, up_ref[...], 0.0)
    off = CONV_HALO - (CONV_WIDTH - 1)
    acc = jnp.zeros(u_ref.shape, F32)
    for t in range(CONV_WIDTH):
        acc = acc + wdw_ref[t:t + 1, :] * x_ref[off + t:off + t + tm, :]
    y = acc + bdw_ref[...]
    mu = jnp.mean(y, axis=-1, keepdims=True)
    dlt = y - mu
    var = jnp.mean(dlt * dlt, axis=-1, keepdims=True)
    y = _silu(dlt * lax.rsqrt(var + EPS) * lg_ref[...] + lb_ref[...])
    o_ref[...] = h_ref[...] + _dot(y, w2_ref[...]) + b2_ref[...]


def _conv_out(u, w_dw, b_dw, ln_g, ln_b, w2, b2, h, tm=256):
    L, D = u.shape
    r = tm // CONV_HALO
    vec = lambda a: a.reshape(1, D)
    return pl.pallas_call(
        functools.partial(_conv_out_kernel, tm=tm),
        grid=(L // tm,),
        in_specs=[
            pl.BlockSpec((tm, D), lambda i: (i, 0)),
            pl.BlockSpec((CONV_HALO, D), lambda i: (jnp.maximum(i * r - 1, 0), 0)),
            _const_spec((CONV_WIDTH, D)),
            _const_spec((1, D)), _const_spec((1, D)), _const_spec((1, D)),
            _const_spec((D, D)), _const_spec((1, D)),
            pl.BlockSpec((tm, D), lambda i: (i, 0)),
        ],
        out_specs=pl.BlockSpec((tm, D), lambda i: (i, 0)),
        out_shape=jax.ShapeDtypeStruct((L, D), F32),
        scratch_shapes=[pltpu.VMEM((tm + CONV_HALO, D), F32)],
        compiler_params=_params("arbitrary"),
        name="conv_out",
    )(u, u, w_dw, vec(b_dw), vec(ln_g), vec(ln_b), w2, vec(b2), h)


def _conv_layer(h, g, w_pw1, b_pw1, w_dw, b_dw, ln_g, ln_b, w_pw2, b_pw2):
    u = _glu_proj(h, g, w_pw1.astype(MXU_DTYPE), b_pw1)
    return _conv_out(u, w_dw, b_dw, ln_g, ln_b, w_pw2.astype(MXU_DTYPE), b_pw2, h)


def _mla_prep_kernel(h_ref, g_ref, w1_ref, qg_ref, kvg_ref, wqa_ref, wqb_ref, wka_ref, wv_ref,
                     c_ref, s_ref, q_ref, k_ref, v_ref, *, scale):
    xn = _rmsnorm_rows(h_ref[...], g_ref[...]).astype(MXU_DTYPE)
    proj = jnp.dot(xn, w1_ref[...], preferred_element_type=F32)
    o1 = MLA_Q_LORA
    o2 = o1 + MLA_KV_LORA
    cq = _rmsnorm_rows(proj[:, :o1], qg_ref[...]).astype(MXU_DTYPE)
    ckv = _rmsnorm_rows(proj[:, o1:o2], kvg_ref[...]).astype(MXU_DTYPE)
    cos = c_ref[...]
    sin = s_ref[...]
    kr = proj[:, o2:o2 + LANES] * cos + proj[:, o2 + LANES:o2 + 2 * LANES] * sin
    for hd in range(MLA_HEADS):
        cs = slice(hd * LANES, (hd + 1) * LANES)
        qa = jnp.dot(cq, wqa_ref[:, cs], preferred_element_type=F32)
        qb = jnp.dot(cq, wqb_ref[:, cs], preferred_element_type=F32)
        q_ref[:, cs] = ((qa * cos + qb * sin) * scale).astype(q_ref.dtype)
        ka = jnp.dot(ckv, wka_ref[:, cs], preferred_element_type=F32)
        k_ref[:, cs] = (ka + kr).astype(k_ref.dtype)
    v_ref[...] = jnp.dot(ckv, wv_ref[...], preferred_element_type=F32).astype(v_ref.dtype)


def _mla_prep(h, g, w1, qg, kvg, wqa, wqb, wka, wv, ctab, stab, tm=256):
    L, D = h.shape
    HP = MLA_HEADS * LANES
    NV = MLA_HEADS * MLA_V
    row = lambda n: pl.BlockSpec((tm, n), lambda i: (i, 0))
    full = lambda a: _const_spec(a.shape)
    return pl.pallas_call(
        functools.partial(_mla_prep_kernel, scale=(MLA_NOPE + MLA_ROPE) ** -0.5 * LOG2E),
        grid=(L // tm,),
        in_specs=[row(D), _const_spec((1, D)), full(w1), _const_spec((1, MLA_Q_LORA)),
                  _const_spec((1, MLA_KV_LORA)), full(wqa), full(wqb), full(wka), full(wv),
                  row(LANES), row(LANES)],
        out_specs=[row(HP), row(HP), row(NV)],
        out_shape=[jax.ShapeDtypeStruct((L, HP), MXU_DTYPE), jax.ShapeDtypeStruct((L, HP), MXU_DTYPE),
                   jax.ShapeDtypeStruct((L, NV), MXU_DTYPE)],
        compiler_params=_params("arbitrary"),
        name="mla_prep",
    )(h, g.reshape(1, D), w1, qg.reshape(1, -1), kvg.reshape(1, -1), wqa, wqb, wka, wv, ctab, stab)


def _mla_attn_kernel(qi_s, kj_s, q_ref, k_ref, v_ref, o_ref, m_ref, l_ref, acc_ref, *, tq, tk, nheads):
    step = pl.program_id(0)
    i = qi_s[step]
    j = kj_s[step]
    n_kt = ((i + 1) * tq + tk - 1) // tk

    @pl.when(j == 0)
    def _init():
        m_ref[...] = jnp.full_like(m_ref, NEG_BIG)
        l_ref[...] = jnp.zeros_like(l_ref)
        acc_ref[...] = jnp.zeros_like(acc_ref)

    qpos = i * tq + lax.broadcasted_iota(jnp.int32, (tq, tk), 0)
    kpos = j * tk + lax.broadcasted_iota(jnp.int32, (tq, tk), 1)
    bias = jnp.where(kpos <= qpos, 0.0, NEG_BIG).astype(SOFTMAX_DTYPE)
    _flash_heads(q_ref, k_ref, v_ref, m_ref, l_ref, acc_ref, bias, nheads,
                 lambda h: slice((h // 2) * LANES, (h // 2 + 1) * LANES))

    @pl.when(j == n_kt - 1)
    def _finish():
        _flash_finish(o_ref, l_ref, acc_ref, nheads)


def _mla_attn(q, k, v, tq=512, tk=1024):
    L, HP = q.shape
    NV = v.shape[1]
    qi_s, kj_s = _causal_steps(L // tq, tq, tk)
    qmap = lambda s, qs, ks: (qs[s], 0)
    kmap = lambda s, qs, ks: (ks[s], 0)
    grid_spec = pltpu.PrefetchScalarGridSpec(
        num_scalar_prefetch=2,
        grid=(qi_s.shape[0],),
        in_specs=[pl.BlockSpec((tq, HP), qmap),
                  pl.BlockSpec((tk, HP), kmap), pl.BlockSpec((tk, NV), kmap)],
        out_specs=pl.BlockSpec((tq, HP), qmap),
        scratch_shapes=[pltpu.VMEM((MLA_HEADS, tq, LANES), F32),
                        pltpu.VMEM((MLA_HEADS, tq, LANES), F32),
                        pltpu.VMEM((tq, HP), F32)],
    )
    return pl.pallas_call(
        functools.partial(_mla_attn_kernel, tq=tq, tk=tk, nheads=MLA_HEADS),
        grid_spec=grid_spec,
        out_shape=jax.ShapeDtypeStruct((L, HP), MXU_DTYPE),
        compiler_params=_params("arbitrary"),
        name="mla_attn",
    )(qi_s, kj_s, q, k, v)


def _place_heads(w, nheads, src_w, lane0, swap_half=0):
    K = w.shape[0]
    w3 = w.reshape(K, nheads, src_w)
    if swap_half:
        w3 = jnp.concatenate([w3[..., swap_half:], w3[..., :swap_half]], axis=-1)
    out = jnp.zeros((K, nheads, LANES), w.dtype)
    out = out.at[:, :, lane0:lane0 + src_w].set(w3)
    return out.reshape(K, nheads * LANES)


def _mla_layer(h, g, w_in, q_norm_g, w_uq, kv_norm_g, w_ukv, w_out):
    L, D = h.shape
    half = MLA_ROPE // 2
    o2 = MLA_Q_LORA + MLA_KV_LORA
    w_kr = w_in[:, o2:]
    w1 = jnp.concatenate(
        [w_in[:, :o2], _place_heads(w_kr, 1, MLA_ROPE, MLA_NOPE),
         _place_heads(w_kr, 1, MLA_ROPE, MLA_NOPE, swap_half=half)], axis=1).astype(MXU_DTYPE)
    uq = w_uq.reshape(MLA_Q_LORA, MLA_HEADS, MLA_NOPE + MLA_ROPE)
    uq_rope = uq[..., MLA_NOPE:].reshape(MLA_Q_LORA, MLA_HEADS * MLA_ROPE)
    wqa = _place_heads(w_uq, MLA_HEADS, MLA_NOPE + MLA_ROPE, 0).astype(MXU_DTYPE)
    wqb = _place_heads(uq_rope, MLA_HEADS, MLA_ROPE, MLA_NOPE, swap_half=half).astype(MXU_DTYPE)
    ukv = w_ukv.reshape(MLA_KV_LORA, MLA_HEADS, MLA_NOPE + MLA_V)
    wka = _place_heads(ukv[..., :MLA_NOPE].reshape(MLA_KV_LORA, -1), MLA_HEADS, MLA_NOPE, 0).astype(MXU_DTYPE)
    wv = ukv[..., MLA_NOPE:].reshape(MLA_KV_LORA, MLA_HEADS * MLA_V).astype(MXU_DTYPE)
    pos = jnp.arange(L, dtype=F32)
    inv_freq = ROPE_THETA ** (-jnp.arange(0, MLA_ROPE, 2, dtype=F32) / MLA_ROPE)
    ang = pos[:, None] * inv_freq[None, :]
    cos, sin = jnp.cos(ang), jnp.sin(ang)
    zeros = jnp.zeros((L, LANES - MLA_NOPE - MLA_ROPE), F32)
    ctab = jnp.concatenate([jnp.ones((L, MLA_NOPE), F32), cos, cos, zeros], axis=1)
    stab = jnp.concatenate([jnp.zeros((L, MLA_NOPE), F32), -sin, sin, zeros], axis=1)
    q, k, v = _mla_prep(h, g, w1, q_norm_g, kv_norm_g, wqa, wqb, wka, wv, ctab, stab)
    o = _mla_attn(q, k, v)
    wo = w_out.reshape(MLA_HEADS // 2, 2, MLA_V, D)
    zo = jnp.zeros_like(wo[:, 0])
    wo_pad = jnp.stack([jnp.concatenate([wo[:, 0], zo], axis=1),
                        jnp.concatenate([zo, wo[:, 1]], axis=1)], axis=1)
    wo_pad = wo_pad.reshape(MLA_HEADS * LANES, D).astype(MXU_DTYPE)
    return _out_proj(o, wo_pad, jnp.zeros((D,), F32), h)


def _gdn_scan_kernel(qkv_ref, ba_ref, cw_ref, alog_ref, dtb_ref, o_ref, x_ref, s_ref,
                     *, nheads, dk, dv, c):
    n = pl.program_id(0)
    halo = 8
    nqk = nheads * dk

    @pl.when(n == 0)
    def _init():
        x_ref[:halo, :] = jnp.zeros((halo, x_ref.shape[1]), F32)
        s_ref[...] = jnp.zeros_like(s_ref)

    x_ref[halo:, :] = qkv_ref[...]
    off = halo - (GDN_CONV - 1)
    acc = jnp.zeros(qkv_ref.shape, F32)
    for t in range(GDN_CONV):
        acc = acc + cw_ref[t:t + 1, :] * x_ref[off + t:off + t + c, :]
    x_ref[:halo, :] = qkv_ref[c - halo:, :]
    qkv = _silu(acc)

    beta_all = _sigmoid(ba_ref[:, :LANES])
    a_raw = ba_ref[:, LANES:] + dtb_ref[...]
    softplus = jnp.maximum(a_raw, 0.0) + jnp.log(1.0 + jnp.exp(-jnp.abs(a_raw)))
    g_all = -jnp.exp(alog_ref[...]) * softplus
    row = lax.broadcasted_iota(jnp.int32, (c, c), 0)
    col = lax.broadcasted_iota(jnp.int32, (c, c), 1)
    tril = row >= col
    tril_strict = row > col
    gc_all = jnp.dot(tril.astype(F32), g_all, preferred_element_type=F32,
                     precision=lax.Precision.HIGHEST)
    gc_t = gc_all.T

    heads = range(nheads)
    q = [qkv[:, hd * dk:(hd + 1) * dk] for hd in heads]
    k = [qkv[:, nqk + hd * dk:nqk + (hd + 1) * dk] for hd in heads]
    v = [qkv[:, 2 * nqk + hd * dv:2 * nqk + (hd + 1) * dv] for hd in heads]
    q = [x * lax.rsqrt(jnp.sum(x * x, axis=-1, keepdims=True) + EPS) * (dk ** -0.5) for x in q]
    k = [x * lax.rsqrt(jnp.sum(x * x, axis=-1, keepdims=True) + EPS) for x in k]
    beta = [beta_all[:, hd:hd + 1] for hd in heads]
    gc = [gc_all[:, hd:hd + 1] for hd in heads]
    gc_last = [gc_all[c - 1:c, hd:hd + 1] for hd in heads]
    decay = [jnp.where(tril, jnp.exp(jnp.where(tril, gc[hd] - gc_t[hd:hd + 1, :], 0.0)), 0.0)
             for hd in heads]
    kk = [_dot_nt(k[hd], k[hd]) for hd in heads]
    qk = [_dot_nt(q[hd], k[hd]) for hd in heads]
    a_mat = [jnp.where(tril_strict, beta[hd] * kk[hd] * decay[hd], 0.0) for hd in heads]
    eye = jnp.where(row == col, 1.0, 0.0)
    t_inv = [eye - jnp.where((row >> 1) == (col >> 1), a_mat[hd], 0.0) for hd in heads]
    for lvl in range(1, int(math.log2(c))):
        blk_mask = jnp.logical_and((row >> (lvl + 1)) == (col >> (lvl + 1)), (row >> lvl) != (col >> lvl))
        tm = [_dot(t_inv[hd], jnp.where(blk_mask, a_mat[hd], 0.0)) for hd in heads]
        tmt = [_dot(tm[hd], t_inv[hd]) for hd in heads]
        t_inv = [t_inv[hd] - tmt[hd] for hd in heads]
    e_gc = [jnp.exp(gc[hd]) for hd in heads]
    u = [_dot(t_inv[hd], v[hd] * beta[hd]) for hd in heads]
    w = [_dot(t_inv[hd], k[hd] * (beta[hd] * e_gc[hd])) for hd in heads]
    s = [s_ref[hd] for hd in heads]
    ws = [_dot(w[hd], s[hd]) for hd in heads]
    qs = [_dot(q[hd] * e_gc[hd], s[hd]) for hd in heads]
    v_new = [u[hd] - ws[hd] for hd in heads]
    k_dec_t = [(k[hd] * jnp.exp(gc_last[hd] - gc[hd])).T for hd in heads]
    o_intra = [_dot(qk[hd] * decay[hd], v_new[hd]) for hd in heads]
    s_upd = [_dot(k_dec_t[hd], v_new[hd]) for hd in heads]
    for hd in heads:
        o_ref[:, hd * dv:(hd + 1) * dv] = qs[hd] + o_intra[hd]
        s_ref[hd] = s[hd] * jnp.exp(gc_last[hd]) + s_upd[hd]


def _gdn_scan(proj, conv_w, a_log_pad, dt_bias_pad):
    L = proj.shape[0]
    c = GDN_CHUNK
    nqkv = 2 * GDN_HEADS * GDN_DK + GDN_HEADS * GDN_DV
    nv = GDN_HEADS * GDN_DV
    ba_block = (nqkv + nv) // (2 * LANES)
    return pl.pallas_call(
        functools.partial(_gdn_scan_kernel, nheads=GDN_HEADS, dk=GDN_DK, dv=GDN_DV, c=c),
        grid=(L // c,),
        in_specs=[pl.BlockSpec((c, nqkv), lambda n: (n, 0)),
                  pl.BlockSpec((c, 2 * LANES), lambda n: (n, ba_block)),
                  _const_spec((GDN_CONV, nqkv)), _const_spec((1, LANES)), _const_spec((1, LANES))],
        out_specs=pl.BlockSpec((c, nv), lambda n: (n, 0)),
        out_shape=jax.ShapeDtypeStruct((L, nv), F32),
        scratch_shapes=[pltpu.VMEM((c + 8, nqkv), F32),
                        pltpu.VMEM((GDN_HEADS, GDN_DK, GDN_DV), F32)],
        compiler_params=_params("arbitrary"),
        name="gdn_scan",
    )(proj, proj, conv_w, a_log_pad, dt_bias_pad)


def _gdn_out_kernel(o_ref, gate_ref, og_ref, w_ref, h_ref, y_ref, *, nheads, dv):
    parts = []
    for hd in range(nheads):
        cs = slice(hd * dv, (hd + 1) * dv)
        on = _rmsnorm_rows(o_ref[:, cs], og_ref[...])
        parts.append((on * _silu(gate_ref[:, cs])).astype(MXU_DTYPE))
    y_ref[...] = h_ref[...] + jnp.dot(jnp.concatenate(parts, axis=1), w_ref[...],
                                      preferred_element_type=F32)


def _gdn_out(o, proj, o_norm_g, w_out, h, tm=512):
    L, D = h.shape
    nv = GDN_HEADS * GDN_DV
    gate_block = (2 * GDN_HEADS * GDN_DK + nv) // nv
    return pl.pallas_call(
        functools.partial(_gdn_out_kernel, nheads=GDN_HEADS, dv=GDN_DV),
        grid=(L // tm,),
        in_specs=[pl.BlockSpec((tm, nv), lambda i: (i, 0)),
                  pl.BlockSpec((tm, nv), lambda i: (i, gate_block)),
                  _const_spec((1, GDN_DV)), _const_spec((nv, D)),
                  pl.BlockSpec((tm, D), lambda i: (i, 0))],
        out_specs=pl.BlockSpec((tm, D), lambda i: (i, 0)),
        out_shape=jax.ShapeDtypeStruct((L, D), F32),
        compiler_params=_params("arbitrary"),
        name="gdn_out",
    )(o, proj, o_norm_g.reshape(1, GDN_DV), w_out, h)


def _gdn_layer(h, g, w_in, conv_w, a_log, dt_bias, o_norm_g, w_out):
    L, D = h.shape
    nmain = 2 * GDN_HEADS * GDN_DK + 2 * GDN_HEADS * GDN_DV
    zpad = jnp.zeros((D, LANES - GDN_HEADS), w_in.dtype)
    w_cat = jnp.concatenate([w_in[:, :nmain], w_in[:, nmain:nmain + GDN_HEADS], zpad,
                             w_in[:, nmain + GDN_HEADS:], zpad], axis=1).astype(MXU_DTYPE)
    proj = _norm_proj(h, g, w_cat, tn=w_cat.shape[1] // 2)
    vpad = lambda a: jnp.concatenate([a.astype(F32), jnp.zeros((LANES - GDN_HEADS,), F32)]).reshape(1, LANES)
    o = _gdn_scan(proj, conv_w, vpad(a_log), vpad(dt_bias))
    return _gdn_out(o, proj, o_norm_g, w_out.astype(MXU_DTYPE), h)


def kernel(x, norm_mix_g, norm_mlp_g, final_g, mlp_w1, mlp_w2, dsa_w_in, dsa_idx_k_g, dsa_idx_k_b, dsa_w_out, conv_w_pw1, conv_b_pw1, conv_w_dw, conv_b_dw, conv_ln_g, conv_ln_b, conv_w_pw2, conv_b_pw2, mla_w_in, mla_q_norm_g, mla_w_uq, mla_kv_norm_g, mla_w_ukv, mla_w_out, gdn_w_in, gdn_conv_w, gdn_a_log, gdn_dt_bias, gdn_o_norm_g, gdn_w_out):
    b, L, D = x.shape
    depth = norm_mix_g.shape[0]
    outs = []
    for bi in range(b):
        h = x[bi]
        for i in range(depth):
            m = i % 4
            jl = i // 4
            g = norm_mix_g[i]
            if m == 0:
                h = _dsa_layer(h, g, dsa_w_in[jl], dsa_idx_k_g[jl], dsa_idx_k_b[jl], dsa_w_out[jl])
            elif m == 1:
                h = _conv_layer(h, g, conv_w_pw1[jl], conv_b_pw1[jl], conv_w_dw[jl], conv_b_dw[jl],
                                conv_ln_g[jl], conv_ln_b[jl], conv_w_pw2[jl], conv_b_pw2[jl])
            elif m == 2:
                h = _mla_layer(h, g, mla_w_in[jl], mla_q_norm_g[jl], mla_w_uq[jl], mla_kv_norm_g[jl],
                               mla_w_ukv[jl], mla_w_out[jl])
            else:
                h = _gdn_layer(h, g, gdn_w_in[jl], gdn_conv_w[jl], gdn_a_log[jl], gdn_dt_bias[jl],
                               gdn_o_norm_g[jl], gdn_w_out[jl])
            h = _mlp(h, norm_mlp_g[i], mlp_w1[i].astype(MXU_DTYPE), mlp_w2[i].astype(MXU_DTYPE),
                     final_g, final_norm=(i == depth - 1))
        outs.append(h)
    return jnp.stack(outs)
```

```python
import functools
import math

import jax
import jax.numpy as jnp
import numpy as np
from jax import lax
from jax.experimental import pallas as pl
from jax.experimental.pallas import tpu as pltpu

F32 = jnp.float32
MXU_DTYPE = jnp.bfloat16
EPS = 1e-6
LANES = 128
SUBLANES = 8
VMEM_LIMIT_BYTES = 56 * 1024 * 1024
SOFTMAX_DTYPE = jnp.bfloat16
NEG_BIG = -(2.0 ** 100)
INT_MIN = -(2 ** 31)

Q_BLOCK = 128
IDX_HEADS = 8
IDX_DIM = 64
TOPK_MAX = 256
DSA_HEADS = 8
CONV_WIDTH = 31
MLA_HEADS = 16
MLA_Q_LORA = 384
MLA_KV_LORA = 256
MLA_NOPE = 64
MLA_ROPE = 32
MLA_V = 64
ROPE_THETA = 10000.0
GDN_HEADS = 8
GDN_DK = 128
GDN_DV = 128
GDN_CONV = 4
GDN_CHUNK = 64


def _params(*sem):
    return pltpu.CompilerParams(dimension_semantics=sem, vmem_limit_bytes=VMEM_LIMIT_BYTES)


def _dot(a, b):
    return jnp.dot(a.astype(MXU_DTYPE), b.astype(MXU_DTYPE), preferred_element_type=F32)


def _dot_nt(a, b):
    return lax.dot_general(a.astype(MXU_DTYPE), b.astype(MXU_DTYPE),
                           (((1,), (1,)), ((), ())), preferred_element_type=F32)


def _rmsnorm_rows(x, g):
    return x * lax.rsqrt(jnp.mean(x * x, axis=-1, keepdims=True) + EPS) * g


def _sigmoid(x):
    return 1.0 / (1.0 + jnp.exp(-x))


def _silu(x):
    return x * _sigmoid(x)


def _const_spec(shape):
    return pl.BlockSpec(shape, lambda *_: (0,) * len(shape))


LOG2E = math.log2(math.e)
LOGITS_AHEAD = 2


def _causal_steps(n_q, tq, tk):
    qi, kj = [], []
    for i in range(n_q):
        for j in range(((i + 1) * tq - 1) // tk + 1):
            qi.append(i)
            kj.append(j)
    return jnp.asarray(np.array(qi, np.int32)), jnp.asarray(np.array(kj, np.int32))


def _flash_heads(q_ref, k_ref, v_ref, m_ref, l_ref, acc_ref, bias, nheads, v_cols):
    tk = k_ref.shape[0]
    ones = jnp.ones((tk, LANES), MXU_DTYPE)

    def logits(h):
        cs = slice(h * LANES, (h + 1) * LANES)
        return _dot_nt(q_ref[:, cs], k_ref[:, cs])

    ahead = [logits(h) for h in range(min(LOGITS_AHEAD, nheads))]
    for h in range(nheads):
        cs = slice(h * LANES, (h + 1) * LANES)
        s = ahead.pop(0)
        if h + LOGITS_AHEAD < nheads:
            ahead.append(logits(h + LOGITS_AHEAD))
        s = s.astype(SOFTMAX_DTYPE) + bias
        m_prev = m_ref[h]
        m_next = jnp.maximum(m_prev, jnp.max(s, axis=1, keepdims=True).astype(F32))
        alpha = jnp.exp2(m_prev - m_next)
        p = jnp.exp2(s - m_next[:, 0:1].astype(SOFTMAX_DTYPE)).astype(MXU_DTYPE)
        v_ext = jnp.concatenate([v_ref[:, v_cols(h)].astype(MXU_DTYPE), ones], axis=1)
        pv = jnp.dot(p, v_ext, preferred_element_type=F32)
        l_ref[h] = alpha * l_ref[h] + pv[:, LANES:]
        m_ref[h] = m_next
        acc_ref[:, cs] = alpha * acc_ref[:, cs] + pv[:, :LANES]


def _flash_finish(o_ref, l_ref, acc_ref, nheads):
    for h in range(nheads):
        cs = slice(h * LANES, (h + 1) * LANES)
        o_ref[:, cs] = (acc_ref[:, cs] / l_ref[h]).astype(o_ref.dtype)


def _mlp_kernel(h_ref, g_ref, w1_ref, w2_ref, gf_ref, o_ref, xn_ref, acc_ref, *, final_norm):
    f = pl.program_id(1)

    @pl.when(f == 0)
    def _():
        xn_ref[...] = _rmsnorm_rows(h_ref[...], g_ref[...]).astype(xn_ref.dtype)
        acc_ref[...] = jnp.zeros_like(acc_ref)

    a = _dot(xn_ref[...], w1_ref[...])
    a = jnp.square(jnp.maximum(a, 0.0))
    acc_ref[...] += _dot(a, w2_ref[...])

    @pl.when(f == pl.num_programs(1) - 1)
    def _():
        y = h_ref[...] + acc_ref[...]
        if final_norm:
            y = _rmsnorm_rows(y, gf_ref[...])
        o_ref[...] = y


def _mlp(h, g, w1, w2, gf, final_norm, tm=1024, tf=2048):
    L, D = h.shape
    dff = w1.shape[1]
    return pl.pallas_call(
        functools.partial(_mlp_kernel, final_norm=final_norm),
        grid=(L // tm, dff // tf),
        in_specs=[
            pl.BlockSpec((tm, D), lambda i, f: (i, 0)),
            _const_spec((1, D)),
            pl.BlockSpec((D, tf), lambda i, f: (0, f)),
            pl.BlockSpec((tf, D), lambda i, f: (f, 0)),
            _const_spec((1, D)),
        ],
        out_specs=pl.BlockSpec((tm, D), lambda i, f: (i, 0)),
        out_shape=jax.ShapeDtypeStruct((L, D), F32),
        scratch_shapes=[pltpu.VMEM((tm, D), MXU_DTYPE), pltpu.VMEM((tm, D), F32)],
        compiler_params=_params("arbitrary", "arbitrary"),
        name="mlp",
    )(h, g.reshape(1, D), w1, w2, gf.reshape(1, D))


def _out_proj_kernel(a_ref, w_ref, b_ref, h_ref, o_ref):
    o_ref[...] = h_ref[...] + _dot(a_ref[...], w_ref[...]) + b_ref[...]


def _out_proj(a, w, b, h, tm=512):
    L, K = a.shape
    D = w.shape[1]
    return pl.pallas_call(
        _out_proj_kernel,
        grid=(L // tm,),
        in_specs=[
            pl.BlockSpec((tm, K), lambda i: (i, 0)),
            _const_spec((K, D)),
            _const_spec((1, D)),
            pl.BlockSpec((tm, D), lambda i: (i, 0)),
        ],
        out_specs=pl.BlockSpec((tm, D), lambda i: (i, 0)),
        out_shape=jax.ShapeDtypeStruct((L, D), F32),
        compiler_params=_params("arbitrary"),
        name="out_proj",
    )(a, w, b.reshape(1, D), h)


def _norm_proj_kernel(h_ref, g_ref, w_ref, o_ref, xn_ref):
    @pl.when(pl.program_id(1) == 0)
    def _():
        xn_ref[...] = _rmsnorm_rows(h_ref[...], g_ref[...]).astype(xn_ref.dtype)

    o_ref[...] = jnp.dot(xn_ref[...], w_ref[...], preferred_element_type=F32).astype(o_ref.dtype)


def _norm_proj(h, g, w, tn, out_dtype=F32, tm=512):
    L, D = h.shape
    N = w.shape[1]
    return pl.pallas_call(
        _norm_proj_kernel,
        grid=(L // tm, N // tn),
        in_specs=[
            pl.BlockSpec((tm, D), lambda i, j: (i, 0)),
            _const_spec((1, D)),
            pl.BlockSpec((D, tn), lambda i, j: (0, j)),
        ],
        out_specs=pl.BlockSpec((tm, tn), lambda i, j: (i, j)),
        out_shape=jax.ShapeDtypeStruct((L, N), out_dtype),
        scratch_shapes=[pltpu.VMEM((tm, D), MXU_DTYPE)],
        compiler_params=_params("arbitrary", "arbitrary"),
        name="norm_proj",
    )(h, g.reshape(1, D), w)


def _dsa_proj_kernel(h_ref, g_ref, w_ref, lg_ref, lb_ref,
                     q_ref, k_ref, v_ref, qi_ref, kia_ref, kib_ref, wi_ref, *, d, scale, wi_scale):
    xn = _rmsnorm_rows(h_ref[...], g_ref[...]).astype(MXU_DTYPE)

    def mm(c0, n):
        return jnp.dot(xn, w_ref[:, c0:c0 + n], preferred_element_type=F32)

    cw = 512
    for c in range(0, d, cw):
        q_ref[:, c:c + cw] = (mm(c, cw) * scale).astype(q_ref.dtype)
        k_ref[:, c:c + cw] = mm(d + c, cw).astype(k_ref.dtype)
        v_ref[:, c:c + cw] = mm(2 * d + c, cw).astype(v_ref.dtype)
    nqi = IDX_HEADS * IDX_DIM
    qi_ref[...] = mm(3 * d, nqi).astype(qi_ref.dtype)
    xa = mm(3 * d + nqi, LANES)
    xb = mm(3 * d + nqi + LANES, LANES)
    lane = lax.broadcasted_iota(jnp.int32, xa.shape, 1)

    def masked_ln(x, m, gain, bias):
        mu = jnp.sum(jnp.where(m, x, 0.0), axis=-1, keepdims=True) * (1.0 / IDX_DIM)
        dlt = jnp.where(m, x - mu, 0.0)
        var = jnp.sum(dlt * dlt, axis=-1, keepdims=True) * (1.0 / IDX_DIM)
        return dlt * lax.rsqrt(var + EPS) * gain + bias

    kia_ref[...] = masked_ln(xa, lane < IDX_DIM, lg_ref[0:1, :], lb_ref[0:1, :]).astype(kia_ref.dtype)
    kib_ref[...] = masked_ln(xb, lane >= IDX_DIM, lg_ref[1:2, :], lb_ref[1:2, :]).astype(kib_ref.dtype)
    wi_ref[...] = jnp.where(lane < IDX_HEADS, xb, 0.0) * wi_scale


def _dsa_proj(h, g, w_cat, lg2, lb2, tm=256):
    L, D = h.shape
    N = w_cat.shape[1]
    nqi = IDX_HEADS * IDX_DIM
    row = lambda n: pl.BlockSpec((tm, n), lambda i: (i, 0))
    return pl.pallas_call(
        functools.partial(_dsa_proj_kernel, d=D, scale=(D // DSA_HEADS) ** -0.5 * LOG2E,
                          wi_scale=IDX_HEADS ** -0.5 * IDX_DIM ** -0.5),
        grid=(L // tm,),
        in_specs=[row(D), _const_spec((1, D)), _const_spec((D, N)),
                  _const_spec((2, LANES)), _const_spec((2, LANES))],
        out_specs=[row(D), row(D), row(D), row(nqi), row(LANES), row(LANES), row(LANES)],
        out_shape=[jax.ShapeDtypeStruct((L, D), MXU_DTYPE)] * 3
        + [jax.ShapeDtypeStruct((L, nqi), MXU_DTYPE)]
        + [jax.ShapeDtypeStruct((L, LANES), MXU_DTYPE)] * 2
        + [jax.ShapeDtypeStruct((L, LANES), F32)],
        compiler_params=_params("arbitrary"),
        name="dsa_proj",
    )(h, g.reshape(1, D), w_cat, lg2, lb2)


I16 = jnp.int16
I16_MIN = -(2 ** 15)
TIE_ROWS = 16


def _dsa_select_kernel(qi_ref, wi_ref, kia_ref, kib_ref, bias_ref, hi_ref, lo_ref, cand_ref,
                       thr_hi_ref, thr_lo_ref, surplus_ref,
                       *, tq, tk, topk, rb):
    i = pl.program_id(0)
    n_tiles = hi_ref.shape[0]
    n_kt = ((i + 1) * tq + tk - 1) // tk
    q0 = i * tq
    qpos = q0 + lax.broadcasted_iota(jnp.int32, (tq, tk), 0)
    lane_pos = lax.broadcasted_iota(jnp.int32, (tq, tk), 1)

    def score_tile(c, carry):
        k0 = pl.multiple_of(c * tk, tk)
        ka = kia_ref[pl.ds(k0, tk), :]
        kb = kib_ref[pl.ds(k0, tk), :]
        acc = jnp.zeros((tq, tk), F32)
        for p in range(IDX_HEADS // 2):
            lhs = qi_ref[:, p * LANES:(p + 1) * LANES]
            sa = jnp.maximum(_dot_nt(lhs, ka), 0.0)
            sb = jnp.maximum(_dot_nt(lhs, kb), 0.0)
            acc = acc + wi_ref[:, 2 * p:2 * p + 1] * sa
            acc = acc + wi_ref[:, 2 * p + 1:2 * p + 2] * sb
        bits = pltpu.bitcast(acc, jnp.int32)
        key = bits ^ ((bits >> 31) & jnp.int32(0x7FFFFFFF))
        key = jnp.where(k0 + lane_pos <= qpos, key, jnp.int32(INT_MIN))
        hi_ref[c] = (key >> 16).astype(I16)
        lo_ref[c] = ((key & 0xFFFF) + I16_MIN).astype(I16)
        return carry

    lax.fori_loop(0, n_kt, score_tile, 0)

    def count(pred):
        def body(c, acc):
            parts = []
            for r0 in range(0, tq, rb):
                cand = cand_ref[r0:r0 + rb, :]
                part = jnp.zeros((rb, LANES), I16)
                for t in range(tk // LANES):
                    ls = slice(t * LANES, (t + 1) * LANES)
                    hit = pred(hi_ref[c, r0:r0 + rb, ls], lo_ref[c, r0:r0 + rb, ls], cand,
                               c * tk + t * LANES, r0)
                    part = part + hit.astype(I16)
                parts.append(part)
            return acc + jnp.concatenate(parts, axis=0)
        acc = lax.fori_loop(0, n_kt, body, jnp.zeros((tq, LANES), I16))
        return jnp.sum(acc.astype(F32), axis=1, keepdims=True)

    def set_cand(x):
        cand_ref[...] = jnp.broadcast_to(x, (tq, LANES)).astype(I16)

    def bisect16(pred, n_ge0):
        def bit_step(carry):
            b, c_best, n_best = carry
            cand = c_best + lax.shift_left(jnp.int32(1), 15 - b)
            set_cand(cand)
            cnt = count(pred)
            keep = cnt >= topk
            return b + 1, jnp.where(keep, cand, c_best), jnp.where(keep, cnt, n_best)

        def unresolved(carry):
            b, _, n_best = carry
            return jnp.logical_and(b < 16, jnp.max(jnp.abs(n_best - topk)) > 0.0)

        _, c_best, n_best = lax.while_loop(
            unresolved, bit_step, (jnp.int32(0), jnp.full((tq, 1), I16_MIN, jnp.int32), n_ge0))
        return c_best, n_best

    n_all = jnp.broadcast_to((n_kt * tk).astype(F32), (tq, 1))
    thr_hi, n_hi = bisect16(lambda hi, lo, cd, k0, r0: hi >= cd, n_all)
    thr_hi16 = jnp.broadcast_to(thr_hi, (tq, LANES)).astype(I16)

    def clamp_low(c, carry):
        for r0 in range(0, tq, rb):
            hi = hi_ref[c, r0:r0 + rb, :]
            th = jnp.concatenate([thr_hi16[r0:r0 + rb]] * (tk // LANES), axis=1)
            side = jnp.where(hi > th, jnp.asarray(2 ** 15 - 1, I16), jnp.asarray(I16_MIN, I16))
            lo_ref[c, r0:r0 + rb, :] = jnp.where(hi == th, lo_ref[c, r0:r0 + rb, :], side)
        return carry

    lax.fori_loop(0, n_kt, clamp_low, 0)
    thr_lo, n_ge = bisect16(lambda hi, lo, cd, k0, r0: lo >= cd, n_hi)
    thr_lo16 = jnp.broadcast_to(thr_lo, (tq, LANES)).astype(I16)
    is_floor = jnp.logical_and(thr_hi == I16_MIN, thr_lo == I16_MIN)

    cand_ref[...] = jnp.broadcast_to(jnp.where(is_floor, -1, 2 ** 15 - 1), (tq, LANES)).astype(I16)
    thr_hi_ref[...] = thr_hi16
    thr_lo_ref[...] = thr_lo16
    surplus_ref[...] = jnp.broadcast_to(jnp.where(is_floor, 0.0, n_ge - topk), (tq, LANES))
    lane_idx16 = lax.broadcasted_iota(jnp.int32, (TIE_ROWS, LANES), 1)

    def tie_group(gi, carry):
        r0 = pl.multiple_of(gi * TIE_ROWS, TIE_ROWS)
        rows = pl.ds(r0, TIE_ROWS)

        @pl.when(jnp.max(surplus_ref[rows, :]) > 0.0)
        def _ties():
            th = thr_hi_ref[rows, :]
            tl = thr_lo_ref[rows, :]

            def count_rows(pred):
                def body(c, acc):
                    part = jnp.zeros((TIE_ROWS, LANES), I16)
                    for t in range(tk // LANES):
                        ls = slice(t * LANES, (t + 1) * LANES)
                        hit = pred(hi_ref[c, rows, ls], lo_ref[c, rows, ls], c * tk + t * LANES)
                        part = part + hit.astype(I16)
                    return acc + part
                acc = lax.fori_loop(0, n_kt, body, jnp.zeros((TIE_ROWS, LANES), I16))
                return jnp.sum(acc.astype(F32), axis=1, keepdims=True)

            need = topk - count_rows(lambda hi, lo, k0: jnp.logical_or(
                hi > th, jnp.logical_and(hi == th, lo > tl)))

            def idx_step(b, cut):
                cand = cut + lax.shift_left(jnp.int32(1), 14 - b)
                cd = jnp.broadcast_to(cand, (TIE_ROWS, LANES)).astype(I16)
                cnt = count_rows(lambda hi, lo, k0: jnp.logical_and(
                    jnp.logical_and(hi == th, lo == tl), (k0 + lane_idx16).astype(I16) < cd))
                return jnp.where(cnt < need, cand, cut)

            cut = lax.fori_loop(0, 15, idx_step, jnp.zeros((TIE_ROWS, 1), jnp.int32))
            floor = jnp.logical_and(th.astype(jnp.int32) == I16_MIN, tl.astype(jnp.int32) == I16_MIN)
            cand_ref[rows, :] = jnp.where(floor, -1, jnp.broadcast_to(cut, (TIE_ROWS, LANES))).astype(I16)

        return carry

    lax.fori_loop(0, tq // TIE_ROWS, tie_group, 0)

    lane_idx = lax.broadcasted_iota(jnp.int32, (rb, tk), 1)

    def write_tile(c, carry):
        for r0 in range(0, tq, rb):
            hi = hi_ref[c, r0:r0 + rb, :]
            lo = lo_ref[c, r0:r0 + rb, :]
            th = jnp.concatenate([thr_hi16[r0:r0 + rb]] * (tk // LANES), axis=1)
            tl = jnp.concatenate([thr_lo16[r0:r0 + rb]] * (tk // LANES), axis=1)
            cut = jnp.concatenate([cand_ref[r0:r0 + rb, :]] * (tk // LANES), axis=1)
            gt = jnp.logical_or(hi > th, jnp.logical_and(hi == th, lo > tl))
            eq = jnp.logical_and(jnp.logical_and(hi == th, lo == tl),
                                 (c * tk + lane_idx).astype(I16) <= cut)
            keep = jnp.logical_or(gt, eq)
            bias_ref[c, r0:r0 + rb, :] = jnp.where(
                keep, jnp.zeros((), SOFTMAX_DTYPE), jnp.asarray(NEG_BIG, SOFTMAX_DTYPE))
        return carry

    lax.fori_loop(0, n_kt, write_tile, 0)

    def fill_tile(c, carry):
        bias_ref[c] = jnp.full((tq, tk), NEG_BIG, SOFTMAX_DTYPE)
        return carry

    lax.fori_loop(n_kt, n_tiles, fill_tile, 0)


def _dsa_select(qi, kia, kib, wi, topk, tq=256, tk=512, rb=64):
    L, nqi = qi.shape
    assert L // LANES < 2 ** 15 and L < 2 ** 15
    n_tiles = L // tk
    return pl.pallas_call(
        functools.partial(_dsa_select_kernel, tq=tq, tk=tk, topk=topk, rb=rb),
        grid=(L // tq,),
        in_specs=[
            pl.BlockSpec((tq, nqi), lambda i: (i, 0)),
            pl.BlockSpec((tq, LANES), lambda i: (i, 0)),
            _const_spec((L, LANES)),
            _const_spec((L, LANES)),
        ],
        out_specs=pl.BlockSpec((n_tiles, tq, tk), lambda i: (0, i, 0)),
        out_shape=jax.ShapeDtypeStruct((n_tiles, L, tk), SOFTMAX_DTYPE),
        scratch_shapes=[
            pltpu.VMEM((n_tiles, tq, tk), I16),
            pltpu.VMEM((n_tiles, tq, tk), I16),
            pltpu.VMEM((tq, LANES), I16),
            pltpu.VMEM((tq, LANES), I16),
            pltpu.VMEM((tq, LANES), I16),
            pltpu.VMEM((tq, LANES), F32),
        ],
        compiler_params=_params("arbitrary"),
        name="dsa_select",
    )(qi, wi, kia, kib)


def _dsa_attn_kernel(qi_s, kj_s, q_ref, k_ref, v_ref, bias_ref, o_ref, m_ref, l_ref, acc_ref,
                     *, tq, tk, nheads):
    step = pl.program_id(0)
    i = qi_s[step]
    j = kj_s[step]
    n_kt = ((i + 1) * tq + tk - 1) // tk

    @pl.when(j == 0)
    def _init():
        m_ref[...] = jnp.full_like(m_ref, NEG_BIG)
        l_ref[...] = jnp.zeros_like(l_ref)
        acc_ref[...] = jnp.zeros_like(acc_ref)

    bias = jnp.concatenate([bias_ref[t] for t in range(bias_ref.shape[0])], axis=1)
    _flash_heads(q_ref, k_ref, v_ref, m_ref, l_ref, acc_ref, bias, nheads,
                 lambda h: slice(h * LANES, (h + 1) * LANES))

    @pl.when(j == n_kt - 1)
    def _finish():
        _flash_finish(o_ref, l_ref, acc_ref, nheads)


def _dsa_attn(q, k, v, bias, tq=512, tk=1024):
    L, D = q.shape
    nheads = DSA_HEADS
    tb = bias.shape[2]
    assert D == nheads * LANES and bias.shape == (L // tb, L, tb) and tk % tb == 0
    qi_s, kj_s = _causal_steps(L // tq, tq, tk)
    qmap = lambda s, qs, ks: (qs[s], 0)
    kmap = lambda s, qs, ks: (ks[s], 0)
    grid_spec = pltpu.PrefetchScalarGridSpec(
        num_scalar_prefetch=2,
        grid=(qi_s.shape[0],),
        in_specs=[
            pl.BlockSpec((tq, D), qmap),
            pl.BlockSpec((tk, D), kmap),
            pl.BlockSpec((tk, D), kmap),
            pl.BlockSpec((tk // tb, tq, tb), lambda s, qs, ks: (ks[s], qs[s], 0)),
        ],
        out_specs=pl.BlockSpec((tq, D), qmap),
        scratch_shapes=[
            pltpu.VMEM((nheads, tq, LANES), F32),
            pltpu.VMEM((nheads, tq, LANES), F32),
            pltpu.VMEM((tq, D), F32),
        ],
    )
    return pl.pallas_call(
        functools.partial(_dsa_attn_kernel, tq=tq, tk=tk, nheads=nheads),
        grid_spec=grid_spec,
        out_shape=jax.ShapeDtypeStruct((L, D), MXU_DTYPE),
        compiler_params=_params("arbitrary"),
        name="dsa_attn",
    )(qi_s, kj_s, q, k, v, bias)


def _dsa_layer(h, g, w_in, idx_k_g, idx_k_b, w_out):
    L, D = h.shape
    nqi = IDX_HEADS * IDX_DIM
    o3 = 3 * D
    o4 = o3 + nqi
    o5 = o4 + IDX_DIM
    w_ki = w_in[:, o4:o5]
    w_wi = w_in[:, o5:]
    zpad = lambda n: jnp.zeros((D, n), w_in.dtype)
    w_cat = jnp.concatenate(
        [w_in[:, :o4],
         w_ki, w_wi, zpad(LANES - IDX_DIM - IDX_HEADS),
         w_wi, zpad(LANES - IDX_DIM - IDX_HEADS), w_ki], axis=1).astype(MXU_DTYPE)
    z = jnp.zeros((IDX_DIM,), F32)
    lg2 = jnp.stack([jnp.concatenate([idx_k_g, z]), jnp.concatenate([z, idx_k_g])])
    lb2 = jnp.stack([jnp.concatenate([idx_k_b, z]), jnp.concatenate([z, idx_k_b])])
    q, k, v, qi, kia, kib, wi = _dsa_proj(h, g, w_cat, lg2, lb2)
    bias = _dsa_select(qi, kia, kib, wi, topk=min(TOPK_MAX, L // 4))
    o = _dsa_attn(q, k, v, bias)
    return _out_proj(o, w_out.astype(MXU_DTYPE), jnp.zeros((D,), F32), h)


def _glu_proj_kernel(h_ref, g_ref, w_ref, b_ref, u_ref, *, d):
    xn = _rmsnorm_rows(h_ref[...], g_ref[...]).astype(MXU_DTYPE)
    cw = 512
    for c in range(0, d, cw):
        a = jnp.dot(xn, w_ref[:, c:c + cw], preferred_element_type=F32) + b_ref[:, c:c + cw]
        gt = jnp.dot(xn, w_ref[:, d + c:d + c + cw], preferred_element_type=F32) + b_ref[:, d + c:d + c + cw]
        u_ref[:, c:c + cw] = a * _sigmoid(gt)


def _glu_proj(h, g, w, b, tm=512):
    L, D = h.shape
    return pl.pallas_call(
        functools.partial(_glu_proj_kernel, d=D),
        grid=(L // tm,),
        in_specs=[pl.BlockSpec((tm, D), lambda i: (i, 0)), _const_spec((1, D)),
                  _const_spec((D, 2 * D)), _const_spec((1, 2 * D))],
        out_specs=pl.BlockSpec((tm, D), lambda i: (i, 0)),
        out_shape=jax.ShapeDtypeStruct((L, D), F32),
        compiler_params=_params("arbitrary"),
        name="conv_glu_proj",
    )(h, g.reshape(1, D), w, b.reshape(1, 2 * D))


CONV_HALO = 32


def _conv_out_kernel(u_ref, up_ref, wdw_ref, bdw_ref, lg_ref, lb_ref, w2_ref, b2_ref, h_ref, o_ref,
                     x_ref, xs_ref, *, tm):
    i = pl.program_id(0)
    x_ref[CONV_HALO:, :] = u_ref[...]
    x_ref[:CONV_HALO, :] = jnp.where(i > 0, up_ref[...], 0.0)
    off = CONV_HALO - (CONV_WIDTH - 1)
    acc = jnp.zeros(u_ref.shape, F32)
    for r in range(SUBLANES):
        taps = [t for t in range(CONV_WIDTH) if (off + t) % SUBLANES == r]
        if not taps:
            continue
        span = off + taps[-1] - r + tm
        src = x_ref
        if r:
            xs_ref[:span, :] = x_ref[r:r + span, :]
            src = xs_ref
        for t in taps:
            a = off + t - r
            acc = acc + wdw_ref[t:t + 1, :] * src[a:a + tm, :]
    y = acc + bdw_ref[...]
    mu = jnp.mean(y, axis=-1, keepdims=True)
    dlt = y - mu
    var = jnp.mean(dlt * dlt, axis=-1, keepdims=True)
    y = _silu(dlt * lax.rsqrt(var + EPS) * lg_ref[...] + lb_ref[...])
    o_ref[...] = h_ref[...] + _dot(y, w2_ref[...]) + b2_ref[...]


def _conv_out(u, w_dw, b_dw, ln_g, ln_b, w2, b2, h, tm=512):
    L, D = u.shape
    r = tm // CONV_HALO
    vec = lambda a: a.reshape(1, D)
    return pl.pallas_call(
        functools.partial(_conv_out_kernel, tm=tm),
        grid=(L // tm,),
        in_specs=[
            pl.BlockSpec((tm, D), lambda i: (i, 0)),
            pl.BlockSpec((CONV_HALO, D), lambda i: (jnp.maximum(i * r - 1, 0), 0)),
            _const_spec((CONV_WIDTH, D)),
            _const_spec((1, D)), _const_spec((1, D)), _const_spec((1, D)),
            _const_spec((D, D)), _const_spec((1, D)),
            pl.BlockSpec((tm, D), lambda i: (i, 0)),
        ],
        out_specs=pl.BlockSpec((tm, D), lambda i: (i, 0)),
        out_shape=jax.ShapeDtypeStruct((L, D), F32),
        scratch_shapes=[pltpu.VMEM((tm + CONV_HALO, D), F32), pltpu.VMEM((tm + CONV_HALO, D), F32)],
        compiler_params=_params("arbitrary"),
        name="conv_out",
    )(u, u, w_dw, vec(b_dw), vec(ln_g), vec(ln_b), w2, vec(b2), h)


def _conv_layer(h, g, w_pw1, b_pw1, w_dw, b_dw, ln_g, ln_b, w_pw2, b_pw2):
    u = _glu_proj(h, g, w_pw1.astype(MXU_DTYPE), b_pw1)
    return _conv_out(u, w_dw, b_dw, ln_g, ln_b, w_pw2.astype(MXU_DTYPE), b_pw2, h)


def _mla_prep_kernel(h_ref, g_ref, w1_ref, qg_ref, kvg_ref, wqa_ref, wqb_ref, wka_ref, wv_ref,
                     c_ref, s_ref, q_ref, k_ref, v_ref, *, scale):
    xn = _rmsnorm_rows(h_ref[...], g_ref[...]).astype(MXU_DTYPE)
    proj = jnp.dot(xn, w1_ref[...], preferred_element_type=F32)
    o1 = MLA_Q_LORA
    o2 = o1 + MLA_KV_LORA
    cq = _rmsnorm_rows(proj[:, :o1], qg_ref[...]).astype(MXU_DTYPE)
    ckv = _rmsnorm_rows(proj[:, o1:o2], kvg_ref[...]).astype(MXU_DTYPE)
    cos = c_ref[...]
    sin = s_ref[...]
    kr = proj[:, o2:o2 + LANES] * cos + proj[:, o2 + LANES:o2 + 2 * LANES] * sin
    for hd in range(MLA_HEADS):
        cs = slice(hd * LANES, (hd + 1) * LANES)
        qa = jnp.dot(cq, wqa_ref[:, cs], preferred_element_type=F32)
        qb = jnp.dot(cq, wqb_ref[:, cs], preferred_element_type=F32)
        q_ref[:, cs] = ((qa * cos + qb * sin) * scale).astype(q_ref.dtype)
        ka = jnp.dot(ckv, wka_ref[:, cs], preferred_element_type=F32)
        k_ref[:, cs] = (ka + kr).astype(k_ref.dtype)
    v_ref[...] = jnp.dot(ckv, wv_ref[...], preferred_element_type=F32).astype(v_ref.dtype)


def _mla_prep(h, g, w1, qg, kvg, wqa, wqb, wka, wv, ctab, stab, tm=256):
    L, D = h.shape
    HP = MLA_HEADS * LANES
    NV = MLA_HEADS * MLA_V
    row = lambda n: pl.BlockSpec((tm, n), lambda i: (i, 0))
    full = lambda a: _const_spec(a.shape)
    return pl.pallas_call(
        functools.partial(_mla_prep_kernel, scale=(MLA_NOPE + MLA_ROPE) ** -0.5 * LOG2E),
        grid=(L // tm,),
        in_specs=[row(D), _const_spec((1, D)), full(w1), _const_spec((1, MLA_Q_LORA)),
                  _const_spec((1, MLA_KV_LORA)), full(wqa), full(wqb), full(wka), full(wv),
                  row(LANES), row(LANES)],
        out_specs=[row(HP), row(HP), row(NV)],
        out_shape=[jax.ShapeDtypeStruct((L, HP), MXU_DTYPE), jax.ShapeDtypeStruct((L, HP), MXU_DTYPE),
                   jax.ShapeDtypeStruct((L, NV), MXU_DTYPE)],
        compiler_params=_params("arbitrary"),
        name="mla_prep",
    )(h, g.reshape(1, D), w1, qg.reshape(1, -1), kvg.reshape(1, -1), wqa, wqb, wka, wv, ctab, stab)


def _mla_attn_kernel(qi_s, kj_s, q_ref, k_ref, v_ref, o_ref, m_ref, l_ref, acc_ref, *, tq, tk, nheads):
    step = pl.program_id(0)
    i = qi_s[step]
    j = kj_s[step]
    n_kt = ((i + 1) * tq + tk - 1) // tk

    @pl.when(j == 0)
    def _init():
        m_ref[...] = jnp.full_like(m_ref, NEG_BIG)
        l_ref[...] = jnp.zeros_like(l_ref)
        acc_ref[...] = jnp.zeros_like(acc_ref)

    qpos = i * tq + lax.broadcasted_iota(jnp.int32, (tq, tk), 0)
    kpos = j * tk + lax.broadcasted_iota(jnp.int32, (tq, tk), 1)
    bias = jnp.where(kpos <= qpos, 0.0, NEG_BIG).astype(SOFTMAX_DTYPE)
    _flash_heads(q_ref, k_ref, v_ref, m_ref, l_ref, acc_ref, bias, nheads,
                 lambda h: slice((h // 2) * LANES, (h // 2 + 1) * LANES))

    @pl.when(j == n_kt - 1)
    def _finish():
        _flash_finish(o_ref, l_ref, acc_ref, nheads)


def _mla_attn(q, k, v, tq=512, tk=1024):
    L, HP = q.shape
    NV = v.shape[1]
    qi_s, kj_s = _causal_steps(L // tq, tq, tk)
    qmap = lambda s, qs, ks: (qs[s], 0)
    kmap = lambda s, qs, ks: (ks[s], 0)
    grid_spec = pltpu.PrefetchScalarGridSpec(
        num_scalar_prefetch=2,
        grid=(qi_s.shape[0],),
        in_specs=[pl.BlockSpec((tq, HP), qmap),
                  pl.BlockSpec((tk, HP), kmap), pl.BlockSpec((tk, NV), kmap)],
        out_specs=pl.BlockSpec((tq, HP), qmap),
        scratch_shapes=[pltpu.VMEM((MLA_HEADS, tq, LANES), F32),
                        pltpu.VMEM((MLA_HEADS, tq, LANES), F32),
                        pltpu.VMEM((tq, HP), F32)],
    )
    return pl.pallas_call(
        functools.partial(_mla_attn_kernel, tq=tq, tk=tk, nheads=MLA_HEADS),
        grid_spec=grid_spec,
        out_shape=jax.ShapeDtypeStruct((L, HP), MXU_DTYPE),
        compiler_params=_params("arbitrary"),
        name="mla_attn",
    )(qi_s, kj_s, q, k, v)


def _place_heads(w, nheads, src_w, lane0, swap_half=0):
    K = w.shape[0]
    w3 = w.reshape(K, nheads, src_w)
    if swap_half:
        w3 = jnp.concatenate([w3[..., swap_half:], w3[..., :swap_half]], axis=-1)
    out = jnp.zeros((K, nheads, LANES), w.dtype)
    out = out.at[:, :, lane0:lane0 + src_w].set(w3)
    return out.reshape(K, nheads * LANES)


def _mla_layer(h, g, w_in, q_norm_g, w_uq, kv_norm_g, w_ukv, w_out):
    L, D = h.shape
    half = MLA_ROPE // 2
    o2 = MLA_Q_LORA + MLA_KV_LORA
    w_kr = w_in[:, o2:]
    w1 = jnp.concatenate(
        [w_in[:, :o2], _place_heads(w_kr, 1, MLA_ROPE, MLA_NOPE),
         _place_heads(w_kr, 1, MLA_ROPE, MLA_NOPE, swap_half=half)], axis=1).astype(MXU_DTYPE)
    uq = w_uq.reshape(MLA_Q_LORA, MLA_HEADS, MLA_NOPE + MLA_ROPE)
    uq_rope = uq[..., MLA_NOPE:].reshape(MLA_Q_LORA, MLA_HEADS * MLA_ROPE)
    wqa = _place_heads(w_uq, MLA_HEADS, MLA_NOPE + MLA_ROPE, 0).astype(MXU_DTYPE)
    wqb = _place_heads(uq_rope, MLA_HEADS, MLA_ROPE, MLA_NOPE, swap_half=half).astype(MXU_DTYPE)
    ukv = w_ukv.reshape(MLA_KV_LORA, MLA_HEADS, MLA_NOPE + MLA_V)
    wka = _place_heads(ukv[..., :MLA_NOPE].reshape(MLA_KV_LORA, -1), MLA_HEADS, MLA_NOPE, 0).astype(MXU_DTYPE)
    wv = ukv[..., MLA_NOPE:].reshape(MLA_KV_LORA, MLA_HEADS * MLA_V).astype(MXU_DTYPE)
    pos = jnp.arange(L, dtype=F32)
    inv_freq = ROPE_THETA ** (-jnp.arange(0, MLA_ROPE, 2, dtype=F32) / MLA_ROPE)
    ang = pos[:, None] * inv_freq[None, :]
    cos, sin = jnp.cos(ang), jnp.sin(ang)
    zeros = jnp.zeros((L, LANES - MLA_NOPE - MLA_ROPE), F32)
    ctab = jnp.concatenate([jnp.ones((L, MLA_NOPE), F32), cos, cos, zeros], axis=1)
    stab = jnp.concatenate([jnp.zeros((L, MLA_NOPE), F32), -sin, sin, zeros], axis=1)
    q, k, v = _mla_prep(h, g, w1, q_norm_g, kv_norm_g, wqa, wqb, wka, wv, ctab, stab)
    o = _mla_attn(q, k, v)
    wo = w_out.reshape(MLA_HEADS // 2, 2, MLA_V, D)
    zo = jnp.zeros_like(wo[:, 0])
    wo_pad = jnp.stack([jnp.concatenate([wo[:, 0], zo], axis=1),
                        jnp.concatenate([zo, wo[:, 1]], axis=1)], axis=1)
    wo_pad = wo_pad.reshape(MLA_HEADS * LANES, D).astype(MXU_DTYPE)
    return _out_proj(o, wo_pad, jnp.zeros((D,), F32), h)


def _gdn_scan_kernel(qkv_ref, ba_ref, cw_ref, alog_ref, dtb_ref, o_ref, x_ref, s_ref,
                     *, nheads, dk, dv, c, nch):
    n = pl.program_id(0)
    halo = 8
    nqk = nheads * dk

    @pl.when(n == 0)
    def _init():
        x_ref[:halo, :] = jnp.zeros((halo, x_ref.shape[1]), F32)
        s_ref[...] = jnp.zeros_like(s_ref)

    rows = nch * c
    x_ref[halo:, :] = qkv_ref[...]
    off = halo - (GDN_CONV - 1)
    acc = jnp.zeros(qkv_ref.shape, F32)
    for t in range(GDN_CONV):
        acc = acc + cw_ref[t:t + 1, :] * x_ref[off + t:off + t + rows, :]
    x_ref[:halo, :] = qkv_ref[rows - halo:, :]
    qkv = _silu(acc)

    beta_all = _sigmoid(ba_ref[:, :LANES])
    a_raw = ba_ref[:, LANES:] + dtb_ref[...]
    softplus = jnp.maximum(a_raw, 0.0) + jnp.log(1.0 + jnp.exp(-jnp.abs(a_raw)))
    g_all = -jnp.exp(alog_ref[...]) * softplus
    row = lax.broadcasted_iota(jnp.int32, (c, c), 0)
    col = lax.broadcasted_iota(jnp.int32, (c, c), 1)
    tril = row >= col
    tril_strict = row > col
    eye = jnp.where(row == col, 1.0, 0.0)

    units = [(ch, hd) for ch in range(nch) for hd in range(nheads)]
    gc_all = [jnp.dot(tril.astype(F32), g_all[ch * c:(ch + 1) * c], preferred_element_type=F32,
                      precision=lax.Precision.HIGHEST) for ch in range(nch)]
    gc_t = [x.T for x in gc_all]

    def chunk_rows(x, ch):
        return x[ch * c:(ch + 1) * c]

    q = [chunk_rows(qkv[:, hd * dk:(hd + 1) * dk], ch) for ch, hd in units]
    k = [chunk_rows(qkv[:, nqk + hd * dk:nqk + (hd + 1) * dk], ch) for ch, hd in units]
    v = [chunk_rows(qkv[:, 2 * nqk + hd * dv:2 * nqk + (hd + 1) * dv], ch) for ch, hd in units]
    q = [x * lax.rsqrt(jnp.sum(x * x, axis=-1, keepdims=True) + EPS) * (dk ** -0.5) for x in q]
    k = [x * lax.rsqrt(jnp.sum(x * x, axis=-1, keepdims=True) + EPS) for x in k]
    beta = [chunk_rows(beta_all[:, hd:hd + 1], ch) for ch, hd in units]
    gc = [gc_all[ch][:, hd:hd + 1] for ch, hd in units]
    gc_last = [gc_all[ch][c - 1:c, hd:hd + 1] for ch, hd in units]
    n_u = range(len(units))
    decay = [jnp.where(tril, jnp.exp(jnp.where(tril, gc[u] - gc_t[units[u][0]][units[u][1]:units[u][1] + 1, :],
                                               0.0)), 0.0) for u in n_u]
    kk = [_dot_nt(k[u], k[u]) for u in n_u]
    qk = [_dot_nt(q[u], k[u]) for u in n_u]
    a_mat = [jnp.where(tril_strict, beta[u] * kk[u] * decay[u], 0.0) for u in n_u]
    t_inv = [eye - jnp.where((row >> 1) == (col >> 1), a_mat[u], 0.0) for u in n_u]
    for lvl in range(1, int(math.log2(c))):
        blk_mask = jnp.logical_and((row >> (lvl + 1)) == (col >> (lvl + 1)), (row >> lvl) != (col >> lvl))
        tm = [_dot(t_inv[u], jnp.where(blk_mask, a_mat[u], 0.0)) for u in n_u]
        tmt = [_dot(tm[u], t_inv[u]) for u in n_u]
        t_inv = [t_inv[u] - tmt[u] for u in n_u]
    e_gc = [jnp.exp(gc[u]) for u in n_u]
    uu = [_dot(t_inv[u], v[u] * beta[u]) for u in n_u]
    w = [_dot(t_inv[u], k[u] * (beta[u] * e_gc[u])) for u in n_u]
    q_dec = [q[u] * e_gc[u] for u in n_u]
    k_dec_t = [(k[u] * jnp.exp(gc_last[u] - gc[u])).T for u in n_u]
    qk_dec = [qk[u] * decay[u] for u in n_u]
    g_last = [jnp.exp(gc_last[u]) for u in n_u]

    s = [s_ref[hd] for hd in range(nheads)]
    for ch in range(nch):
        us = [ch * nheads + hd for hd in range(nheads)]
        ws = [_dot(w[u], s[hd]) for hd, u in enumerate(us)]
        qs = [_dot(q_dec[u], s[hd]) for hd, u in enumerate(us)]
        v_new = [uu[u] - ws[hd] for hd, u in enumerate(us)]
        o_intra = [_dot(qk_dec[u], v_new[hd]) for hd, u in enumerate(us)]
        s_upd = [_dot(k_dec_t[u], v_new[hd]) for hd, u in enumerate(us)]
        for hd, u in enumerate(us):
            o_ref[ch * c:(ch + 1) * c, hd * dv:(hd + 1) * dv] = qs[hd] + o_intra[hd]
        s = [s[hd] * g_last[u] + s_upd[hd] for hd, u in enumerate(us)]
    for hd in range(nheads):
        s_ref[hd] = s[hd]


def _gdn_scan(proj, conv_w, a_log_pad, dt_bias_pad, nch=4):
    L = proj.shape[0]
    c = GDN_CHUNK
    rows = nch * c
    nqkv = 2 * GDN_HEADS * GDN_DK + GDN_HEADS * GDN_DV
    nv = GDN_HEADS * GDN_DV
    ba_block = (nqkv + nv) // (2 * LANES)
    return pl.pallas_call(
        functools.partial(_gdn_scan_kernel, nheads=GDN_HEADS, dk=GDN_DK, dv=GDN_DV, c=c, nch=nch),
        grid=(L // rows,),
        in_specs=[pl.BlockSpec((rows, nqkv), lambda n: (n, 0)),
                  pl.BlockSpec((rows, 2 * LANES), lambda n: (n, ba_block)),
                  _const_spec((GDN_CONV, nqkv)), _const_spec((1, LANES)), _const_spec((1, LANES))],
        out_specs=pl.BlockSpec((rows, nv), lambda n: (n, 0)),
        out_shape=jax.ShapeDtypeStruct((L, nv), F32),
        scratch_shapes=[pltpu.VMEM((rows + 8, nqkv), F32),
                        pltpu.VMEM((GDN_HEADS, GDN_DK, GDN_DV), F32)],
        compiler_params=_params("arbitrary"),
        name="gdn_scan",
    )(proj, proj, conv_w, a_log_pad, dt_bias_pad)


def _gdn_out_kernel(o_ref, gate_ref, og_ref, w_ref, h_ref, y_ref, *, nheads, dv):
    parts = []
    for hd in range(nheads):
        cs = slice(hd * dv, (hd + 1) * dv)
        on = _rmsnorm_rows(o_ref[:, cs], og_ref[...])
        parts.append((on * _silu(gate_ref[:, cs])).astype(MXU_DTYPE))
    y_ref[...] = h_ref[...] + jnp.dot(jnp.concatenate(parts, axis=1), w_ref[...],
                                      preferred_element_type=F32)


def _gdn_out(o, proj, o_norm_g, w_out, h, tm=512):
    L, D = h.shape
    nv = GDN_HEADS * GDN_DV
    gate_block = (2 * GDN_HEADS * GDN_DK + nv) // nv
    return pl.pallas_call(
        functools.partial(_gdn_out_kernel, nheads=GDN_HEADS, dv=GDN_DV),
        grid=(L // tm,),
        in_specs=[pl.BlockSpec((tm, nv), lambda i: (i, 0)),
                  pl.BlockSpec((tm, nv), lambda i: (i, gate_block)),
                  _const_spec((1, GDN_DV)), _const_spec((nv, D)),
                  pl.BlockSpec((tm, D), lambda i: (i, 0))],
        out_specs=pl.BlockSpec((tm, D), lambda i: (i, 0)),
        out_shape=jax.ShapeDtypeStruct((L, D), F32),
        compiler_params=_params("arbitrary"),
        name="gdn_out",
    )(o, proj, o_norm_g.reshape(1, GDN_DV), w_out, h)


def _gdn_layer(h, g, w_in, conv_w, a_log, dt_bias, o_norm_g, w_out):
    L, D = h.shape
    nmain = 2 * GDN_HEADS * GDN_DK + 2 * GDN_HEADS * GDN_DV
    zpad = jnp.zeros((D, LANES - GDN_HEADS), w_in.dtype)
    w_cat = jnp.concatenate([w_in[:, :nmain], w_in[:, nmain:nmain + GDN_HEADS], zpad,
                             w_in[:, nmain + GDN_HEADS:], zpad], axis=1).astype(MXU_DTYPE)
    proj = _norm_proj(h, g, w_cat, tn=w_cat.shape[1] // 2)
    vpad = lambda a: jnp.concatenate([a.astype(F32), jnp.zeros((LANES - GDN_HEADS,), F32)]).reshape(1, LANES)
    o = _gdn_scan(proj, conv_w, vpad(a_log), vpad(dt_bias))
    return _gdn_out(o, proj, o_norm_g, w_out.astype(MXU_DTYPE), h)


def kernel(x, norm_mix_g, norm_mlp_g, final_g, mlp_w1, mlp_w2, dsa_w_in, dsa_idx_k_g, dsa_idx_k_b, dsa_w_out, conv_w_pw1, conv_b_pw1, conv_w_dw, conv_b_dw, conv_ln_g, conv_ln_b, conv_w_pw2, conv_b_pw2, mla_w_in, mla_q_norm_g, mla_w_uq, mla_kv_norm_g, mla_w_ukv, mla_w_out, gdn_w_in, gdn_conv_w, gdn_a_log, gdn_dt_bias, gdn_o_norm_g, gdn_w_out):
    b, L, D = x.shape
    depth = norm_mix_g.shape[0]
    outs = []
    rows = x.reshape(b * L, D)
    for bi in range(b):
        h = rows if b == 1 else rows[bi * L:(bi + 1) * L]
        for i in range(depth):
            m = i % 4
            jl = i // 4
            g = norm_mix_g[i]
            if m == 0:
                h = _dsa_layer(h, g, dsa_w_in[jl], dsa_idx_k_g[jl], dsa_idx_k_b[jl], dsa_w_out[jl])
            elif m == 1:
                h = _conv_layer(h, g, conv_w_pw1[jl], conv_b_pw1[jl], conv_w_dw[jl], conv_b_dw[jl],
                                conv_ln_g[jl], conv_ln_b[jl], conv_w_pw2[jl], conv_b_pw2[jl])
            elif m == 2:
                h = _mla_layer(h, g, mla_w_in[jl], mla_q_norm_g[jl], mla_w_uq[jl], mla_kv_norm_g[jl],
                               mla_w_ukv[jl], mla_w_out[jl])
            else:
                h = _gdn_layer(h, g, gdn_w_in[jl], gdn_conv_w[jl], gdn_a_log[jl], gdn_dt_bias[jl],
                               gdn_o_norm_g[jl], gdn_w_out[jl])
            h = _mlp(h, norm_mlp_g[i], mlp_w1[i].astype(MXU_DTYPE), mlp_w2[i].astype(MXU_DTYPE),
                     final_g, final_norm=(i == depth - 1))
        outs.append(h)
    return outs[0].reshape(1, L, D) if b == 1 else jnp.stack(outs)
```

```python
import functools
import math

import jax
import jax.numpy as jnp
import numpy as np
from jax import lax
from jax.experimental import pallas as pl
from jax.experimental.pallas import tpu as pltpu

F32 = jnp.float32
MXU_DTYPE = jnp.bfloat16
EPS = 1e-6
LANES = 128
SUBLANES = 8
VMEM_LIMIT_BYTES = 56 * 1024 * 1024
SOFTMAX_DTYPE = jnp.bfloat16
NEG_BIG = -(2.0 ** 100)
INT_MIN = -(2 ** 31)

Q_BLOCK = 128
IDX_HEADS = 8
IDX_DIM = 64
TOPK_MAX = 256
DSA_HEADS = 8
CONV_WIDTH = 31
MLA_HEADS = 16
MLA_Q_LORA = 384
MLA_KV_LORA = 256
MLA_NOPE = 64
MLA_ROPE = 32
MLA_V = 64
ROPE_THETA = 10000.0
GDN_HEADS = 8
GDN_DK = 128
GDN_DV = 128
GDN_CONV = 4
GDN_CHUNK = 64


def _params(*sem):
    return pltpu.CompilerParams(dimension_semantics=sem, vmem_limit_bytes=VMEM_LIMIT_BYTES)


def _dot(a, b):
    return jnp.dot(a.astype(MXU_DTYPE), b.astype(MXU_DTYPE), preferred_element_type=F32)


def _dot_nt(a, b):
    return lax.dot_general(a.astype(MXU_DTYPE), b.astype(MXU_DTYPE),
                           (((1,), (1,)), ((), ())), preferred_element_type=F32)


def _rmsnorm_rows(x, g):
    return x * lax.rsqrt(jnp.mean(x * x, axis=-1, keepdims=True) + EPS) * g


def _sigmoid(x):
    return 1.0 / (1.0 + jnp.exp(-x))


def _silu(x):
    return x * _sigmoid(x)


def _const_spec(shape):
    return pl.BlockSpec(shape, lambda *_: (0,) * len(shape))


LOG2E = math.log2(math.e)
LOGITS_AHEAD = 2


def _causal_steps(n_q, tq, tk):
    qi, kj = [], []
    for i in range(n_q):
        for j in range(((i + 1) * tq - 1) // tk + 1):
            qi.append(i)
            kj.append(j)
    return jnp.asarray(np.array(qi, np.int32)), jnp.asarray(np.array(kj, np.int32))


def _flash_heads(q_ref, k_ref, v_ref, m_ref, l_ref, acc_ref, bias, nheads):
    tk = k_ref.shape[0]
    ones = jnp.ones((tk, LANES), MXU_DTYPE)

    def logits(h):
        cs = slice(h * LANES, (h + 1) * LANES)
        return _dot_nt(q_ref[:, cs], k_ref[:, cs])

    ahead = [logits(h) for h in range(min(LOGITS_AHEAD, nheads))]
    for h in range(nheads):
        cs = slice(h * LANES, (h + 1) * LANES)
        s = ahead.pop(0)
        if h + LOGITS_AHEAD < nheads:
            ahead.append(logits(h + LOGITS_AHEAD))
        s = s.astype(SOFTMAX_DTYPE) + bias
        m_prev = m_ref[h]
        m_next = jnp.maximum(m_prev, jnp.max(s, axis=1, keepdims=True).astype(F32))
        alpha = jnp.exp2(m_prev - m_next)
        p = jnp.exp2(s - m_next[:, 0:1].astype(SOFTMAX_DTYPE)).astype(MXU_DTYPE)
        m_ref[h] = m_next
        if l_ref is None:
            acc_ref[:, cs] = alpha * acc_ref[:, cs] + jnp.dot(p, v_ref[:, cs], preferred_element_type=F32)
        else:
            v_ext = jnp.concatenate([v_ref[:, cs].astype(MXU_DTYPE), ones], axis=1)
            pv = jnp.dot(p, v_ext, preferred_element_type=F32)
            l_ref[h] = alpha * l_ref[h] + pv[:, LANES:]
            acc_ref[:, cs] = alpha * acc_ref[:, cs] + pv[:, :LANES]


def _flash_finish(o_ref, l_ref, acc_ref, nheads):
    half = LANES // 2
    for h in range(nheads):
        cs = slice(h * LANES, (h + 1) * LANES)
        acc = acc_ref[:, cs]
        if l_ref is None:
            lane = lax.broadcasted_iota(jnp.int32, acc.shape, 1)
            out = jnp.where(lane < half, acc / pltpu.roll(acc, half, axis=1), 0.0)
        else:
            out = acc / l_ref[h]
        o_ref[:, cs] = out.astype(o_ref.dtype)


def _mlp_kernel(h_ref, g_ref, w1_ref, w2_ref, gf_ref, o_ref, xn_ref, acc_ref, *, final_norm):
    f = pl.program_id(1)

    @pl.when(f == 0)
    def _():
        xn_ref[...] = _rmsnorm_rows(h_ref[...], g_ref[...]).astype(xn_ref.dtype)
        acc_ref[...] = jnp.zeros_like(acc_ref)

    a = _dot(xn_ref[...], w1_ref[...])
    a = jnp.square(jnp.maximum(a, 0.0))
    acc_ref[...] += _dot(a, w2_ref[...])

    @pl.when(f == pl.num_programs(1) - 1)
    def _():
        y = h_ref[...] + acc_ref[...]
        if final_norm:
            y = _rmsnorm_rows(y, gf_ref[...])
        o_ref[...] = y


def _mlp(h, g, w1, w2, gf, final_norm, tm=1024, tf=2048):
    L, D = h.shape
    dff = w1.shape[1]
    return pl.pallas_call(
        functools.partial(_mlp_kernel, final_norm=final_norm),
        grid=(L // tm, dff // tf),
        in_specs=[
            pl.BlockSpec((tm, D), lambda i, f: (i, 0)),
            _const_spec((1, D)),
            pl.BlockSpec((D, tf), lambda i, f: (0, f)),
            pl.BlockSpec((tf, D), lambda i, f: (f, 0)),
            _const_spec((1, D)),
        ],
        out_specs=pl.BlockSpec((tm, D), lambda i, f: (i, 0)),
        out_shape=jax.ShapeDtypeStruct((L, D), F32),
        scratch_shapes=[pltpu.VMEM((tm, D), MXU_DTYPE), pltpu.VMEM((tm, D), F32)],
        compiler_params=_params("arbitrary", "arbitrary"),
        name="mlp",
    )(h, g.reshape(1, D), w1, w2, gf.reshape(1, D))


def _out_proj_kernel(a_ref, w_ref, b_ref, h_ref, o_ref):
    o_ref[...] = h_ref[...] + _dot(a_ref[...], w_ref[...]) + b_ref[...]


def _out_proj(a, w, b, h, tm=512):
    L, K = a.shape
    D = w.shape[1]
    return pl.pallas_call(
        _out_proj_kernel,
        grid=(L // tm,),
        in_specs=[
            pl.BlockSpec((tm, K), lambda i: (i, 0)),
            _const_spec((K, D)),
            _const_spec((1, D)),
            pl.BlockSpec((tm, D), lambda i: (i, 0)),
        ],
        out_specs=pl.BlockSpec((tm, D), lambda i: (i, 0)),
        out_shape=jax.ShapeDtypeStruct((L, D), F32),
        compiler_params=_params("arbitrary"),
        name="out_proj",
    )(a, w, b.reshape(1, D), h)


def _dsa_proj_kernel(h_ref, g_ref, w_ref, lg_ref, lb_ref,
                     q_ref, k_ref, v_ref, qi_ref, kia_ref, kib_ref, wi_ref, *, d, scale, wi_scale):
    xn = _rmsnorm_rows(h_ref[...], g_ref[...]).astype(MXU_DTYPE)

    def mm(c0, n):
        return jnp.dot(xn, w_ref[:, c0:c0 + n], preferred_element_type=F32)

    cw = 512
    for c in range(0, d, cw):
        q_ref[:, c:c + cw] = (mm(c, cw) * scale).astype(q_ref.dtype)
        k_ref[:, c:c + cw] = mm(d + c, cw).astype(k_ref.dtype)
        v_ref[:, c:c + cw] = mm(2 * d + c, cw).astype(v_ref.dtype)
    nqi = IDX_HEADS * IDX_DIM
    qi_ref[...] = mm(3 * d, nqi).astype(qi_ref.dtype)
    xa = mm(3 * d + nqi, LANES)
    xb = mm(3 * d + nqi + LANES, LANES)
    lane = lax.broadcasted_iota(jnp.int32, xa.shape, 1)

    def masked_ln(x, m, gain, bias):
        mu = jnp.sum(jnp.where(m, x, 0.0), axis=-1, keepdims=True) * (1.0 / IDX_DIM)
        dlt = jnp.where(m, x - mu, 0.0)
        var = jnp.sum(dlt * dlt, axis=-1, keepdims=True) * (1.0 / IDX_DIM)
        return dlt * lax.rsqrt(var + EPS) * gain + bias

    kia_ref[...] = masked_ln(xa, lane < IDX_DIM, lg_ref[0:1, :], lb_ref[0:1, :]).astype(kia_ref.dtype)
    kib_ref[...] = masked_ln(xb, lane >= IDX_DIM, lg_ref[1:2, :], lb_ref[1:2, :]).astype(kib_ref.dtype)
    wi_ref[...] = jnp.where(lane < IDX_HEADS, xb, 0.0) * wi_scale


def _dsa_proj(h, g, w_cat, lg2, lb2, tm=1024):
    L, D = h.shape
    N = w_cat.shape[1]
    nqi = IDX_HEADS * IDX_DIM
    row = lambda n: pl.BlockSpec((tm, n), lambda i: (i, 0))
    return pl.pallas_call(
        functools.partial(_dsa_proj_kernel, d=D, scale=(D // DSA_HEADS) ** -0.5 * LOG2E,
                          wi_scale=IDX_HEADS ** -0.5 * IDX_DIM ** -0.5),
        grid=(L // tm,),
        in_specs=[row(D), _const_spec((1, D)), _const_spec((D, N)),
                  _const_spec((2, LANES)), _const_spec((2, LANES))],
        out_specs=[row(D), row(D), row(D), row(nqi), row(LANES), row(LANES), row(LANES)],
        out_shape=[jax.ShapeDtypeStruct((L, D), MXU_DTYPE)] * 3
        + [jax.ShapeDtypeStruct((L, nqi), MXU_DTYPE)]
        + [jax.ShapeDtypeStruct((L, LANES), MXU_DTYPE)] * 2
        + [jax.ShapeDtypeStruct((L, LANES), F32)],
        compiler_params=_params("arbitrary"),
        name="dsa_proj",
    )(h, g.reshape(1, D), w_cat, lg2, lb2)


I16 = jnp.int16
I16_MIN = -(2 ** 15)
TIE_ROWS = 16


def _dsa_select_kernel(qi_ref, wi_ref, kia_ref, kib_ref, bias_ref, hi_ref, lo_ref, cand_ref,
                       thr_hi_ref, thr_lo_ref, surplus_ref,
                       *, tq, tk, topk, rb):
    i = pl.program_id(0)
    n_tiles = hi_ref.shape[0]
    n_kt = ((i + 1) * tq + tk - 1) // tk
    q0 = i * tq
    qpos = q0 + lax.broadcasted_iota(jnp.int32, (tq, tk), 0)
    lane_pos = lax.broadcasted_iota(jnp.int32, (tq, tk), 1)

    def score_tile(c, carry):
        k0 = pl.multiple_of(c * tk, tk)
        ka = kia_ref[pl.ds(k0, tk), :]
        kb = kib_ref[pl.ds(k0, tk), :]
        acc = jnp.zeros((tq, tk), F32)
        for p in range(IDX_HEADS // 2):
            lhs = qi_ref[:, p * LANES:(p + 1) * LANES]
            sa = jnp.maximum(_dot_nt(lhs, ka), 0.0)
            sb = jnp.maximum(_dot_nt(lhs, kb), 0.0)
            acc = acc + wi_ref[:, 2 * p:2 * p + 1] * sa
            acc = acc + wi_ref[:, 2 * p + 1:2 * p + 2] * sb
        bits = pltpu.bitcast(acc, jnp.int32)
        key = bits ^ ((bits >> 31) & jnp.int32(0x7FFFFFFF))
        key = jnp.where(k0 + lane_pos <= qpos, key, jnp.int32(INT_MIN))
        hi_ref[c] = (key >> 16).astype(I16)
        lo_ref[c] = ((key & 0xFFFF) + I16_MIN).astype(I16)
        return carry

    lax.fori_loop(0, n_kt, score_tile, 0)

    def count(pred):
        def body(c, acc):
            parts = []
            for r0 in range(0, tq, rb):
                cand = cand_ref[r0:r0 + rb, :]
                part = jnp.zeros((rb, LANES), I16)
                for t in range(tk // LANES):
                    ls = slice(t * LANES, (t + 1) * LANES)
                    hit = pred(hi_ref[c, r0:r0 + rb, ls], lo_ref[c, r0:r0 + rb, ls], cand,
                               c * tk + t * LANES, r0)
                    part = part + hit.astype(I16)
                parts.append(part)
            return acc + jnp.concatenate(parts, axis=0)
        acc = lax.fori_loop(0, n_kt, body, jnp.zeros((tq, LANES), I16))
        return jnp.sum(acc.astype(F32), axis=1, keepdims=True)

    def set_cand(x):
        cand_ref[...] = jnp.broadcast_to(x, (tq, LANES)).astype(I16)

    def bisect16(pred, n_ge0):
        def bit_step(carry):
            b, c_best, n_best = carry
            cand = c_best + lax.shift_left(jnp.int32(1), 15 - b)
            set_cand(cand)
            cnt = count(pred)
            keep = cnt >= topk
            return b + 1, jnp.where(keep, cand, c_best), jnp.where(keep, cnt, n_best)

        def unresolved(carry):
            b, _, n_best = carry
            return jnp.logical_and(b < 16, jnp.max(jnp.abs(n_best - topk)) > 0.0)

        _, c_best, n_best = lax.while_loop(
            unresolved, bit_step, (jnp.int32(0), jnp.full((tq, 1), I16_MIN, jnp.int32), n_ge0))
        return c_best, n_best

    n_all = jnp.broadcast_to((n_kt * tk).astype(F32), (tq, 1))
    thr_hi, n_hi = bisect16(lambda hi, lo, cd, k0, r0: hi >= cd, n_all)
    thr_hi16 = jnp.broadcast_to(thr_hi, (tq, LANES)).astype(I16)

    def clamp_low(c, carry):
        for r0 in range(0, tq, rb):
            hi = hi_ref[c, r0:r0 + rb, :]
            th = jnp.concatenate([thr_hi16[r0:r0 + rb]] * (tk // LANES), axis=1)
            side = jnp.where(hi > th, jnp.asarray(2 ** 15 - 1, I16), jnp.asarray(I16_MIN, I16))
            lo_ref[c, r0:r0 + rb, :] = jnp.where(hi == th, lo_ref[c, r0:r0 + rb, :], side)
        return carry

    lax.fori_loop(0, n_kt, clamp_low, 0)
    thr_lo, n_ge = bisect16(lambda hi, lo, cd, k0, r0: lo >= cd, n_hi)
    thr_lo16 = jnp.broadcast_to(thr_lo, (tq, LANES)).astype(I16)
    is_floor = jnp.logical_and(thr_hi == I16_MIN, thr_lo == I16_MIN)

    cand_ref[...] = jnp.broadcast_to(jnp.where(is_floor, -1, 2 ** 15 - 1), (tq, LANES)).astype(I16)
    thr_hi_ref[...] = thr_hi16
    thr_lo_ref[...] = thr_lo16
    surplus_ref[...] = jnp.broadcast_to(jnp.where(is_floor, 0.0, n_ge - topk), (tq, LANES))
    lane_idx16 = lax.broadcasted_iota(jnp.int32, (TIE_ROWS, LANES), 1)

    def tie_group(gi, carry):
        r0 = pl.multiple_of(gi * TIE_ROWS, TIE_ROWS)
        rows = pl.ds(r0, TIE_ROWS)

        @pl.when(jnp.max(surplus_ref[rows, :]) > 0.0)
        def _ties():
            th = thr_hi_ref[rows, :]
            tl = thr_lo_ref[rows, :]

            def count_rows(pred):
                def body(c, acc):
                    part = jnp.zeros((TIE_ROWS, LANES), I16)
                    for t in range(tk // LANES):
                        ls = slice(t * LANES, (t + 1) * LANES)
                        hit = pred(hi_ref[c, rows, ls], lo_ref[c, rows, ls], c * tk + t * LANES)
                        part = part + hit.astype(I16)
                    return acc + part
                acc = lax.fori_loop(0, n_kt, body, jnp.zeros((TIE_ROWS, LANES), I16))
                return jnp.sum(acc.astype(F32), axis=1, keepdims=True)

            need = topk - count_rows(lambda hi, lo, k0: jnp.logical_or(
                hi > th, jnp.logical_and(hi == th, lo > tl)))

            def idx_step(b, cut):
                cand = cut + lax.shift_left(jnp.int32(1), 14 - b)
                cd = jnp.broadcast_to(cand, (TIE_ROWS, LANES)).astype(I16)
                cnt = count_rows(lambda hi, lo, k0: jnp.logical_and(
                    jnp.logical_and(hi == th, lo == tl), (k0 + lane_idx16).astype(I16) < cd))
                return jnp.where(cnt < need, cand, cut)

            cut = lax.fori_loop(0, 15, idx_step, jnp.zeros((TIE_ROWS, 1), jnp.int32))
            floor = jnp.logical_and(th.astype(jnp.int32) == I16_MIN, tl.astype(jnp.int32) == I16_MIN)
            cand_ref[rows, :] = jnp.where(floor, -1, jnp.broadcast_to(cut, (TIE_ROWS, LANES))).astype(I16)

        return carry

    lax.fori_loop(0, tq // TIE_ROWS, tie_group, 0)

    lane_idx = lax.broadcasted_iota(jnp.int32, (rb, tk), 1)

    def write_tile(c, carry):
        for r0 in range(0, tq, rb):
            hi = hi_ref[c, r0:r0 + rb, :]
            lo = lo_ref[c, r0:r0 + rb, :]
            th = jnp.concatenate([thr_hi16[r0:r0 + rb]] * (tk // LANES), axis=1)
            tl = jnp.concatenate([thr_lo16[r0:r0 + rb]] * (tk // LANES), axis=1)
            cut = jnp.concatenate([cand_ref[r0:r0 + rb, :]] * (tk // LANES), axis=1)
            gt = jnp.logical_or(hi > th, jnp.logical_and(hi == th, lo > tl))
            eq = jnp.logical_and(jnp.logical_and(hi == th, lo == tl),
                                 (c * tk + lane_idx).astype(I16) <= cut)
            keep = jnp.logical_or(gt, eq)
            bias_ref[c, r0:r0 + rb, :] = jnp.where(
                keep, jnp.zeros((), SOFTMAX_DTYPE), jnp.asarray(NEG_BIG, SOFTMAX_DTYPE))
        return carry

    lax.fori_loop(0, n_kt, write_tile, 0)

    def fill_tile(c, carry):
        bias_ref[c] = jnp.full((tq, tk), NEG_BIG, SOFTMAX_DTYPE)
        return carry

    lax.fori_loop(n_kt, n_tiles, fill_tile, 0)


def _dsa_select(qi, kia, kib, wi, topk, tq=256, tk=512, rb=32):
    L, nqi = qi.shape
    assert L // LANES < 2 ** 15 and L < 2 ** 15
    n_tiles = L // tk
    return pl.pallas_call(
        functools.partial(_dsa_select_kernel, tq=tq, tk=tk, topk=topk, rb=rb),
        grid=(L // tq,),
        in_specs=[
            pl.BlockSpec((tq, nqi), lambda i: (i, 0)),
            pl.BlockSpec((tq, LANES), lambda i: (i, 0)),
            _const_spec((L, LANES)),
            _const_spec((L, LANES)),
        ],
        out_specs=pl.BlockSpec((n_tiles, tq, tk), lambda i: (0, i, 0)),
        out_shape=jax.ShapeDtypeStruct((n_tiles, L, tk), SOFTMAX_DTYPE),
        scratch_shapes=[
            pltpu.VMEM((n_tiles, tq, tk), I16),
            pltpu.VMEM((n_tiles, tq, tk), I16),
            pltpu.VMEM((tq, LANES), I16),
            pltpu.VMEM((tq, LANES), I16),
            pltpu.VMEM((tq, LANES), I16),
            pltpu.VMEM((tq, LANES), F32),
        ],
        compiler_params=_params("arbitrary"),
        name="dsa_select",
    )(qi, wi, kia, kib)


def _dsa_attn_kernel(qi_s, kj_s, q_ref, k_ref, v_ref, bias_ref, o_ref, m_ref, l_ref, acc_ref,
                     *, tq, tk, nheads):
    step = pl.program_id(0)
    i = qi_s[step]
    j = kj_s[step]
    n_kt = ((i + 1) * tq + tk - 1) // tk

    @pl.when(j == 0)
    def _init():
        m_ref[...] = jnp.full_like(m_ref, NEG_BIG)
        l_ref[...] = jnp.zeros_like(l_ref)
        acc_ref[...] = jnp.zeros_like(acc_ref)

    bias = jnp.concatenate([bias_ref[t] for t in range(bias_ref.shape[0])], axis=1)
    _flash_heads(q_ref, k_ref, v_ref, m_ref, l_ref, acc_ref, bias, nheads)

    @pl.when(j == n_kt - 1)
    def _finish():
        _flash_finish(o_ref, l_ref, acc_ref, nheads)


def _dsa_attn(q, k, v, bias, tq=512, tk=1024):
    L, D = q.shape
    nheads = DSA_HEADS
    tb = bias.shape[2]
    assert D == nheads * LANES and bias.shape == (L // tb, L, tb) and tk % tb == 0
    qi_s, kj_s = _causal_steps(L // tq, tq, tk)
    qmap = lambda s, qs, ks: (qs[s], 0)
    kmap = lambda s, qs, ks: (ks[s], 0)
    grid_spec = pltpu.PrefetchScalarGridSpec(
        num_scalar_prefetch=2,
        grid=(qi_s.shape[0],),
        in_specs=[
            pl.BlockSpec((tq, D), qmap),
            pl.BlockSpec((tk, D), kmap),
            pl.BlockSpec((tk, D), kmap),
            pl.BlockSpec((tk // tb, tq, tb), lambda s, qs, ks: (ks[s], qs[s], 0)),
        ],
        out_specs=pl.BlockSpec((tq, D), qmap),
        scratch_shapes=[
            pltpu.VMEM((nheads, tq, LANES), F32),
            pltpu.VMEM((nheads, tq, LANES), F32),
            pltpu.VMEM((tq, D), F32),
        ],
    )
    return pl.pallas_call(
        functools.partial(_dsa_attn_kernel, tq=tq, tk=tk, nheads=nheads),
        grid_spec=grid_spec,
        out_shape=jax.ShapeDtypeStruct((L, D), MXU_DTYPE),
        compiler_params=_params("arbitrary"),
        name="dsa_attn",
    )(qi_s, kj_s, q, k, v, bias)


def _dsa_layer(h, g, w_in, idx_k_g, idx_k_b, w_out):
    L, D = h.shape
    nqi = IDX_HEADS * IDX_DIM
    o3 = 3 * D
    o4 = o3 + nqi
    o5 = o4 + IDX_DIM
    w_ki = w_in[:, o4:o5]
    w_wi = w_in[:, o5:]
    zpad = lambda n: jnp.zeros((D, n), w_in.dtype)
    w_cat = jnp.concatenate(
        [w_in[:, :o4],
         w_ki, w_wi, zpad(LANES - IDX_DIM - IDX_HEADS),
         w_wi, zpad(LANES - IDX_DIM - IDX_HEADS), w_ki], axis=1).astype(MXU_DTYPE)
    z = jnp.zeros((IDX_DIM,), F32)
    lg2 = jnp.stack([jnp.concatenate([idx_k_g, z]), jnp.concatenate([z, idx_k_g])])
    lb2 = jnp.stack([jnp.concatenate([idx_k_b, z]), jnp.concatenate([z, idx_k_b])])
    q, k, v, qi, kia, kib, wi = _dsa_proj(h, g, w_cat, lg2, lb2)
    bias = _dsa_select(qi, kia, kib, wi, topk=min(TOPK_MAX, L // 4))
    o = _dsa_attn(q, k, v, bias)
    return _out_proj(o, w_out.astype(MXU_DTYPE), jnp.zeros((D,), F32), h)


def _glu_proj_kernel(h_ref, g_ref, w_ref, b_ref, u_ref, *, d):
    xn = _rmsnorm_rows(h_ref[...], g_ref[...]).astype(MXU_DTYPE)
    cw = 512
    for c in range(0, d, cw):
        a = jnp.dot(xn, w_ref[:, c:c + cw], preferred_element_type=F32) + b_ref[:, c:c + cw]
        gt = jnp.dot(xn, w_ref[:, d + c:d + c + cw], preferred_element_type=F32) + b_ref[:, d + c:d + c + cw]
        u_ref[:, c:c + cw] = a * _sigmoid(gt)


def _glu_proj(h, g, w, b, tm=512):
    L, D = h.shape
    return pl.pallas_call(
        functools.partial(_glu_proj_kernel, d=D),
        grid=(L // tm,),
        in_specs=[pl.BlockSpec((tm, D), lambda i: (i, 0)), _const_spec((1, D)),
                  _const_spec((D, 2 * D)), _const_spec((1, 2 * D))],
        out_specs=pl.BlockSpec((tm, D), lambda i: (i, 0)),
        out_shape=jax.ShapeDtypeStruct((L, D), F32),
        compiler_params=_params("arbitrary"),
        name="conv_glu_proj",
    )(h, g.reshape(1, D), w, b.reshape(1, 2 * D))


CONV_HALO = 32


def _conv_out_kernel(u_ref, up_ref, wdw_ref, bdw_ref, lg_ref, lb_ref, w2_ref, b2_ref, h_ref, o_ref,
                     x_ref, xs_ref, *, tm):
    i = pl.program_id(0)
    x_ref[CONV_HALO:, :] = u_ref[...]
    x_ref[:CONV_HALO, :] = jnp.where(i > 0, up_ref[...], 0.0)
    off = CONV_HALO - (CONV_WIDTH - 1)
    acc = jnp.zeros(u_ref.shape, F32)
    for r in range(SUBLANES):
        taps = [t for t in range(CONV_WIDTH) if (off + t) % SUBLANES == r]
        if not taps:
            continue
        span = off + taps[-1] - r + tm
        src = x_ref
        if r:
            xs_ref[:span, :] = x_ref[r:r + span, :]
            src = xs_ref
        for t in taps:
            a = off + t - r
            acc = acc + wdw_ref[t:t + 1, :] * src[a:a + tm, :]
    y = acc + bdw_ref[...]
    mu = jnp.mean(y, axis=-1, keepdims=True)
    dlt = y - mu
    var = jnp.mean(dlt * dlt, axis=-1, keepdims=True)
    y = _silu(dlt * lax.rsqrt(var + EPS) * lg_ref[...] + lb_ref[...])
    o_ref[...] = h_ref[...] + _dot(y, w2_ref[...]) + b2_ref[...]


def _conv_out(u, w_dw, b_dw, ln_g, ln_b, w2, b2, h, tm=512):
    L, D = u.shape
    r = tm // CONV_HALO
    vec = lambda a: a.reshape(1, D)
    return pl.pallas_call(
        functools.partial(_conv_out_kernel, tm=tm),
        grid=(L // tm,),
        in_specs=[
            pl.BlockSpec((tm, D), lambda i: (i, 0)),
            pl.BlockSpec((CONV_HALO, D), lambda i: (jnp.maximum(i * r - 1, 0), 0)),
            _const_spec((CONV_WIDTH, D)),
            _const_spec((1, D)), _const_spec((1, D)), _const_spec((1, D)),
            _const_spec((D, D)), _const_spec((1, D)),
            pl.BlockSpec((tm, D), lambda i: (i, 0)),
        ],
        out_specs=pl.BlockSpec((tm, D), lambda i: (i, 0)),
        out_shape=jax.ShapeDtypeStruct((L, D), F32),
        scratch_shapes=[pltpu.VMEM((tm + CONV_HALO, D), F32), pltpu.VMEM((tm + CONV_HALO, D), F32)],
        compiler_params=_params("arbitrary"),
        name="conv_out",
    )(u, u, w_dw, vec(b_dw), vec(ln_g), vec(ln_b), w2, vec(b2), h)


def _conv_layer(h, g, w_pw1, b_pw1, w_dw, b_dw, ln_g, ln_b, w_pw2, b_pw2):
    u = _glu_proj(h, g, w_pw1.astype(MXU_DTYPE), b_pw1)
    return _conv_out(u, w_dw, b_dw, ln_g, ln_b, w_pw2.astype(MXU_DTYPE), b_pw2, h)


def _mla_prep_kernel(h_ref, g_ref, w1_ref, qg_ref, kvg_ref, wqa_ref, wqb_ref, wka_ref, wv_ref,
                     c_ref, s_ref, q_ref, k_ref, v_ref, *, scale):
    xn = _rmsnorm_rows(h_ref[...], g_ref[...]).astype(MXU_DTYPE)
    proj = jnp.dot(xn, w1_ref[...], preferred_element_type=F32)
    o1 = MLA_Q_LORA
    o2 = o1 + MLA_KV_LORA
    cq = _rmsnorm_rows(proj[:, :o1], qg_ref[...]).astype(MXU_DTYPE)
    ckv = _rmsnorm_rows(proj[:, o1:o2], kvg_ref[...]).astype(MXU_DTYPE)
    cos = c_ref[...]
    sin = s_ref[...]
    kr = proj[:, o2:o2 + LANES] * cos + proj[:, o2 + LANES:o2 + 2 * LANES] * sin
    for hd in range(MLA_HEADS):
        cs = slice(hd * LANES, (hd + 1) * LANES)
        qa = jnp.dot(cq, wqa_ref[:, cs], preferred_element_type=F32)
        qb = jnp.dot(cq, wqb_ref[:, cs], preferred_element_type=F32)
        q_ref[:, cs] = ((qa * cos + qb * sin) * scale).astype(q_ref.dtype)
        ka = jnp.dot(ckv, wka_ref[:, cs], preferred_element_type=F32)
        k_ref[:, cs] = (ka + kr).astype(k_ref.dtype)
        va = jnp.dot(ckv, wv_ref[:, cs], preferred_element_type=F32)
        lane = lax.broadcasted_iota(jnp.int32, va.shape, 1)
        v_ref[:, cs] = jnp.where(lane < MLA_V, va, 1.0).astype(v_ref.dtype)


def _mla_prep(h, g, w1, qg, kvg, wqa, wqb, wka, wv, ctab, stab, tm=512):
    L, D = h.shape
    HP = MLA_HEADS * LANES
    row = lambda n: pl.BlockSpec((tm, n), lambda i: (i, 0))
    full = lambda a: _const_spec(a.shape)
    return pl.pallas_call(
        functools.partial(_mla_prep_kernel, scale=(MLA_NOPE + MLA_ROPE) ** -0.5 * LOG2E),
        grid=(L // tm,),
        in_specs=[row(D), _const_spec((1, D)), full(w1), _const_spec((1, MLA_Q_LORA)),
                  _const_spec((1, MLA_KV_LORA)), full(wqa), full(wqb), full(wka), full(wv),
                  row(LANES), row(LANES)],
        out_specs=[row(HP), row(HP), row(HP)],
        out_shape=[jax.ShapeDtypeStruct((L, HP), MXU_DTYPE)] * 3,
        compiler_params=_params("arbitrary"),
        name="mla_prep",
    )(h, g.reshape(1, D), w1, qg.reshape(1, -1), kvg.reshape(1, -1), wqa, wqb, wka, wv, ctab, stab)


def _mla_attn_kernel(qi_s, kj_s, q_ref, k_ref, v_ref, o_ref, m_ref, acc_ref, *, tq, tk, nheads):
    step = pl.program_id(0)
    i = qi_s[step]
    j = kj_s[step]
    n_kt = ((i + 1) * tq + tk - 1) // tk

    @pl.when(j == 0)
    def _init():
        m_ref[...] = jnp.full_like(m_ref, NEG_BIG)
        acc_ref[...] = jnp.zeros_like(acc_ref)

    qpos = i * tq + lax.broadcasted_iota(jnp.int32, (tq, tk), 0)
    kpos = j * tk + lax.broadcasted_iota(jnp.int32, (tq, tk), 1)
    bias = jnp.where(kpos <= qpos, 0.0, NEG_BIG).astype(SOFTMAX_DTYPE)
    _flash_heads(q_ref, k_ref, v_ref, m_ref, None, acc_ref, bias, nheads)

    @pl.when(j == n_kt - 1)
    def _finish():
        _flash_finish(o_ref, None, acc_ref, nheads)


def _mla_attn(q, k, v, tq=512, tk=1024):
    L, HP = q.shape
    qi_s, kj_s = _causal_steps(L // tq, tq, tk)
    qmap = lambda s, qs, ks: (qs[s], 0)
    kmap = lambda s, qs, ks: (ks[s], 0)
    grid_spec = pltpu.PrefetchScalarGridSpec(
        num_scalar_prefetch=2,
        grid=(qi_s.shape[0],),
        in_specs=[pl.BlockSpec((tq, HP), qmap),
                  pl.BlockSpec((tk, HP), kmap), pl.BlockSpec((tk, HP), kmap)],
        out_specs=pl.BlockSpec((tq, HP), qmap),
        scratch_shapes=[pltpu.VMEM((MLA_HEADS, tq, LANES), F32),
                        pltpu.VMEM((tq, HP), F32)],
    )
    return pl.pallas_call(
        functools.partial(_mla_attn_kernel, tq=tq, tk=tk, nheads=MLA_HEADS),
        grid_spec=grid_spec,
        out_shape=jax.ShapeDtypeStruct((L, HP), MXU_DTYPE),
        compiler_params=_params("arbitrary"),
        name="mla_attn",
    )(qi_s, kj_s, q, k, v)


def _place_heads(w, nheads, src_w, lane0, swap_half=0):
    K = w.shape[0]
    w3 = w.reshape(K, nheads, src_w)
    if swap_half:
        w3 = jnp.concatenate([w3[..., swap_half:], w3[..., :swap_half]], axis=-1)
    out = jnp.zeros((K, nheads, LANES), w.dtype)
    out = out.at[:, :, lane0:lane0 + src_w].set(w3)
    return out.reshape(K, nheads * LANES)


def _mla_layer(h, g, w_in, q_norm_g, w_uq, kv_norm_g, w_ukv, w_out):
    L, D = h.shape
    half = MLA_ROPE // 2
    o2 = MLA_Q_LORA + MLA_KV_LORA
    w_kr = w_in[:, o2:]
    w1 = jnp.concatenate(
        [w_in[:, :o2], _place_heads(w_kr, 1, MLA_ROPE, MLA_NOPE),
         _place_heads(w_kr, 1, MLA_ROPE, MLA_NOPE, swap_half=half)], axis=1).astype(MXU_DTYPE)
    uq = w_uq.reshape(MLA_Q_LORA, MLA_HEADS, MLA_NOPE + MLA_ROPE)
    uq_rope = uq[..., MLA_NOPE:].reshape(MLA_Q_LORA, MLA_HEADS * MLA_ROPE)
    wqa = _place_heads(w_uq, MLA_HEADS, MLA_NOPE + MLA_ROPE, 0).astype(MXU_DTYPE)
    wqb = _place_heads(uq_rope, MLA_HEADS, MLA_ROPE, MLA_NOPE, swap_half=half).astype(MXU_DTYPE)
    ukv = w_ukv.reshape(MLA_KV_LORA, MLA_HEADS, MLA_NOPE + MLA_V)
    wka = _place_heads(ukv[..., :MLA_NOPE].reshape(MLA_KV_LORA, -1), MLA_HEADS, MLA_NOPE, 0).astype(MXU_DTYPE)
    wv = _place_heads(ukv[..., MLA_NOPE:].reshape(MLA_KV_LORA, -1), MLA_HEADS, MLA_V, 0).astype(MXU_DTYPE)
    pos = jnp.arange(L, dtype=F32)
    inv_freq = ROPE_THETA ** (-jnp.arange(0, MLA_ROPE, 2, dtype=F32) / MLA_ROPE)
    ang = pos[:, None] * inv_freq[None, :]
    cos, sin = jnp.cos(ang), jnp.sin(ang)
    zeros = jnp.zeros((L, LANES - MLA_NOPE - MLA_ROPE), F32)
    ctab = jnp.concatenate([jnp.ones((L, MLA_NOPE), F32), cos, cos, zeros], axis=1)
    stab = jnp.concatenate([jnp.zeros((L, MLA_NOPE), F32), -sin, sin, zeros], axis=1)
    q, k, v = _mla_prep(h, g, w1, q_norm_g, kv_norm_g, wqa, wqb, wka, wv, ctab, stab)
    o = _mla_attn(q, k, v)
    wo = w_out.reshape(MLA_HEADS, MLA_V, D)
    wo_pad = jnp.concatenate([wo, jnp.zeros_like(wo)], axis=1).reshape(MLA_HEADS * LANES, D).astype(MXU_DTYPE)
    return _out_proj(o, wo_pad, jnp.zeros((D,), F32), h)


def _gdn_scan_kernel(h_ref, g_ref, w_ref, cw_ref, alog_ref, dtb_ref, o_ref, gate_ref, x_ref, s_ref,
                     *, nheads, dk, dv, c, nch):
    n = pl.program_id(0)
    halo = 8
    nqk = nheads * dk

    @pl.when(n == 0)
    def _init():
        x_ref[:halo, :] = jnp.zeros((halo, x_ref.shape[1]), F32)
        s_ref[...] = jnp.zeros_like(s_ref)

    rows = nch * c
    nqkv = x_ref.shape[1]
    nv = nheads * dv
    xn = _rmsnorm_rows(h_ref[...], g_ref[...]).astype(MXU_DTYPE)
    cw_cols = 512
    for c0 in range(0, nqkv, cw_cols):
        x_ref[halo:, c0:c0 + cw_cols] = jnp.dot(xn, w_ref[:, c0:c0 + cw_cols], preferred_element_type=F32)
    gate_ref[...] = jnp.dot(xn, w_ref[:, nqkv:nqkv + nv], preferred_element_type=F32)
    ba = jnp.dot(xn, w_ref[:, nqkv + nv:], preferred_element_type=F32)
    off = halo - (GDN_CONV - 1)
    acc = jnp.zeros((rows, nqkv), F32)
    for t in range(GDN_CONV):
        acc = acc + cw_ref[t:t + 1, :] * x_ref[off + t:off + t + rows, :]
    x_ref[:halo, :] = x_ref[rows:rows + halo, :]
    qkv = _silu(acc)

    beta_all = _sigmoid(ba[:, :LANES])
    a_raw = ba[:, LANES:] + dtb_ref[...]
    softplus = jnp.maximum(a_raw, 0.0) + jnp.log(1.0 + jnp.exp(-jnp.abs(a_raw)))
    g_all = -jnp.exp(alog_ref[...]) * softplus
    row = lax.broadcasted_iota(jnp.int32, (c, c), 0)
    col = lax.broadcasted_iota(jnp.int32, (c, c), 1)
    tril = row >= col
    tril_strict = row > col
    eye = jnp.where(row == col, 1.0, 0.0)

    units = [(ch, hd) for ch in range(nch) for hd in range(nheads)]
    gc_all = [jnp.dot(tril.astype(F32), g_all[ch * c:(ch + 1) * c], preferred_element_type=F32,
                      precision=lax.Precision.HIGHEST) for ch in range(nch)]
    gc_t = [x.T for x in gc_all]

    def chunk_rows(x, ch):
        return x[ch * c:(ch + 1) * c]

    q = [chunk_rows(qkv[:, hd * dk:(hd + 1) * dk], ch) for ch, hd in units]
    k = [chunk_rows(qkv[:, nqk + hd * dk:nqk + (hd + 1) * dk], ch) for ch, hd in units]
    v = [chunk_rows(qkv[:, 2 * nqk + hd * dv:2 * nqk + (hd + 1) * dv], ch) for ch, hd in units]
    q = [x * lax.rsqrt(jnp.sum(x * x, axis=-1, keepdims=True) + EPS) * (dk ** -0.5) for x in q]
    k = [x * lax.rsqrt(jnp.sum(x * x, axis=-1, keepdims=True) + EPS) for x in k]
    beta = [chunk_rows(beta_all[:, hd:hd + 1], ch) for ch, hd in units]
    gc = [gc_all[ch][:, hd:hd + 1] for ch, hd in units]
    gc_last = [gc_all[ch][c - 1:c, hd:hd + 1] for ch, hd in units]
    n_u = range(len(units))
    decay = [jnp.where(tril, jnp.exp(jnp.where(tril, gc[u] - gc_t[units[u][0]][units[u][1]:units[u][1] + 1, :],
                                               0.0)), 0.0) for u in n_u]
    kk = [_dot_nt(k[u], k[u]) for u in n_u]
    qk = [_dot_nt(q[u], k[u]) for u in n_u]
    a_mat = [jnp.where(tril_strict, beta[u] * kk[u] * decay[u], 0.0) for u in n_u]
    t_inv = [eye - jnp.where((row >> 1) == (col >> 1), a_mat[u], 0.0) for u in n_u]
    for lvl in range(1, int(math.log2(c))):
        blk_mask = jnp.logical_and((row >> (lvl + 1)) == (col >> (lvl + 1)), (row >> lvl) != (col >> lvl))
        tm = [_dot(t_inv[u], jnp.where(blk_mask, a_mat[u], 0.0)) for u in n_u]
        tmt = [_dot(tm[u], t_inv[u]) for u in n_u]
        t_inv = [t_inv[u] - tmt[u] for u in n_u]
    e_gc = [jnp.exp(gc[u]) for u in n_u]
    uu = [_dot(t_inv[u], v[u] * beta[u]) for u in n_u]
    w = [_dot(t_inv[u], k[u] * (beta[u] * e_gc[u])) for u in n_u]
    q_dec = [q[u] * e_gc[u] for u in n_u]
    k_dec_t = [(k[u] * jnp.exp(gc_last[u] - gc[u])).T for u in n_u]
    qk_dec = [qk[u] * decay[u] for u in n_u]
    g_last = [jnp.exp(gc_last[u]) for u in n_u]

    s = [s_ref[hd] for hd in range(nheads)]
    for ch in range(nch):
        us = [ch * nheads + hd for hd in range(nheads)]
        ws = [_dot(w[u], s[hd]) for hd, u in enumerate(us)]
        qs = [_dot(q_dec[u], s[hd]) for hd, u in enumerate(us)]
        v_new = [uu[u] - ws[hd] for hd, u in enumerate(us)]
        o_intra = [_dot(qk_dec[u], v_new[hd]) for hd, u in enumerate(us)]
        s_upd = [_dot(k_dec_t[u], v_new[hd]) for hd, u in enumerate(us)]
        for hd, u in enumerate(us):
            o_ref[ch * c:(ch + 1) * c, hd * dv:(hd + 1) * dv] = qs[hd] + o_intra[hd]
        s = [s[hd] * g_last[u] + s_upd[hd] for hd, u in enumerate(us)]
    for hd in range(nheads):
        s_ref[hd] = s[hd]


def _gdn_scan(h, g, w_cat, conv_w, a_log_pad, dt_bias_pad, nch=4):
    L, D = h.shape
    c = GDN_CHUNK
    rows = nch * c
    nqkv = 2 * GDN_HEADS * GDN_DK + GDN_HEADS * GDN_DV
    nv = GDN_HEADS * GDN_DV
    assert w_cat.shape == (D, nqkv + nv + 2 * LANES)
    return pl.pallas_call(
        functools.partial(_gdn_scan_kernel, nheads=GDN_HEADS, dk=GDN_DK, dv=GDN_DV, c=c, nch=nch),
        grid=(L // rows,),
        in_specs=[pl.BlockSpec((rows, D), lambda n: (n, 0)), _const_spec((1, D)),
                  _const_spec(w_cat.shape),
                  _const_spec((GDN_CONV, nqkv)), _const_spec((1, LANES)), _const_spec((1, LANES))],
        out_specs=[pl.BlockSpec((rows, nv), lambda n: (n, 0)), pl.BlockSpec((rows, nv), lambda n: (n, 0))],
        out_shape=[jax.ShapeDtypeStruct((L, nv), F32), jax.ShapeDtypeStruct((L, nv), F32)],
        scratch_shapes=[pltpu.VMEM((rows + 8, nqkv), F32),
                        pltpu.VMEM((GDN_HEADS, GDN_DK, GDN_DV), F32)],
        compiler_params=_params("arbitrary"),
        name="gdn_scan",
    )(h, g.reshape(1, D), w_cat, conv_w, a_log_pad, dt_bias_pad)


def _gdn_out_kernel(o_ref, gate_ref, og_ref, w_ref, h_ref, y_ref, *, nheads, dv):
    parts = []
    for hd in range(nheads):
        cs = slice(hd * dv, (hd + 1) * dv)
        on = _rmsnorm_rows(o_ref[:, cs], og_ref[...])
        parts.append((on * _silu(gate_ref[:, cs])).astype(MXU_DTYPE))
    y_ref[...] = h_ref[...] + jnp.dot(jnp.concatenate(parts, axis=1), w_ref[...],
                                      preferred_element_type=F32)


def _gdn_out(o, gate, o_norm_g, w_out, h, tm=512):
    L, D = h.shape
    nv = GDN_HEADS * GDN_DV
    return pl.pallas_call(
        functools.partial(_gdn_out_kernel, nheads=GDN_HEADS, dv=GDN_DV),
        grid=(L // tm,),
        in_specs=[pl.BlockSpec((tm, nv), lambda i: (i, 0)),
                  pl.BlockSpec((tm, nv), lambda i: (i, 0)),
                  _const_spec((1, GDN_DV)), _const_spec((nv, D)),
                  pl.BlockSpec((tm, D), lambda i: (i, 0))],
        out_specs=pl.BlockSpec((tm, D), lambda i: (i, 0)),
        out_shape=jax.ShapeDtypeStruct((L, D), F32),
        compiler_params=_params("arbitrary"),
        name="gdn_out",
    )(o, gate, o_norm_g.reshape(1, GDN_DV), w_out, h)


def _gdn_layer(h, g, w_in, conv_w, a_log, dt_bias, o_norm_g, w_out):
    L, D = h.shape
    nmain = 2 * GDN_HEADS * GDN_DK + 2 * GDN_HEADS * GDN_DV
    zpad = jnp.zeros((D, LANES - GDN_HEADS), w_in.dtype)
    w_cat = jnp.concatenate([w_in[:, :nmain], w_in[:, nmain:nmain + GDN_HEADS], zpad,
                             w_in[:, nmain + GDN_HEADS:], zpad], axis=1).astype(MXU_DTYPE)
    vpad = lambda a: jnp.concatenate([a.astype(F32), jnp.zeros((LANES - GDN_HEADS,), F32)]).reshape(1, LANES)
    o, gate = _gdn_scan(h, g, w_cat, conv_w, vpad(a_log), vpad(dt_bias))
    return _gdn_out(o, gate, o_norm_g, w_out.astype(MXU_DTYPE), h)


def kernel(x, norm_mix_g, norm_mlp_g, final_g, mlp_w1, mlp_w2, dsa_w_in, dsa_idx_k_g, dsa_idx_k_b, dsa_w_out, conv_w_pw1, conv_b_pw1, conv_w_dw, conv_b_dw, conv_ln_g, conv_ln_b, conv_w_pw2, conv_b_pw2, mla_w_in, mla_q_norm_g, mla_w_uq, mla_kv_norm_g, mla_w_ukv, mla_w_out, gdn_w_in, gdn_conv_w, gdn_a_log, gdn_dt_bias, gdn_o_norm_g, gdn_w_out):
    b, L, D = x.shape
    depth = norm_mix_g.shape[0]
    outs = []
    rows = x.reshape(b * L, D)
    for bi in range(b):
        h = rows if b == 1 else rows[bi * L:(bi + 1) * L]
        for i in range(depth):
            m = i % 4
            jl = i // 4
            g = norm_mix_g[i]
            if m == 0:
                h = _dsa_layer(h, g, dsa_w_in[jl], dsa_idx_k_g[jl], dsa_idx_k_b[jl], dsa_w_out[jl])
            elif m == 1:
                h = _conv_layer(h, g, conv_w_pw1[jl], conv_b_pw1[jl], conv_w_dw[jl], conv_b_dw[jl],
                                conv_ln_g[jl], conv_ln_b[jl], conv_w_pw2[jl], conv_b_pw2[jl])
            elif m == 2:
                h = _mla_layer(h, g, mla_w_in[jl], mla_q_norm_g[jl], mla_w_uq[jl], mla_kv_norm_g[jl],
                               mla_w_ukv[jl], mla_w_out[jl])
            else:
                h = _gdn_layer(h, g, gdn_w_in[jl], gdn_conv_w[jl], gdn_a_log[jl], gdn_dt_bias[jl],
                               gdn_o_norm_g[jl], gdn_w_out[jl])
            h = _mlp(h, norm_mlp_g[i], mlp_w1[i].astype(MXU_DTYPE), mlp_w2[i].astype(MXU_DTYPE),
                     final_g, final_norm=(i == depth - 1))
        outs.append(h)
    return outs[0].reshape(1, L, D) if b == 1 else jnp.stack(outs)
```

```python
import functools
import math

import jax
import jax.numpy as jnp
import numpy as np
from jax import lax
from jax.experimental import pallas as pl
from jax.experimental.pallas import tpu as pltpu

F32 = jnp.float32
MXU_DTYPE = jnp.bfloat16
EPS = 1e-6
LANES = 128
SUBLANES = 8
VMEM_LIMIT_BYTES = 56 * 1024 * 1024
SOFTMAX_DTYPE = jnp.bfloat16
NEG_BIG = -(2.0 ** 100)
INT_MIN = -(2 ** 31)

IDX_HEADS = 8
IDX_DIM = 64
TOPK_MAX = 256
DSA_HEADS = 8
CONV_WIDTH = 31
MLA_HEADS = 16
MLA_Q_LORA = 384
MLA_KV_LORA = 256
MLA_NOPE = 64
MLA_ROPE = 32
MLA_V = 64
ROPE_THETA = 10000.0
GDN_HEADS = 8
GDN_DK = 128
GDN_DV = 128
GDN_CONV = 4
GDN_CHUNK = 64


def _params(*sem):
    return pltpu.CompilerParams(dimension_semantics=sem, vmem_limit_bytes=VMEM_LIMIT_BYTES)


def _dot(a, b):
    return jnp.dot(a.astype(MXU_DTYPE), b.astype(MXU_DTYPE), preferred_element_type=F32)


def _dot_nt(a, b):
    return lax.dot_general(a.astype(MXU_DTYPE), b.astype(MXU_DTYPE),
                           (((1,), (1,)), ((), ())), preferred_element_type=F32)


def _rmsnorm_rows(x, g):
    return x * lax.rsqrt(jnp.mean(x * x, axis=-1, keepdims=True) + EPS) * g


def _sigmoid(x):
    return 1.0 / (1.0 + jnp.exp(-x))


def _silu(x):
    return x * _sigmoid(x)


def _const_spec(shape):
    return pl.BlockSpec(shape, lambda *_: (0,) * len(shape))


LOG2E = math.log2(math.e)
LOGITS_AHEAD = 2


def _causal_steps(n_q, tq, tk):
    qi, kj = [], []
    for i in range(n_q):
        for j in range(((i + 1) * tq - 1) // tk + 1):
            qi.append(i)
            kj.append(j)
    return jnp.asarray(np.array(qi, np.int32)), jnp.asarray(np.array(kj, np.int32))


def _flash_heads(q_ref, k_ref, v_ref, m_ref, l_ref, acc_ref, bias, nheads):
    ones = None if l_ref is None else jnp.ones((k_ref.shape[0], LANES), MXU_DTYPE)

    def logits(h):
        cs = slice(h * LANES, (h + 1) * LANES)
        return _dot_nt(q_ref[:, cs], k_ref[:, cs])

    ahead = [logits(h) for h in range(min(LOGITS_AHEAD, nheads))]
    for h in range(nheads):
        cs = slice(h * LANES, (h + 1) * LANES)
        s = ahead.pop(0)
        if h + LOGITS_AHEAD < nheads:
            ahead.append(logits(h + LOGITS_AHEAD))
        s = s.astype(SOFTMAX_DTYPE) + bias
        m_prev = m_ref[h]
        m_next = jnp.maximum(m_prev, jnp.max(s, axis=1, keepdims=True).astype(F32))
        alpha = jnp.exp2(m_prev - m_next)
        p = jnp.exp2(s - m_next[:, 0:1].astype(SOFTMAX_DTYPE)).astype(MXU_DTYPE)
        m_ref[h] = m_next
        if l_ref is None:
            acc_ref[:, cs] = alpha * acc_ref[:, cs] + jnp.dot(p, v_ref[:, cs], preferred_element_type=F32)
        else:
            v_ext = jnp.concatenate([v_ref[:, cs].astype(MXU_DTYPE), ones], axis=1)
            pv = jnp.dot(p, v_ext, preferred_element_type=F32)
            l_ref[h] = alpha * l_ref[h] + pv[:, LANES:]
            acc_ref[:, cs] = alpha * acc_ref[:, cs] + pv[:, :LANES]


def _flash_finish(o_ref, l_ref, acc_ref, nheads):
    half = LANES // 2
    for h in range(nheads):
        cs = slice(h * LANES, (h + 1) * LANES)
        acc = acc_ref[:, cs]
        if l_ref is None:
            lane = lax.broadcasted_iota(jnp.int32, acc.shape, 1)
            out = jnp.where(lane < half, acc / pltpu.roll(acc, half, axis=1), 0.0)
        else:
            out = acc / l_ref[h]
        o_ref[:, cs] = out.astype(o_ref.dtype)


def _mlp_kernel(h_ref, g_ref, w1_ref, w2_ref, gf_ref, o_ref, xn_ref, acc_ref, *, final_norm):
    f = pl.program_id(1)

    @pl.when(f == 0)
    def _():
        xn_ref[...] = _rmsnorm_rows(h_ref[...], g_ref[...]).astype(xn_ref.dtype)
        acc_ref[...] = jnp.zeros_like(acc_ref)

    a = _dot(xn_ref[...], w1_ref[...])
    a = jnp.square(jnp.maximum(a, 0.0))
    acc_ref[...] += _dot(a, w2_ref[...])

    @pl.when(f == pl.num_programs(1) - 1)
    def _():
        y = h_ref[...] + acc_ref[...]
        if final_norm:
            y = _rmsnorm_rows(y, gf_ref[...])
        o_ref[...] = y


def _mlp(h, g, w1, w2, gf, final_norm, tm=1024, tf=2048):
    L, D = h.shape
    dff = w1.shape[1]
    return pl.pallas_call(
        functools.partial(_mlp_kernel, final_norm=final_norm),
        grid=(L // tm, dff // tf),
        in_specs=[
            pl.BlockSpec((tm, D), lambda i, f: (i, 0)),
            _const_spec((1, D)),
            pl.BlockSpec((D, tf), lambda i, f: (0, f)),
            pl.BlockSpec((tf, D), lambda i, f: (f, 0)),
            _const_spec((1, D)),
        ],
        out_specs=pl.BlockSpec((tm, D), lambda i, f: (i, 0)),
        out_shape=jax.ShapeDtypeStruct((L, D), F32),
        scratch_shapes=[pltpu.VMEM((tm, D), MXU_DTYPE), pltpu.VMEM((tm, D), F32)],
        compiler_params=_params("arbitrary", "arbitrary"),
        name="mlp",
    )(h, g.reshape(1, D), w1, w2, gf.reshape(1, D))


def _out_proj_kernel(a_ref, w_ref, b_ref, h_ref, o_ref):
    o_ref[...] = h_ref[...] + _dot(a_ref[...], w_ref[...]) + b_ref[...]


def _out_proj(a, w, b, h, tm=512):
    L, K = a.shape
    D = w.shape[1]
    return pl.pallas_call(
        _out_proj_kernel,
        grid=(L // tm,),
        in_specs=[
            pl.BlockSpec((tm, K), lambda i: (i, 0)),
            _const_spec((K, D)),
            _const_spec((1, D)),
            pl.BlockSpec((tm, D), lambda i: (i, 0)),
        ],
        out_specs=pl.BlockSpec((tm, D), lambda i: (i, 0)),
        out_shape=jax.ShapeDtypeStruct((L, D), F32),
        compiler_params=_params("arbitrary"),
        name="out_proj",
    )(a, w, b.reshape(1, D), h)


def _dsa_proj_kernel(h_ref, g_ref, w_ref, lg_ref, lb_ref,
                     q_ref, k_ref, v_ref, qi_ref, kia_ref, kib_ref, wi_ref, *, d, scale, wi_scale):
    xn = _rmsnorm_rows(h_ref[...], g_ref[...]).astype(MXU_DTYPE)

    def mm(c0, n):
        return jnp.dot(xn, w_ref[:, c0:c0 + n], preferred_element_type=F32)

    cw = 512
    for c in range(0, d, cw):
        q_ref[:, c:c + cw] = (mm(c, cw) * scale).astype(q_ref.dtype)
        k_ref[:, c:c + cw] = mm(d + c, cw).astype(k_ref.dtype)
        v_ref[:, c:c + cw] = mm(2 * d + c, cw).astype(v_ref.dtype)
    nqi = IDX_HEADS * IDX_DIM
    qi_ref[...] = mm(3 * d, nqi).astype(qi_ref.dtype)
    xa = mm(3 * d + nqi, LANES)
    xb = mm(3 * d + nqi + LANES, LANES)
    lane = lax.broadcasted_iota(jnp.int32, xa.shape, 1)

    def masked_ln(x, m, gain, bias):
        mu = jnp.sum(jnp.where(m, x, 0.0), axis=-1, keepdims=True) * (1.0 / IDX_DIM)
        dlt = jnp.where(m, x - mu, 0.0)
        var = jnp.sum(dlt * dlt, axis=-1, keepdims=True) * (1.0 / IDX_DIM)
        return dlt * lax.rsqrt(var + EPS) * gain + bias

    kia_ref[...] = masked_ln(xa, lane < IDX_DIM, lg_ref[0:1, :], lb_ref[0:1, :]).astype(kia_ref.dtype)
    kib_ref[...] = masked_ln(xb, lane >= IDX_DIM, lg_ref[1:2, :], lb_ref[1:2, :]).astype(kib_ref.dtype)
    wi_ref[...] = jnp.where(lane < IDX_HEADS, xb, 0.0) * wi_scale


def _dsa_proj(h, g, w_cat, lg2, lb2, tm=1024):
    L, D = h.shape
    N = w_cat.shape[1]
    nqi = IDX_HEADS * IDX_DIM
    row = lambda n: pl.BlockSpec((tm, n), lambda i: (i, 0))
    return pl.pallas_call(
        functools.partial(_dsa_proj_kernel, d=D, scale=(D // DSA_HEADS) ** -0.5 * LOG2E,
                          wi_scale=IDX_HEADS ** -0.5 * IDX_DIM ** -0.5),
        grid=(L // tm,),
        in_specs=[row(D), _const_spec((1, D)), _const_spec((D, N)),
                  _const_spec((2, LANES)), _const_spec((2, LANES))],
        out_specs=[row(D), row(D), row(D), row(nqi), row(LANES), row(LANES), row(LANES)],
        out_shape=[jax.ShapeDtypeStruct((L, D), MXU_DTYPE)] * 3
        + [jax.ShapeDtypeStruct((L, nqi), MXU_DTYPE)]
        + [jax.ShapeDtypeStruct((L, LANES), MXU_DTYPE)] * 2
        + [jax.ShapeDtypeStruct((L, LANES), F32)],
        compiler_params=_params("arbitrary"),
        name="dsa_proj",
    )(h, g.reshape(1, D), w_cat, lg2, lb2)


I16 = jnp.int16
I16_MIN = -(2 ** 15)
TIE_ROWS = 16


def _dsa_select_kernel(qi_ref, wi_ref, kia_ref, kib_ref, bias_ref, hi_ref, lo_ref, cand_ref,
                       thr_hi_ref, thr_lo_ref, surplus_ref,
                       *, tq, tk, topk, rb):
    i = pl.program_id(0)
    n_tiles = hi_ref.shape[0]
    n_kt = ((i + 1) * tq + tk - 1) // tk
    q0 = i * tq
    qpos = q0 + lax.broadcasted_iota(jnp.int32, (tq, tk), 0)
    lane_pos = lax.broadcasted_iota(jnp.int32, (tq, tk), 1)

    def score_tile(c, carry):
        k0 = pl.multiple_of(c * tk, tk)
        ka = kia_ref[pl.ds(k0, tk), :]
        kb = kib_ref[pl.ds(k0, tk), :]
        acc = jnp.zeros((tq, tk), F32)
        for p in range(IDX_HEADS // 2):
            lhs = qi_ref[:, p * LANES:(p + 1) * LANES]
            sa = jnp.maximum(_dot_nt(lhs, ka), 0.0)
            sb = jnp.maximum(_dot_nt(lhs, kb), 0.0)
            acc = acc + wi_ref[:, 2 * p:2 * p + 1] * sa
            acc = acc + wi_ref[:, 2 * p + 1:2 * p + 2] * sb
        bits = pltpu.bitcast(acc, jnp.int32)
        key = bits ^ ((bits >> 31) & jnp.int32(0x7FFFFFFF))
        key = jnp.where(k0 + lane_pos <= qpos, key, jnp.int32(INT_MIN))
        hi_ref[c] = (key >> 16).astype(I16)
        lo_ref[c] = ((key & 0xFFFF) + I16_MIN).astype(I16)
        return carry

    lax.fori_loop(0, n_kt, score_tile, 0)

    def count(pred):
        def body(c, acc):
            parts = []
            for r0 in range(0, tq, rb):
                cand = cand_ref[r0:r0 + rb, :]
                part = jnp.zeros((rb, LANES), I16)
                for t in range(tk // LANES):
                    ls = slice(t * LANES, (t + 1) * LANES)
                    hit = pred(hi_ref[c, r0:r0 + rb, ls], lo_ref[c, r0:r0 + rb, ls], cand,
                               c * tk + t * LANES, r0)
                    part = part + hit.astype(I16)
                parts.append(part)
            return acc + jnp.concatenate(parts, axis=0)
        acc = lax.fori_loop(0, n_kt, body, jnp.zeros((tq, LANES), I16))
        return jnp.sum(acc.astype(F32), axis=1, keepdims=True)

    def set_cand(x):
        cand_ref[...] = jnp.broadcast_to(x, (tq, LANES)).astype(I16)

    def bisect16(pred, n_ge0):
        def bit_step(carry):
            b, c_best, n_best = carry
            cand = c_best + lax.shift_left(jnp.int32(1), 15 - b)
            set_cand(cand)
            cnt = count(pred)
            keep = cnt >= topk
            return b + 1, jnp.where(keep, cand, c_best), jnp.where(keep, cnt, n_best)

        def unresolved(carry):
            b, _, n_best = carry
            return jnp.logical_and(b < 16, jnp.max(jnp.abs(n_best - topk)) > 0.0)

        _, c_best, n_best = lax.while_loop(
            unresolved, bit_step, (jnp.int32(0), jnp.full((tq, 1), I16_MIN, jnp.int32), n_ge0))
        return c_best, n_best

    n_all = jnp.broadcast_to((n_kt * tk).astype(F32), (tq, 1))
    thr_hi, n_hi = bisect16(lambda hi, lo, cd, k0, r0: hi >= cd, n_all)
    thr_hi16 = jnp.broadcast_to(thr_hi, (tq, LANES)).astype(I16)

    def clamp_low(c, carry):
        for r0 in range(0, tq, rb):
            hi = hi_ref[c, r0:r0 + rb, :]
            th = jnp.concatenate([thr_hi16[r0:r0 + rb]] * (tk // LANES), axis=1)
            side = jnp.where(hi > th, jnp.asarray(2 ** 15 - 1, I16), jnp.asarray(I16_MIN, I16))
            lo_ref[c, r0:r0 + rb, :] = jnp.where(hi == th, lo_ref[c, r0:r0 + rb, :], side)
        return carry

    lax.fori_loop(0, n_kt, clamp_low, 0)
    thr_lo, n_ge = bisect16(lambda hi, lo, cd, k0, r0: lo >= cd, n_hi)
    thr_lo16 = jnp.broadcast_to(thr_lo, (tq, LANES)).astype(I16)
    is_floor = jnp.logical_and(thr_hi == I16_MIN, thr_lo == I16_MIN)

    cand_ref[...] = jnp.broadcast_to(jnp.where(is_floor, -1, 2 ** 15 - 1), (tq, LANES)).astype(I16)
    thr_hi_ref[...] = thr_hi16
    thr_lo_ref[...] = thr_lo16
    surplus_ref[...] = jnp.broadcast_to(jnp.where(is_floor, 0.0, n_ge - topk), (tq, LANES))
    lane_idx16 = lax.broadcasted_iota(jnp.int32, (TIE_ROWS, LANES), 1)

    def tie_group(gi, carry):
        r0 = pl.multiple_of(gi * TIE_ROWS, TIE_ROWS)
        rows = pl.ds(r0, TIE_ROWS)

        @pl.when(jnp.max(surplus_ref[rows, :]) > 0.0)
        def _ties():
            th = thr_hi_ref[rows, :]
            tl = thr_lo_ref[rows, :]

            def count_rows(pred):
                def body(c, acc):
                    part = jnp.zeros((TIE_ROWS, LANES), I16)
                    for t in range(tk // LANES):
                        ls = slice(t * LANES, (t + 1) * LANES)
                        hit = pred(hi_ref[c, rows, ls], lo_ref[c, rows, ls], c * tk + t * LANES)
                        part = part + hit.astype(I16)
                    return acc + part
                acc = lax.fori_loop(0, n_kt, body, jnp.zeros((TIE_ROWS, LANES), I16))
                return jnp.sum(acc.astype(F32), axis=1, keepdims=True)

            need = topk - count_rows(lambda hi, lo, k0: jnp.logical_or(
                hi > th, jnp.logical_and(hi == th, lo > tl)))

            def idx_step(b, cut):
                cand = cut + lax.shift_left(jnp.int32(1), 14 - b)
                cd = jnp.broadcast_to(cand, (TIE_ROWS, LANES)).astype(I16)
                cnt = count_rows(lambda hi, lo, k0: jnp.logical_and(
                    jnp.logical_and(hi == th, lo == tl), (k0 + lane_idx16).astype(I16) < cd))
                return jnp.where(cnt < need, cand, cut)

            cut = lax.fori_loop(0, 15, idx_step, jnp.zeros((TIE_ROWS, 1), jnp.int32))
            floor = jnp.logical_and(th.astype(jnp.int32) == I16_MIN, tl.astype(jnp.int32) == I16_MIN)
            cand_ref[rows, :] = jnp.where(floor, -1, jnp.broadcast_to(cut, (TIE_ROWS, LANES))).astype(I16)

        return carry

    lax.fori_loop(0, tq // TIE_ROWS, tie_group, 0)

    lane_idx = lax.broadcasted_iota(jnp.int32, (rb, tk), 1)

    def write_tile(c, carry):
        for r0 in range(0, tq, rb):
            hi = hi_ref[c, r0:r0 + rb, :]
            lo = lo_ref[c, r0:r0 + rb, :]
            th = jnp.concatenate([thr_hi16[r0:r0 + rb]] * (tk // LANES), axis=1)
            tl = jnp.concatenate([thr_lo16[r0:r0 + rb]] * (tk // LANES), axis=1)
            cut = jnp.concatenate([cand_ref[r0:r0 + rb, :]] * (tk // LANES), axis=1)
            gt = jnp.logical_or(hi > th, jnp.logical_and(hi == th, lo > tl))
            eq = jnp.logical_and(jnp.logical_and(hi == th, lo == tl),
                                 (c * tk + lane_idx).astype(I16) <= cut)
            keep = jnp.logical_or(gt, eq)
            bias_ref[c, r0:r0 + rb, :] = jnp.where(
                keep, jnp.zeros((), SOFTMAX_DTYPE), jnp.asarray(NEG_BIG, SOFTMAX_DTYPE))
        return carry

    lax.fori_loop(0, n_kt, write_tile, 0)

    def fill_tile(c, carry):
        bias_ref[c] = jnp.full((tq, tk), NEG_BIG, SOFTMAX_DTYPE)
        return carry

    lax.fori_loop(n_kt, n_tiles, fill_tile, 0)


def _dsa_select(qi, kia, kib, wi, topk, tq=256, tk=512, rb=32):
    L, nqi = qi.shape
    assert L // LANES < 2 ** 15 and L < 2 ** 15
    n_tiles = L // tk
    return pl.pallas_call(
        functools.partial(_dsa_select_kernel, tq=tq, tk=tk, topk=topk, rb=rb),
        grid=(L // tq,),
        in_specs=[
            pl.BlockSpec((tq, nqi), lambda i: (i, 0)),
            pl.BlockSpec((tq, LANES), lambda i: (i, 0)),
            _const_spec((L, LANES)),
            _const_spec((L, LANES)),
        ],
        out_specs=pl.BlockSpec((n_tiles, tq, tk), lambda i: (0, i, 0)),
        out_shape=jax.ShapeDtypeStruct((n_tiles, L, tk), SOFTMAX_DTYPE),
        scratch_shapes=[
            pltpu.VMEM((n_tiles, tq, tk), I16),
            pltpu.VMEM((n_tiles, tq, tk), I16),
            pltpu.VMEM((tq, LANES), I16),
            pltpu.VMEM((tq, LANES), I16),
            pltpu.VMEM((tq, LANES), I16),
            pltpu.VMEM((tq, LANES), F32),
        ],
        compiler_params=_params("arbitrary"),
        name="dsa_select",
    )(qi, wi, kia, kib)


def _dsa_attn_kernel(qi_s, kj_s, q_ref, k_ref, v_ref, bias_ref, o_ref, m_ref, l_ref, acc_ref,
                     *, tq, tk, nheads):
    step = pl.program_id(0)
    i = qi_s[step]
    j = kj_s[step]
    n_kt = ((i + 1) * tq + tk - 1) // tk

    @pl.when(j == 0)
    def _init():
        m_ref[...] = jnp.full_like(m_ref, NEG_BIG)
        l_ref[...] = jnp.zeros_like(l_ref)
        acc_ref[...] = jnp.zeros_like(acc_ref)

    bias = jnp.concatenate([bias_ref[t] for t in range(bias_ref.shape[0])], axis=1)
    _flash_heads(q_ref, k_ref, v_ref, m_ref, l_ref, acc_ref, bias, nheads)

    @pl.when(j == n_kt - 1)
    def _finish():
        _flash_finish(o_ref, l_ref, acc_ref, nheads)


def _dsa_attn(q, k, v, bias, tq=512, tk=1024):
    L, D = q.shape
    nheads = DSA_HEADS
    tb = bias.shape[2]
    assert D == nheads * LANES and bias.shape == (L // tb, L, tb) and tk % tb == 0
    qi_s, kj_s = _causal_steps(L // tq, tq, tk)
    qmap = lambda s, qs, ks: (qs[s], 0)
    kmap = lambda s, qs, ks: (ks[s], 0)
    grid_spec = pltpu.PrefetchScalarGridSpec(
        num_scalar_prefetch=2,
        grid=(qi_s.shape[0],),
        in_specs=[
            pl.BlockSpec((tq, D), qmap),
            pl.BlockSpec((tk, D), kmap),
            pl.BlockSpec((tk, D), kmap),
            pl.BlockSpec((tk // tb, tq, tb), lambda s, qs, ks: (ks[s], qs[s], 0)),
        ],
        out_specs=pl.BlockSpec((tq, D), qmap),
        scratch_shapes=[
            pltpu.VMEM((nheads, tq, LANES), F32),
            pltpu.VMEM((nheads, tq, LANES), F32),
            pltpu.VMEM((tq, D), F32),
        ],
    )
    return pl.pallas_call(
        functools.partial(_dsa_attn_kernel, tq=tq, tk=tk, nheads=nheads),
        grid_spec=grid_spec,
        out_shape=jax.ShapeDtypeStruct((L, D), MXU_DTYPE),
        compiler_params=_params("arbitrary"),
        name="dsa_attn",
    )(qi_s, kj_s, q, k, v, bias)


def _dsa_layer(h, g, w_in, idx_k_g, idx_k_b, w_out):
    L, D = h.shape
    nqi = IDX_HEADS * IDX_DIM
    o3 = 3 * D
    o4 = o3 + nqi
    o5 = o4 + IDX_DIM
    w_ki = w_in[:, o4:o5]
    w_wi = w_in[:, o5:]
    zpad = lambda n: jnp.zeros((D, n), w_in.dtype)
    w_cat = jnp.concatenate(
        [w_in[:, :o4],
         w_ki, w_wi, zpad(LANES - IDX_DIM - IDX_HEADS),
         w_wi, zpad(LANES - IDX_DIM - IDX_HEADS), w_ki], axis=1).astype(MXU_DTYPE)
    z = jnp.zeros((IDX_DIM,), F32)
    lg2 = jnp.stack([jnp.concatenate([idx_k_g, z]), jnp.concatenate([z, idx_k_g])])
    lb2 = jnp.stack([jnp.concatenate([idx_k_b, z]), jnp.concatenate([z, idx_k_b])])
    q, k, v, qi, kia, kib, wi = _dsa_proj(h, g, w_cat, lg2, lb2)
    bias = _dsa_select(qi, kia, kib, wi, topk=min(TOPK_MAX, L // 4))
    o = _dsa_attn(q, k, v, bias)
    return _out_proj(o, w_out.astype(MXU_DTYPE), jnp.zeros((D,), F32), h)


def _glu_proj_kernel(h_ref, g_ref, w_ref, b_ref, u_ref, *, d):
    xn = _rmsnorm_rows(h_ref[...], g_ref[...]).astype(MXU_DTYPE)
    cw = 512
    for c in range(0, d, cw):
        a = jnp.dot(xn, w_ref[:, c:c + cw], preferred_element_type=F32) + b_ref[:, c:c + cw]
        gt = jnp.dot(xn, w_ref[:, d + c:d + c + cw], preferred_element_type=F32) + b_ref[:, d + c:d + c + cw]
        u_ref[:, c:c + cw] = a * _sigmoid(gt)


def _glu_proj(h, g, w, b, tm=512):
    L, D = h.shape
    return pl.pallas_call(
        functools.partial(_glu_proj_kernel, d=D),
        grid=(L // tm,),
        in_specs=[pl.BlockSpec((tm, D), lambda i: (i, 0)), _const_spec((1, D)),
                  _const_spec((D, 2 * D)), _const_spec((1, 2 * D))],
        out_specs=pl.BlockSpec((tm, D), lambda i: (i, 0)),
        out_shape=jax.ShapeDtypeStruct((L, D), F32),
        compiler_params=_params("arbitrary"),
        name="conv_glu_proj",
    )(h, g.reshape(1, D), w, b.reshape(1, 2 * D))


CONV_HALO = 32


def _conv_out_kernel(u_ref, up_ref, wdw_ref, bdw_ref, lg_ref, lb_ref, w2_ref, b2_ref, h_ref, o_ref,
                     x_ref, xs_ref, *, tm):
    i = pl.program_id(0)
    x_ref[CONV_HALO:, :] = u_ref[...]
    x_ref[:CONV_HALO, :] = jnp.where(i > 0, up_ref[...], 0.0)
    off = CONV_HALO - (CONV_WIDTH - 1)
    acc = jnp.zeros(u_ref.shape, F32)
    for r in range(SUBLANES):
        taps = [t for t in range(CONV_WIDTH) if (off + t) % SUBLANES == r]
        if not taps:
            continue
        span = off + taps[-1] - r + tm
        src = x_ref
        if r:
            xs_ref[:span, :] = x_ref[r:r + span, :]
            src = xs_ref
        for t in taps:
            a = off + t - r
            acc = acc + wdw_ref[t:t + 1, :] * src[a:a + tm, :]
    y = acc + bdw_ref[...]
    mu = jnp.mean(y, axis=-1, keepdims=True)
    dlt = y - mu
    var = jnp.mean(dlt * dlt, axis=-1, keepdims=True)
    y = _silu(dlt * lax.rsqrt(var + EPS) * lg_ref[...] + lb_ref[...])
    o_ref[...] = h_ref[...] + _dot(y, w2_ref[...]) + b2_ref[...]


def _conv_out(u, w_dw, b_dw, ln_g, ln_b, w2, b2, h, tm=512):
    L, D = u.shape
    r = tm // CONV_HALO
    vec = lambda a: a.reshape(1, D)
    return pl.pallas_call(
        functools.partial(_conv_out_kernel, tm=tm),
        grid=(L // tm,),
        in_specs=[
            pl.BlockSpec((tm, D), lambda i: (i, 0)),
            pl.BlockSpec((CONV_HALO, D), lambda i: (jnp.maximum(i * r - 1, 0), 0)),
            _const_spec((CONV_WIDTH, D)),
            _const_spec((1, D)), _const_spec((1, D)), _const_spec((1, D)),
            _const_spec((D, D)), _const_spec((1, D)),
            pl.BlockSpec((tm, D), lambda i: (i, 0)),
        ],
        out_specs=pl.BlockSpec((tm, D), lambda i: (i, 0)),
        out_shape=jax.ShapeDtypeStruct((L, D), F32),
        scratch_shapes=[pltpu.VMEM((tm + CONV_HALO, D), F32), pltpu.VMEM((tm + CONV_HALO, D), F32)],
        compiler_params=_params("arbitrary"),
        name="conv_out",
    )(u, u, w_dw, vec(b_dw), vec(ln_g), vec(ln_b), w2, vec(b2), h)


def _conv_layer(h, g, w_pw1, b_pw1, w_dw, b_dw, ln_g, ln_b, w_pw2, b_pw2):
    u = _glu_proj(h, g, w_pw1.astype(MXU_DTYPE), b_pw1)
    return _conv_out(u, w_dw, b_dw, ln_g, ln_b, w_pw2.astype(MXU_DTYPE), b_pw2, h)


def _mla_prep_kernel(h_ref, g_ref, w1_ref, qg_ref, kvg_ref, wqa_ref, wqb_ref, wka_ref, wv_ref,
                     c_ref, s_ref, q_ref, k_ref, v_ref, *, scale):
    xn = _rmsnorm_rows(h_ref[...], g_ref[...]).astype(MXU_DTYPE)
    proj = jnp.dot(xn, w1_ref[...], preferred_element_type=F32)
    o1 = MLA_Q_LORA
    o2 = o1 + MLA_KV_LORA
    cq = _rmsnorm_rows(proj[:, :o1], qg_ref[...]).astype(MXU_DTYPE)
    ckv = _rmsnorm_rows(proj[:, o1:o2], kvg_ref[...]).astype(MXU_DTYPE)
    cos = c_ref[...]
    sin = s_ref[...]
    kr = proj[:, o2:o2 + LANES] * cos + proj[:, o2 + LANES:o2 + 2 * LANES] * sin
    for hd in range(MLA_HEADS):
        cs = slice(hd * LANES, (hd + 1) * LANES)
        qa = jnp.dot(cq, wqa_ref[:, cs], preferred_element_type=F32)
        qb = jnp.dot(cq, wqb_ref[:, cs], preferred_element_type=F32)
        q_ref[:, cs] = ((qa * cos + qb * sin) * scale).astype(q_ref.dtype)
        ka = jnp.dot(ckv, wka_ref[:, cs], preferred_element_type=F32)
        k_ref[:, cs] = (ka + kr).astype(k_ref.dtype)
        va = jnp.dot(ckv, wv_ref[:, cs], preferred_element_type=F32)
        lane = lax.broadcasted_iota(jnp.int32, va.shape, 1)
        v_ref[:, cs] = jnp.where(lane < MLA_V, va, 1.0).astype(v_ref.dtype)


def _mla_prep(h, g, w1, qg, kvg, wqa, wqb, wka, wv, ctab, stab, tm=512):
    L, D = h.shape
    HP = MLA_HEADS * LANES
    row = lambda n: pl.BlockSpec((tm, n), lambda i: (i, 0))
    full = lambda a: _const_spec(a.shape)
    return pl.pallas_call(
        functools.partial(_mla_prep_kernel, scale=(MLA_NOPE + MLA_ROPE) ** -0.5 * LOG2E),
        grid=(L // tm,),
        in_specs=[row(D), _const_spec((1, D)), full(w1), _const_spec((1, MLA_Q_LORA)),
                  _const_spec((1, MLA_KV_LORA)), full(wqa), full(wqb), full(wka), full(wv),
                  row(LANES), row(LANES)],
        out_specs=[row(HP), row(HP), row(HP)],
        out_shape=[jax.ShapeDtypeStruct((L, HP), MXU_DTYPE)] * 3,
        compiler_params=_params("arbitrary"),
        name="mla_prep",
    )(h, g.reshape(1, D), w1, qg.reshape(1, -1), kvg.reshape(1, -1), wqa, wqb, wka, wv, ctab, stab)


def _mla_attn_kernel(qi_s, kj_s, q_ref, k_ref, v_ref, o_ref, m_ref, acc_ref, *, tq, tk, nheads):
    step = pl.program_id(0)
    i = qi_s[step]
    j = kj_s[step]
    n_kt = ((i + 1) * tq + tk - 1) // tk

    @pl.when(j == 0)
    def _init():
        m_ref[...] = jnp.full_like(m_ref, NEG_BIG)
        acc_ref[...] = jnp.zeros_like(acc_ref)

    qpos = i * tq + lax.broadcasted_iota(jnp.int32, (tq, tk), 0)
    kpos = j * tk + lax.broadcasted_iota(jnp.int32, (tq, tk), 1)
    bias = jnp.where(kpos <= qpos, 0.0, NEG_BIG).astype(SOFTMAX_DTYPE)
    _flash_heads(q_ref, k_ref, v_ref, m_ref, None, acc_ref, bias, nheads)

    @pl.when(j == n_kt - 1)
    def _finish():
        _flash_finish(o_ref, None, acc_ref, nheads)


def _mla_attn(q, k, v, tq=512, tk=1024):
    L, HP = q.shape
    qi_s, kj_s = _causal_steps(L // tq, tq, tk)
    qmap = lambda s, qs, ks: (qs[s], 0)
    kmap = lambda s, qs, ks: (ks[s], 0)
    grid_spec = pltpu.PrefetchScalarGridSpec(
        num_scalar_prefetch=2,
        grid=(qi_s.shape[0],),
        in_specs=[pl.BlockSpec((tq, HP), qmap),
                  pl.BlockSpec((tk, HP), kmap), pl.BlockSpec((tk, HP), kmap)],
        out_specs=pl.BlockSpec((tq, HP), qmap),
        scratch_shapes=[pltpu.VMEM((MLA_HEADS, tq, LANES), F32),
                        pltpu.VMEM((tq, HP), F32)],
    )
    return pl.pallas_call(
        functools.partial(_mla_attn_kernel, tq=tq, tk=tk, nheads=MLA_HEADS),
        grid_spec=grid_spec,
        out_shape=jax.ShapeDtypeStruct((L, HP), MXU_DTYPE),
        compiler_params=_params("arbitrary"),
        name="mla_attn",
    )(qi_s, kj_s, q, k, v)


def _place_heads(w, nheads, src_w, lane0, swap_half=0):
    K = w.shape[0]
    w3 = w.reshape(K, nheads, src_w)
    if swap_half:
        w3 = jnp.concatenate([w3[..., swap_half:], w3[..., :swap_half]], axis=-1)
    out = jnp.zeros((K, nheads, LANES), w.dtype)
    out = out.at[:, :, lane0:lane0 + src_w].set(w3)
    return out.reshape(K, nheads * LANES)


def _mla_layer(h, g, w_in, q_norm_g, w_uq, kv_norm_g, w_ukv, w_out):
    L, D = h.shape
    half = MLA_ROPE // 2
    o2 = MLA_Q_LORA + MLA_KV_LORA
    w_kr = w_in[:, o2:]
    w1 = jnp.concatenate(
        [w_in[:, :o2], _place_heads(w_kr, 1, MLA_ROPE, MLA_NOPE),
         _place_heads(w_kr, 1, MLA_ROPE, MLA_NOPE, swap_half=half)], axis=1).astype(MXU_DTYPE)
    uq = w_uq.reshape(MLA_Q_LORA, MLA_HEADS, MLA_NOPE + MLA_ROPE)
    uq_rope = uq[..., MLA_NOPE:].reshape(MLA_Q_LORA, MLA_HEADS * MLA_ROPE)
    wqa = _place_heads(w_uq, MLA_HEADS, MLA_NOPE + MLA_ROPE, 0).astype(MXU_DTYPE)
    wqb = _place_heads(uq_rope, MLA_HEADS, MLA_ROPE, MLA_NOPE, swap_half=half).astype(MXU_DTYPE)
    ukv = w_ukv.reshape(MLA_KV_LORA, MLA_HEADS, MLA_NOPE + MLA_V)
    wka = _place_heads(ukv[..., :MLA_NOPE].reshape(MLA_KV_LORA, -1), MLA_HEADS, MLA_NOPE, 0).astype(MXU_DTYPE)
    wv = _place_heads(ukv[..., MLA_NOPE:].reshape(MLA_KV_LORA, -1), MLA_HEADS, MLA_V, 0).astype(MXU_DTYPE)
    pos = jnp.arange(L, dtype=F32)
    inv_freq = ROPE_THETA ** (-jnp.arange(0, MLA_ROPE, 2, dtype=F32) / MLA_ROPE)
    ang = pos[:, None] * inv_freq[None, :]
    cos, sin = jnp.cos(ang), jnp.sin(ang)
    zeros = jnp.zeros((L, LANES - MLA_NOPE - MLA_ROPE), F32)
    ctab = jnp.concatenate([jnp.ones((L, MLA_NOPE), F32), cos, cos, zeros], axis=1)
    stab = jnp.concatenate([jnp.zeros((L, MLA_NOPE), F32), -sin, sin, zeros], axis=1)
    q, k, v = _mla_prep(h, g, w1, q_norm_g, kv_norm_g, wqa, wqb, wka, wv, ctab, stab)
    o = _mla_attn(q, k, v)
    wo = w_out.reshape(MLA_HEADS, MLA_V, D)
    wo_pad = jnp.concatenate([wo, jnp.zeros_like(wo)], axis=1).reshape(MLA_HEADS * LANES, D).astype(MXU_DTYPE)
    return _out_proj(o, wo_pad, jnp.zeros((D,), F32), h)


def _gdn_scan_kernel(h_ref, g_ref, w_ref, cw_ref, alog_ref, dtb_ref, o_ref, gate_ref, x_ref, s_ref,
                     *, nheads, dk, dv, c, nch):
    n = pl.program_id(0)
    halo = 8
    nqk = nheads * dk

    @pl.when(n == 0)
    def _init():
        x_ref[:halo, :] = jnp.zeros((halo, x_ref.shape[1]), F32)
        s_ref[...] = jnp.zeros_like(s_ref)

    rows = nch * c
    nqkv = x_ref.shape[1]
    nv = nheads * dv
    xn = _rmsnorm_rows(h_ref[...], g_ref[...]).astype(MXU_DTYPE)
    cw_cols = 512
    for c0 in range(0, nqkv, cw_cols):
        x_ref[halo:, c0:c0 + cw_cols] = jnp.dot(xn, w_ref[:, c0:c0 + cw_cols], preferred_element_type=F32)
    gate_ref[...] = jnp.dot(xn, w_ref[:, nqkv:nqkv + nv], preferred_element_type=F32)
    ba = jnp.dot(xn, w_ref[:, nqkv + nv:], preferred_element_type=F32)
    off = halo - (GDN_CONV - 1)
    acc = jnp.zeros((rows, nqkv), F32)
    for t in range(GDN_CONV):
        acc = acc + cw_ref[t:t + 1, :] * x_ref[off + t:off + t + rows, :]
    x_ref[:halo, :] = x_ref[rows:rows + halo, :]
    qkv = _silu(acc)

    beta_all = _sigmoid(ba[:, :LANES])
    a_raw = ba[:, LANES:] + dtb_ref[...]
    softplus = jnp.maximum(a_raw, 0.0) + jnp.log(1.0 + jnp.exp(-jnp.abs(a_raw)))
    g_all = -jnp.exp(alog_ref[...]) * softplus
    row = lax.broadcasted_iota(jnp.int32, (c, c), 0)
    col = lax.broadcasted_iota(jnp.int32, (c, c), 1)
    tril = row >= col
    tril_strict = row > col
    eye = jnp.where(row == col, 1.0, 0.0)

    units = [(ch, hd) for ch in range(nch) for hd in range(nheads)]
    gc_all = [jnp.dot(tril.astype(F32), g_all[ch * c:(ch + 1) * c], preferred_element_type=F32,
                      precision=lax.Precision.HIGHEST) for ch in range(nch)]
    gc_t = [x.T for x in gc_all]

    def chunk_rows(x, ch):
        return x[ch * c:(ch + 1) * c]

    q = [chunk_rows(qkv[:, hd * dk:(hd + 1) * dk], ch) for ch, hd in units]
    k = [chunk_rows(qkv[:, nqk + hd * dk:nqk + (hd + 1) * dk], ch) for ch, hd in units]
    v = [chunk_rows(qkv[:, 2 * nqk + hd * dv:2 * nqk + (hd + 1) * dv], ch) for ch, hd in units]
    q = [x * lax.rsqrt(jnp.sum(x * x, axis=-1, keepdims=True) + EPS) * (dk ** -0.5) for x in q]
    k = [x * lax.rsqrt(jnp.sum(x * x, axis=-1, keepdims=True) + EPS) for x in k]
    beta = [chunk_rows(beta_all[:, hd:hd + 1], ch) for ch, hd in units]
    gc = [gc_all[ch][:, hd:hd + 1] for ch, hd in units]
    gc_last = [gc_all[ch][c - 1:c, hd:hd + 1] for ch, hd in units]
    n_u = range(len(units))
    decay = [jnp.where(tril, jnp.exp(jnp.where(tril, gc[u] - gc_t[units[u][0]][units[u][1]:units[u][1] + 1, :],
                                               0.0)), 0.0) for u in n_u]
    kk = [_dot_nt(k[u], k[u]) for u in n_u]
    qk = [_dot_nt(q[u], k[u]) for u in n_u]
    a_mat = [jnp.where(tril_strict, beta[u] * kk[u] * decay[u], 0.0) for u in n_u]
    t_inv = [eye - jnp.where((row >> 1) == (col >> 1), a_mat[u], 0.0) for u in n_u]
    for lvl in range(1, int(math.log2(c))):
        blk_mask = jnp.logical_and((row >> (lvl + 1)) == (col >> (lvl + 1)), (row >> lvl) != (col >> lvl))
        tm = [_dot(t_inv[u], jnp.where(blk_mask, a_mat[u], 0.0)) for u in n_u]
        tmt = [_dot(tm[u], t_inv[u]) for u in n_u]
        t_inv = [t_inv[u] - tmt[u] for u in n_u]
    e_gc = [jnp.exp(gc[u]) for u in n_u]
    uu = [_dot(t_inv[u], v[u] * beta[u]) for u in n_u]
    w = [_dot(t_inv[u], k[u] * (beta[u] * e_gc[u])) for u in n_u]
    q_dec = [q[u] * e_gc[u] for u in n_u]
    k_dec_t = [(k[u] * jnp.exp(gc_last[u] - gc[u])).T for u in n_u]
    qk_dec = [qk[u] * decay[u] for u in n_u]
    g_last = [jnp.exp(gc_last[u]) for u in n_u]

    s = [s_ref[hd] for hd in range(nheads)]
    for ch in range(nch):
        us = [ch * nheads + hd for hd in range(nheads)]
        ws = [_dot(w[u], s[hd]) for hd, u in enumerate(us)]
        qs = [_dot(q_dec[u], s[hd]) for hd, u in enumerate(us)]
        v_new = [uu[u] - ws[hd] for hd, u in enumerate(us)]
        o_intra = [_dot(qk_dec[u], v_new[hd]) for hd, u in enumerate(us)]
        s_upd = [_dot(k_dec_t[u], v_new[hd]) for hd, u in enumerate(us)]
        for hd, u in enumerate(us):
            o_ref[ch * c:(ch + 1) * c, hd * dv:(hd + 1) * dv] = qs[hd] + o_intra[hd]
        s = [s[hd] * g_last[u] + s_upd[hd] for hd, u in enumerate(us)]
    for hd in range(nheads):
        s_ref[hd] = s[hd]


def _gdn_scan(h, g, w_cat, conv_w, a_log_pad, dt_bias_pad, nch=4):
    L, D = h.shape
    c = GDN_CHUNK
    rows = nch * c
    nqkv = 2 * GDN_HEADS * GDN_DK + GDN_HEADS * GDN_DV
    nv = GDN_HEADS * GDN_DV
    assert w_cat.shape == (D, nqkv + nv + 2 * LANES)
    return pl.pallas_call(
        functools.partial(_gdn_scan_kernel, nheads=GDN_HEADS, dk=GDN_DK, dv=GDN_DV, c=c, nch=nch),
        grid=(L // rows,),
        in_specs=[pl.BlockSpec((rows, D), lambda n: (n, 0)), _const_spec((1, D)),
                  _const_spec(w_cat.shape),
                  _const_spec((GDN_CONV, nqkv)), _const_spec((1, LANES)), _const_spec((1, LANES))],
        out_specs=[pl.BlockSpec((rows, nv), lambda n: (n, 0)), pl.BlockSpec((rows, nv), lambda n: (n, 0))],
        out_shape=[jax.ShapeDtypeStruct((L, nv), F32), jax.ShapeDtypeStruct((L, nv), F32)],
        scratch_shapes=[pltpu.VMEM((rows + 8, nqkv), F32),
                        pltpu.VMEM((GDN_HEADS, GDN_DK, GDN_DV), F32)],
        compiler_params=_params("arbitrary"),
        name="gdn_scan",
    )(h, g.reshape(1, D), w_cat, conv_w, a_log_pad, dt_bias_pad)


def _gdn_out_kernel(o_ref, gate_ref, og_ref, w_ref, h_ref, y_ref, *, nheads, dv):
    parts = []
    for hd in range(nheads):
        cs = slice(hd * dv, (hd + 1) * dv)
        on = _rmsnorm_rows(o_ref[:, cs], og_ref[...])
        parts.append((on * _silu(gate_ref[:, cs])).astype(MXU_DTYPE))
    y_ref[...] = h_ref[...] + jnp.dot(jnp.concatenate(parts, axis=1), w_ref[...],
                                      preferred_element_type=F32)


def _gdn_out(o, gate, o_norm_g, w_out, h, tm=512):
    L, D = h.shape
    nv = GDN_HEADS * GDN_DV
    return pl.pallas_call(
        functools.partial(_gdn_out_kernel, nheads=GDN_HEADS, dv=GDN_DV),
        grid=(L // tm,),
        in_specs=[pl.BlockSpec((tm, nv), lambda i: (i, 0)),
                  pl.BlockSpec((tm, nv), lambda i: (i, 0)),
                  _const_spec((1, GDN_DV)), _const_spec((nv, D)),
                  pl.BlockSpec((tm, D), lambda i: (i, 0))],
        out_specs=pl.BlockSpec((tm, D), lambda i: (i, 0)),
        out_shape=jax.ShapeDtypeStruct((L, D), F32),
        compiler_params=_params("arbitrary"),
        name="gdn_out",
    )(o, gate, o_norm_g.reshape(1, GDN_DV), w_out, h)


def _gdn_layer(h, g, w_in, conv_w, a_log, dt_bias, o_norm_g, w_out):
    L, D = h.shape
    nmain = 2 * GDN_HEADS * GDN_DK + 2 * GDN_HEADS * GDN_DV
    zpad = jnp.zeros((D, LANES - GDN_HEADS), w_in.dtype)
    w_cat = jnp.concatenate([w_in[:, :nmain], w_in[:, nmain:nmain + GDN_HEADS], zpad,
                             w_in[:, nmain + GDN_HEADS:], zpad], axis=1).astype(MXU_DTYPE)
    vpad = lambda a: jnp.concatenate([a.astype(F32), jnp.zeros((LANES - GDN_HEADS,), F32)]).reshape(1, LANES)
    o, gate = _gdn_scan(h, g, w_cat, conv_w, vpad(a_log), vpad(dt_bias))
    return _gdn_out(o, gate, o_norm_g, w_out.astype(MXU_DTYPE), h)


def kernel(x, norm_mix_g, norm_mlp_g, final_g, mlp_w1, mlp_w2, dsa_w_in, dsa_idx_k_g, dsa_idx_k_b, dsa_w_out, conv_w_pw1, conv_b_pw1, conv_w_dw, conv_b_dw, conv_ln_g, conv_ln_b, conv_w_pw2, conv_b_pw2, mla_w_in, mla_q_norm_g, mla_w_uq, mla_kv_norm_g, mla_w_ukv, mla_w_out, gdn_w_in, gdn_conv_w, gdn_a_log, gdn_dt_bias, gdn_o_norm_g, gdn_w_out):
    b, L, D = x.shape
    depth = norm_mix_g.shape[0]
    outs = []
    rows = x.reshape(b * L, D)
    for bi in range(b):
        h = rows if b == 1 else rows[bi * L:(bi + 1) * L]
        for i in range(depth):
            m = i % 4
            jl = i // 4
            g = norm_mix_g[i]
            if m == 0:
                h = _dsa_layer(h, g, dsa_w_in[jl], dsa_idx_k_g[jl], dsa_idx_k_b[jl], dsa_w_out[jl])
            elif m == 1:
                h = _conv_layer(h, g, conv_w_pw1[jl], conv_b_pw1[jl], conv_w_dw[jl], conv_b_dw[jl],
                                conv_ln_g[jl], conv_ln_b[jl], conv_w_pw2[jl], conv_b_pw2[jl])
            elif m == 2:
                h = _mla_layer(h, g, mla_w_in[jl], mla_q_norm_g[jl], mla_w_uq[jl], mla_kv_norm_g[jl],
                               mla_w_ukv[jl], mla_w_out[jl])
            else:
                h = _gdn_layer(h, g, gdn_w_in[jl], gdn_conv_w[jl], gdn_a_log[jl], gdn_dt_bias[jl],
                               gdn_o_norm_g[jl], gdn_w_out[jl])
            h = _mlp(h, norm_mlp_g[i], mlp_w1[i].astype(MXU_DTYPE), mlp_w2[i].astype(MXU_DTYPE),
                     final_g, final_norm=(i == depth - 1))
        outs.append(h)
    return outs[0].reshape(1, L, D) if b == 1 else jnp.stack(outs)
```

```python
import functools
import math

import jax
import jax.numpy as jnp
import numpy as np
from jax import lax
from jax.experimental import pallas as pl
from jax.experimental.pallas import tpu as pltpu

F32 = jnp.float32
MXU_DTYPE = jnp.bfloat16
EPS = 1e-6
LANES = 128
SUBLANES = 8
VMEM_LIMIT_BYTES = 56 * 1024 * 1024
SOFTMAX_DTYPE = jnp.bfloat16
NEG_BIG = -(2.0 ** 100)
INT_MIN = -(2 ** 31)

IDX_HEADS = 8
IDX_DIM = 64
TOPK_MAX = 256
DSA_HEADS = 8
CONV_WIDTH = 31
MLA_HEADS = 16
MLA_Q_LORA = 384
MLA_KV_LORA = 256
MLA_NOPE = 64
MLA_ROPE = 32
MLA_V = 64
ROPE_THETA = 10000.0
GDN_HEADS = 8
GDN_DK = 128
GDN_DV = 128
GDN_CONV = 4
GDN_CHUNK = 64


def _params(*sem):
    return pltpu.CompilerParams(dimension_semantics=sem, vmem_limit_bytes=VMEM_LIMIT_BYTES)


def _dot(a, b):
    return jnp.dot(a.astype(MXU_DTYPE), b.astype(MXU_DTYPE), preferred_element_type=F32)


def _dot_nt(a, b):
    return lax.dot_general(a.astype(MXU_DTYPE), b.astype(MXU_DTYPE),
                           (((1,), (1,)), ((), ())), preferred_element_type=F32)


def _rmsnorm_rows(x, g):
    return x * lax.rsqrt(jnp.mean(x * x, axis=-1, keepdims=True) + EPS) * g


def _sigmoid(x):
    return 1.0 / (1.0 + jnp.exp(-x))


def _silu(x):
    return x * _sigmoid(x)


def _const_spec(shape):
    return pl.BlockSpec(shape, lambda *_: (0,) * len(shape))


LOG2E = math.log2(math.e)
LOGITS_AHEAD = 2


def _causal_steps(n_q, tq, tk):
    qi, kj = [], []
    for i in range(n_q):
        for j in range(((i + 1) * tq - 1) // tk + 1):
            qi.append(i)
            kj.append(j)
    return jnp.asarray(np.array(qi, np.int32)), jnp.asarray(np.array(kj, np.int32))


def _flash_heads(q_ref, k_ref, v_ref, m_ref, l_ref, acc_ref, bias, nheads):
    ones = None if l_ref is None else jnp.ones((k_ref.shape[0], LANES), MXU_DTYPE)

    def logits(h):
        cs = slice(h * LANES, (h + 1) * LANES)
        return _dot_nt(q_ref[:, cs], k_ref[:, cs])

    ahead = [logits(h) for h in range(min(LOGITS_AHEAD, nheads))]
    for h in range(nheads):
        cs = slice(h * LANES, (h + 1) * LANES)
        s = ahead.pop(0)
        if h + LOGITS_AHEAD < nheads:
            ahead.append(logits(h + LOGITS_AHEAD))
        s = s.astype(SOFTMAX_DTYPE) + bias
        m_prev = m_ref[h]
        m_next = jnp.maximum(m_prev, jnp.max(s, axis=1, keepdims=True).astype(F32))
        alpha = jnp.exp2(m_prev - m_next)
        p = jnp.exp2(s - m_next[:, 0:1].astype(SOFTMAX_DTYPE)).astype(MXU_DTYPE)
        m_ref[h] = m_next
        if l_ref is None:
            acc_ref[:, cs] = alpha * acc_ref[:, cs] + jnp.dot(p, v_ref[:, cs], preferred_element_type=F32)
        else:
            v_ext = jnp.concatenate([v_ref[:, cs].astype(MXU_DTYPE), ones], axis=1)
            pv = jnp.dot(p, v_ext, preferred_element_type=F32)
            l_ref[h] = alpha * l_ref[h] + pv[:, LANES:]
            acc_ref[:, cs] = alpha * acc_ref[:, cs] + pv[:, :LANES]


def _flash_finish(o_ref, l_ref, acc_ref, nheads):
    half = LANES // 2
    for h in range(nheads):
        cs = slice(h * LANES, (h + 1) * LANES)
        acc = acc_ref[:, cs]
        if l_ref is None:
            lane = lax.broadcasted_iota(jnp.int32, acc.shape, 1)
            out = jnp.where(lane < half, acc / pltpu.roll(acc, half, axis=1), 0.0)
        else:
            out = acc / l_ref[h]
        o_ref[:, cs] = out.astype(o_ref.dtype)


def _mlp_kernel(h_ref, g_ref, w1_ref, w2_ref, gf_ref, o_ref, xn_ref, acc_ref, *, final_norm):
    f = pl.program_id(1)

    @pl.when(f == 0)
    def _():
        xn_ref[...] = _rmsnorm_rows(h_ref[...], g_ref[...]).astype(xn_ref.dtype)
        acc_ref[...] = jnp.zeros_like(acc_ref)

    a = _dot(xn_ref[...], w1_ref[...])
    a = jnp.square(jnp.maximum(a, 0.0))
    acc_ref[...] += _dot(a, w2_ref[...])

    @pl.when(f == pl.num_programs(1) - 1)
    def _():
        y = h_ref[...] + acc_ref[...]
        if final_norm:
            y = _rmsnorm_rows(y, gf_ref[...])
        o_ref[...] = y


def _mlp(h, g, w1, w2, gf, final_norm, tm=1024, tf=2048):
    L, D = h.shape
    dff = w1.shape[1]
    return pl.pallas_call(
        functools.partial(_mlp_kernel, final_norm=final_norm),
        grid=(L // tm, dff // tf),
        in_specs=[
            pl.BlockSpec((tm, D), lambda i, f: (i, 0)),
            _const_spec((1, D)),
            pl.BlockSpec((D, tf), lambda i, f: (0, f)),
            pl.BlockSpec((tf, D), lambda i, f: (f, 0)),
            _const_spec((1, D)),
        ],
        out_specs=pl.BlockSpec((tm, D), lambda i, f: (i, 0)),
        out_shape=jax.ShapeDtypeStruct((L, D), F32),
        scratch_shapes=[pltpu.VMEM((tm, D), MXU_DTYPE), pltpu.VMEM((tm, D), F32)],
        compiler_params=_params("arbitrary", "arbitrary"),
        name="mlp",
    )(h, g.reshape(1, D), w1, w2, gf.reshape(1, D))


def _out_proj_kernel(a_ref, w_ref, b_ref, h_ref, o_ref):
    o_ref[...] = h_ref[...] + _dot(a_ref[...], w_ref[...]) + b_ref[...]


def _out_proj(a, w, b, h, tm=512):
    L, K = a.shape
    D = w.shape[1]
    return pl.pallas_call(
        _out_proj_kernel,
        grid=(L // tm,),
        in_specs=[
            pl.BlockSpec((tm, K), lambda i: (i, 0)),
            _const_spec((K, D)),
            _const_spec((1, D)),
            pl.BlockSpec((tm, D), lambda i: (i, 0)),
        ],
        out_specs=pl.BlockSpec((tm, D), lambda i: (i, 0)),
        out_shape=jax.ShapeDtypeStruct((L, D), F32),
        compiler_params=_params("arbitrary"),
        name="out_proj",
    )(a, w, b.reshape(1, D), h)


def _dsa_proj_kernel(h_ref, g_ref, w_ref, lg_ref, lb_ref,
                     q_ref, k_ref, v_ref, qi_ref, kia_ref, kib_ref, wi_ref, *, d, scale, wi_scale):
    xn = _rmsnorm_rows(h_ref[...], g_ref[...]).astype(MXU_DTYPE)

    def mm(c0, n):
        return jnp.dot(xn, w_ref[:, c0:c0 + n], preferred_element_type=F32)

    cw = 512
    for c in range(0, d, cw):
        q_ref[:, c:c + cw] = (mm(c, cw) * scale).astype(q_ref.dtype)
        k_ref[:, c:c + cw] = mm(d + c, cw).astype(k_ref.dtype)
        v_ref[:, c:c + cw] = mm(2 * d + c, cw).astype(v_ref.dtype)
    nqi = IDX_HEADS * IDX_DIM
    qi_ref[...] = mm(3 * d, nqi).astype(qi_ref.dtype)
    xa = mm(3 * d + nqi, LANES)
    xb = mm(3 * d + nqi + LANES, LANES)
    lane = lax.broadcasted_iota(jnp.int32, xa.shape, 1)

    def masked_ln(x, m, gain, bias):
        mu = jnp.sum(jnp.where(m, x, 0.0), axis=-1, keepdims=True) * (1.0 / IDX_DIM)
        dlt = jnp.where(m, x - mu, 0.0)
        var = jnp.sum(dlt * dlt, axis=-1, keepdims=True) * (1.0 / IDX_DIM)
        return dlt * lax.rsqrt(var + EPS) * gain + bias

    kia_ref[...] = masked_ln(xa, lane < IDX_DIM, lg_ref[0:1, :], lb_ref[0:1, :]).astype(kia_ref.dtype)
    kib_ref[...] = masked_ln(xb, lane >= IDX_DIM, lg_ref[1:2, :], lb_ref[1:2, :]).astype(kib_ref.dtype)
    wi_ref[...] = jnp.where(lane < IDX_HEADS, xb, 0.0) * wi_scale


def _dsa_proj(h, g, w_cat, lg2, lb2, tm=1024):
    L, D = h.shape
    N = w_cat.shape[1]
    nqi = IDX_HEADS * IDX_DIM
    row = lambda n: pl.BlockSpec((tm, n), lambda i: (i, 0))
    return pl.pallas_call(
        functools.partial(_dsa_proj_kernel, d=D, scale=(D // DSA_HEADS) ** -0.5 * LOG2E,
                          wi_scale=IDX_HEADS ** -0.5 * IDX_DIM ** -0.5),
        grid=(L // tm,),
        in_specs=[row(D), _const_spec((1, D)), _const_spec((D, N)),
                  _const_spec((2, LANES)), _const_spec((2, LANES))],
        out_specs=[row(D), row(D), row(D), row(nqi), row(LANES), row(LANES), row(LANES)],
        out_shape=[jax.ShapeDtypeStruct((L, D), MXU_DTYPE)] * 3
        + [jax.ShapeDtypeStruct((L, nqi), MXU_DTYPE)]
        + [jax.ShapeDtypeStruct((L, LANES), MXU_DTYPE)] * 2
        + [jax.ShapeDtypeStruct((L, LANES), F32)],
        compiler_params=_params("arbitrary"),
        name="dsa_proj",
    )(h, g.reshape(1, D), w_cat, lg2, lb2)


I16 = jnp.int16
I16_MIN = -(2 ** 15)
TIE_ROWS = 16


def _dsa_select_kernel(qi_ref, wi_ref, kia_ref, kib_ref, bias_ref, hi_ref, lo_ref, cand_ref,
                       thr_hi_ref, thr_lo_ref, surplus_ref,
                       *, tq, tk, topk, rb):
    i = pl.program_id(0)
    n_tiles = hi_ref.shape[0]
    n_kt = ((i + 1) * tq + tk - 1) // tk
    q0 = i * tq
    qpos = q0 + lax.broadcasted_iota(jnp.int32, (tq, tk), 0)
    lane_pos = lax.broadcasted_iota(jnp.int32, (tq, tk), 1)

    def score_tile(c, carry):
        k0 = pl.multiple_of(c * tk, tk)
        ka = kia_ref[pl.ds(k0, tk), :]
        kb = kib_ref[pl.ds(k0, tk), :]
        acc = jnp.zeros((tq, tk), F32)
        for p in range(IDX_HEADS // 2):
            lhs = qi_ref[:, p * LANES:(p + 1) * LANES]
            sa = jnp.maximum(_dot_nt(lhs, ka), 0.0)
            sb = jnp.maximum(_dot_nt(lhs, kb), 0.0)
            acc = acc + wi_ref[:, 2 * p:2 * p + 1] * sa
            acc = acc + wi_ref[:, 2 * p + 1:2 * p + 2] * sb
        bits = pltpu.bitcast(acc, jnp.int32)
        key = bits ^ ((bits >> 31) & jnp.int32(0x7FFFFFFF))
        key = jnp.where(k0 + lane_pos <= qpos, key, jnp.int32(INT_MIN))
        hi_ref[c] = (key >> 16).astype(I16)
        lo_ref[c] = ((key & 0xFFFF) + I16_MIN).astype(I16)
        return carry

    lax.fori_loop(0, n_kt, score_tile, 0)

    def count(pred):
        def body(c, acc):
            parts = []
            for r0 in range(0, tq, rb):
                cand = cand_ref[r0:r0 + rb, :]
                part = jnp.zeros((rb, LANES), I16)
                for t in range(tk // LANES):
                    ls = slice(t * LANES, (t + 1) * LANES)
                    hit = pred(hi_ref[c, r0:r0 + rb, ls], lo_ref[c, r0:r0 + rb, ls], cand,
                               c * tk + t * LANES, r0)
                    part = part + hit.astype(I16)
                parts.append(part)
            return acc + jnp.concatenate(parts, axis=0)
        acc = lax.fori_loop(0, n_kt, body, jnp.zeros((tq, LANES), I16))
        return jnp.sum(acc.astype(F32), axis=1, keepdims=True)

    def set_cand(x):
        cand_ref[...] = jnp.broadcast_to(x, (tq, LANES)).astype(I16)

    def bisect16(pred, n_ge0):
        def bit_step(carry):
            b, c_best, n_best = carry
            cand = c_best + lax.shift_left(jnp.int32(1), 15 - b)
            set_cand(cand)
            cnt = count(pred)
            keep = cnt >= topk
            return b + 1, jnp.where(keep, cand, c_best), jnp.where(keep, cnt, n_best)

        def unresolved(carry):
            b, _, n_best = carry
            return jnp.logical_and(b < 16, jnp.max(jnp.abs(n_best - topk)) > 0.0)

        _, c_best, n_best = lax.while_loop(
            unresolved, bit_step, (jnp.int32(0), jnp.full((tq, 1), I16_MIN, jnp.int32), n_ge0))
        return c_best, n_best

    n_all = jnp.broadcast_to((n_kt * tk).astype(F32), (tq, 1))
    thr_hi, n_hi = bisect16(lambda hi, lo, cd, k0, r0: hi >= cd, n_all)
    thr_hi16 = jnp.broadcast_to(thr_hi, (tq, LANES)).astype(I16)

    def clamp_low(c, carry):
        for r0 in range(0, tq, rb):
            hi = hi_ref[c, r0:r0 + rb, :]
            th = jnp.concatenate([thr_hi16[r0:r0 + rb]] * (tk // LANES), axis=1)
            side = jnp.where(hi > th, jnp.asarray(2 ** 15 - 1, I16), jnp.asarray(I16_MIN, I16))
            lo_ref[c, r0:r0 + rb, :] = jnp.where(hi == th, lo_ref[c, r0:r0 + rb, :], side)
        return carry

    lax.fori_loop(0, n_kt, clamp_low, 0)
    thr_lo, n_ge = bisect16(lambda hi, lo, cd, k0, r0: lo >= cd, n_hi)
    thr_lo16 = jnp.broadcast_to(thr_lo, (tq, LANES)).astype(I16)
    is_floor = jnp.logical_and(thr_hi == I16_MIN, thr_lo == I16_MIN)

    cand_ref[...] = jnp.broadcast_to(jnp.where(is_floor, -1, 2 ** 15 - 1), (tq, LANES)).astype(I16)
    thr_hi_ref[...] = thr_hi16
    thr_lo_ref[...] = thr_lo16
    surplus_ref[...] = jnp.broadcast_to(jnp.where(is_floor, 0.0, n_ge - topk), (tq, LANES))
    lane_idx16 = lax.broadcasted_iota(jnp.int32, (TIE_ROWS, LANES), 1)

    def tie_group(gi, carry):
        r0 = pl.multiple_of(gi * TIE_ROWS, TIE_ROWS)
        rows = pl.ds(r0, TIE_ROWS)

        @pl.when(jnp.max(surplus_ref[rows, :]) > 0.0)
        def _ties():
            th = thr_hi_ref[rows, :]
            tl = thr_lo_ref[rows, :]

            def count_rows(pred):
                def body(c, acc):
                    part = jnp.zeros((TIE_ROWS, LANES), I16)
                    for t in range(tk // LANES):
                        ls = slice(t * LANES, (t + 1) * LANES)
                        hit = pred(hi_ref[c, rows, ls], lo_ref[c, rows, ls], c * tk + t * LANES)
                        part = part + hit.astype(I16)
                    return acc + part
                acc = lax.fori_loop(0, n_kt, body, jnp.zeros((TIE_ROWS, LANES), I16))
                return jnp.sum(acc.astype(F32), axis=1, keepdims=True)

            need = topk - count_rows(lambda hi, lo, k0: jnp.logical_or(
                hi > th, jnp.logical_and(hi == th, lo > tl)))

            def idx_step(b, cut):
                cand = cut + lax.shift_left(jnp.int32(1), 14 - b)
                cd = jnp.broadcast_to(cand, (TIE_ROWS, LANES)).astype(I16)
                cnt = count_rows(lambda hi, lo, k0: jnp.logical_and(
                    jnp.logical_and(hi == th, lo == tl), (k0 + lane_idx16).astype(I16) < cd))
                return jnp.where(cnt < need, cand, cut)

            cut = lax.fori_loop(0, 15, idx_step, jnp.zeros((TIE_ROWS, 1), jnp.int32))
            floor = jnp.logical_and(th.astype(jnp.int32) == I16_MIN, tl.astype(jnp.int32) == I16_MIN)
            cand_ref[rows, :] = jnp.where(floor, -1, jnp.broadcast_to(cut, (TIE_ROWS, LANES))).astype(I16)

        return carry

    lax.fori_loop(0, tq // TIE_ROWS, tie_group, 0)

    lane_idx = lax.broadcasted_iota(jnp.int32, (rb, tk), 1)

    def write_tile(c, carry):
        for r0 in range(0, tq, rb):
            hi = hi_ref[c, r0:r0 + rb, :]
            lo = lo_ref[c, r0:r0 + rb, :]
            th = jnp.concatenate([thr_hi16[r0:r0 + rb]] * (tk // LANES), axis=1)
            tl = jnp.concatenate([thr_lo16[r0:r0 + rb]] * (tk // LANES), axis=1)
            cut = jnp.concatenate([cand_ref[r0:r0 + rb, :]] * (tk // LANES), axis=1)
            gt = jnp.logical_or(hi > th, jnp.logical_and(hi == th, lo > tl))
            eq = jnp.logical_and(jnp.logical_and(hi == th, lo == tl),
                                 (c * tk + lane_idx).astype(I16) <= cut)
            keep = jnp.logical_or(gt, eq)
            bias_ref[c, r0:r0 + rb, :] = jnp.where(
                keep, jnp.zeros((), SOFTMAX_DTYPE), jnp.asarray(NEG_BIG, SOFTMAX_DTYPE))
        return carry

    lax.fori_loop(0, n_kt, write_tile, 0)

    def fill_tile(c, carry):
        bias_ref[c] = jnp.full((tq, tk), NEG_BIG, SOFTMAX_DTYPE)
        return carry

    lax.fori_loop(n_kt, n_tiles, fill_tile, 0)


def _dsa_select(qi, kia, kib, wi, topk, tq=256, tk=1024, rb=32):
    L, nqi = qi.shape
    assert L // LANES < 2 ** 15 and L < 2 ** 15
    n_tiles = L // tk
    return pl.pallas_call(
        functools.partial(_dsa_select_kernel, tq=tq, tk=tk, topk=topk, rb=rb),
        grid=(L // tq,),
        in_specs=[
            pl.BlockSpec((tq, nqi), lambda i: (i, 0)),
            pl.BlockSpec((tq, LANES), lambda i: (i, 0)),
            _const_spec((L, LANES)),
            _const_spec((L, LANES)),
        ],
        out_specs=pl.BlockSpec((n_tiles, tq, tk), lambda i: (0, i, 0)),
        out_shape=jax.ShapeDtypeStruct((n_tiles, L, tk), SOFTMAX_DTYPE),
        scratch_shapes=[
            pltpu.VMEM((n_tiles, tq, tk), I16),
            pltpu.VMEM((n_tiles, tq, tk), I16),
            pltpu.VMEM((tq, LANES), I16),
            pltpu.VMEM((tq, LANES), I16),
            pltpu.VMEM((tq, LANES), I16),
            pltpu.VMEM((tq, LANES), F32),
        ],
        compiler_params=_params("arbitrary"),
        name="dsa_select",
    )(qi, wi, kia, kib)


def _dsa_attn_kernel(qi_s, kj_s, q_ref, k_ref, v_ref, bias_ref, o_ref, m_ref, l_ref, acc_ref,
                     *, tq, tk, nheads):
    step = pl.program_id(0)
    i = qi_s[step]
    j = kj_s[step]
    n_kt = ((i + 1) * tq + tk - 1) // tk

    @pl.when(j == 0)
    def _init():
        m_ref[...] = jnp.full_like(m_ref, NEG_BIG)
        l_ref[...] = jnp.zeros_like(l_ref)
        acc_ref[...] = jnp.zeros_like(acc_ref)

    bias = jnp.concatenate([bias_ref[t] for t in range(bias_ref.shape[0])], axis=1)
    _flash_heads(q_ref, k_ref, v_ref, m_ref, l_ref, acc_ref, bias, nheads)

    @pl.when(j == n_kt - 1)
    def _finish():
        _flash_finish(o_ref, l_ref, acc_ref, nheads)


def _dsa_attn(q, k, v, bias, tq=512, tk=1024):
    L, D = q.shape
    nheads = DSA_HEADS
    tb = bias.shape[2]
    assert D == nheads * LANES and bias.shape == (L // tb, L, tb) and tk % tb == 0
    qi_s, kj_s = _causal_steps(L // tq, tq, tk)
    qmap = lambda s, qs, ks: (qs[s], 0)
    kmap = lambda s, qs, ks: (ks[s], 0)
    grid_spec = pltpu.PrefetchScalarGridSpec(
        num_scalar_prefetch=2,
        grid=(qi_s.shape[0],),
        in_specs=[
            pl.BlockSpec((tq, D), qmap),
            pl.BlockSpec((tk, D), kmap),
            pl.BlockSpec((tk, D), kmap),
            pl.BlockSpec((tk // tb, tq, tb), lambda s, qs, ks: (ks[s], qs[s], 0)),
        ],
        out_specs=pl.BlockSpec((tq, D), qmap),
        scratch_shapes=[
            pltpu.VMEM((nheads, tq, LANES), F32),
            pltpu.VMEM((nheads, tq, LANES), F32),
            pltpu.VMEM((tq, D), F32),
        ],
    )
    return pl.pallas_call(
        functools.partial(_dsa_attn_kernel, tq=tq, tk=tk, nheads=nheads),
        grid_spec=grid_spec,
        out_shape=jax.ShapeDtypeStruct((L, D), MXU_DTYPE),
        compiler_params=_params("arbitrary"),
        name="dsa_attn",
    )(qi_s, kj_s, q, k, v, bias)


def _dsa_layer(h, g, w_in, idx_k_g, idx_k_b, w_out):
    L, D = h.shape
    nqi = IDX_HEADS * IDX_DIM
    o3 = 3 * D
    o4 = o3 + nqi
    o5 = o4 + IDX_DIM
    w_ki = w_in[:, o4:o5]
    w_wi = w_in[:, o5:]
    zpad = lambda n: jnp.zeros((D, n), w_in.dtype)
    w_cat = jnp.concatenate(
        [w_in[:, :o4],
         w_ki, w_wi, zpad(LANES - IDX_DIM - IDX_HEADS),
         w_wi, zpad(LANES - IDX_DIM - IDX_HEADS), w_ki], axis=1).astype(MXU_DTYPE)
    z = jnp.zeros((IDX_DIM,), F32)
    lg2 = jnp.stack([jnp.concatenate([idx_k_g, z]), jnp.concatenate([z, idx_k_g])])
    lb2 = jnp.stack([jnp.concatenate([idx_k_b, z]), jnp.concatenate([z, idx_k_b])])
    q, k, v, qi, kia, kib, wi = _dsa_proj(h, g, w_cat, lg2, lb2)
    bias = _dsa_select(qi, kia, kib, wi, topk=min(TOPK_MAX, L // 4))
    o = _dsa_attn(q, k, v, bias)
    return _out_proj(o, w_out.astype(MXU_DTYPE), jnp.zeros((D,), F32), h)


def _glu_proj_kernel(h_ref, g_ref, w_ref, b_ref, u_ref, *, d):
    xn = _rmsnorm_rows(h_ref[...], g_ref[...]).astype(MXU_DTYPE)
    cw = 512
    for c in range(0, d, cw):
        a = jnp.dot(xn, w_ref[:, c:c + cw], preferred_element_type=F32) + b_ref[:, c:c + cw]
        gt = jnp.dot(xn, w_ref[:, d + c:d + c + cw], preferred_element_type=F32) + b_ref[:, d + c:d + c + cw]
        u_ref[:, c:c + cw] = a * _sigmoid(gt)


def _glu_proj(h, g, w, b, tm=512):
    L, D = h.shape
    return pl.pallas_call(
        functools.partial(_glu_proj_kernel, d=D),
        grid=(L // tm,),
        in_specs=[pl.BlockSpec((tm, D), lambda i: (i, 0)), _const_spec((1, D)),
                  _const_spec((D, 2 * D)), _const_spec((1, 2 * D))],
        out_specs=pl.BlockSpec((tm, D), lambda i: (i, 0)),
        out_shape=jax.ShapeDtypeStruct((L, D), F32),
        compiler_params=_params("arbitrary"),
        name="conv_glu_proj",
    )(h, g.reshape(1, D), w, b.reshape(1, 2 * D))


CONV_HALO = 32


def _conv_out_kernel(u_ref, up_ref, wdw_ref, bdw_ref, lg_ref, lb_ref, w2_ref, b2_ref, h_ref, o_ref,
                     x_ref, xs_ref, *, tm):
    i = pl.program_id(0)
    x_ref[CONV_HALO:, :] = u_ref[...]
    x_ref[:CONV_HALO, :] = jnp.where(i > 0, up_ref[...], 0.0)
    off = CONV_HALO - (CONV_WIDTH - 1)
    acc = jnp.zeros(u_ref.shape, F32)
    for r in range(SUBLANES):
        taps = [t for t in range(CONV_WIDTH) if (off + t) % SUBLANES == r]
        if not taps:
            continue
        span = off + taps[-1] - r + tm
        src = x_ref
        if r:
            xs_ref[:span, :] = x_ref[r:r + span, :]
            src = xs_ref
        for t in taps:
            a = off + t - r
            acc = acc + wdw_ref[t:t + 1, :] * src[a:a + tm, :]
    y = acc + bdw_ref[...]
    mu = jnp.mean(y, axis=-1, keepdims=True)
    dlt = y - mu
    var = jnp.mean(dlt * dlt, axis=-1, keepdims=True)
    y = _silu(dlt * lax.rsqrt(var + EPS) * lg_ref[...] + lb_ref[...])
    o_ref[...] = h_ref[...] + _dot(y, w2_ref[...]) + b2_ref[...]


def _conv_out(u, w_dw, b_dw, ln_g, ln_b, w2, b2, h, tm=512):
    L, D = u.shape
    r = tm // CONV_HALO
    vec = lambda a: a.reshape(1, D)
    return pl.pallas_call(
        functools.partial(_conv_out_kernel, tm=tm),
        grid=(L // tm,),
        in_specs=[
            pl.BlockSpec((tm, D), lambda i: (i, 0)),
            pl.BlockSpec((CONV_HALO, D), lambda i: (jnp.maximum(i * r - 1, 0), 0)),
            _const_spec((CONV_WIDTH, D)),
            _const_spec((1, D)), _const_spec((1, D)), _const_spec((1, D)),
            _const_spec((D, D)), _const_spec((1, D)),
            pl.BlockSpec((tm, D), lambda i: (i, 0)),
        ],
        out_specs=pl.BlockSpec((tm, D), lambda i: (i, 0)),
        out_shape=jax.ShapeDtypeStruct((L, D), F32),
        scratch_shapes=[pltpu.VMEM((tm + CONV_HALO, D), F32), pltpu.VMEM((tm + CONV_HALO, D), F32)],
        compiler_params=_params("arbitrary"),
        name="conv_out",
    )(u, u, w_dw, vec(b_dw), vec(ln_g), vec(ln_b), w2, vec(b2), h)


def _conv_layer(h, g, w_pw1, b_pw1, w_dw, b_dw, ln_g, ln_b, w_pw2, b_pw2):
    u = _glu_proj(h, g, w_pw1.astype(MXU_DTYPE), b_pw1)
    return _conv_out(u, w_dw, b_dw, ln_g, ln_b, w_pw2.astype(MXU_DTYPE), b_pw2, h)


def _mla_prep_kernel(h_ref, g_ref, w1_ref, qg_ref, kvg_ref, wqa_ref, wqb_ref, wka_ref, wv_ref,
                     c_ref, s_ref, q_ref, k_ref, v_ref, *, scale):
    xn = _rmsnorm_rows(h_ref[...], g_ref[...]).astype(MXU_DTYPE)
    proj = jnp.dot(xn, w1_ref[...], preferred_element_type=F32)
    o1 = MLA_Q_LORA
    o2 = o1 + MLA_KV_LORA
    cq = _rmsnorm_rows(proj[:, :o1], qg_ref[...]).astype(MXU_DTYPE)
    ckv = _rmsnorm_rows(proj[:, o1:o2], kvg_ref[...]).astype(MXU_DTYPE)
    cos = c_ref[...]
    sin = s_ref[...]
    kr = proj[:, o2:o2 + LANES] * cos + proj[:, o2 + LANES:o2 + 2 * LANES] * sin
    for hd in range(MLA_HEADS):
        cs = slice(hd * LANES, (hd + 1) * LANES)
        qa = jnp.dot(cq, wqa_ref[:, cs], preferred_element_type=F32)
        qb = jnp.dot(cq, wqb_ref[:, cs], preferred_element_type=F32)
        q_ref[:, cs] = ((qa * cos + qb * sin) * scale).astype(q_ref.dtype)
        ka = jnp.dot(ckv, wka_ref[:, cs], preferred_element_type=F32)
        k_ref[:, cs] = (ka + kr).astype(k_ref.dtype)
        va = jnp.dot(ckv, wv_ref[:, cs], preferred_element_type=F32)
        lane = lax.broadcasted_iota(jnp.int32, va.shape, 1)
        v_ref[:, cs] = jnp.where(lane < MLA_V, va, 1.0).astype(v_ref.dtype)


def _mla_prep(h, g, w1, qg, kvg, wqa, wqb, wka, wv, ctab, stab, tm=512):
    L, D = h.shape
    HP = MLA_HEADS * LANES
    row = lambda n: pl.BlockSpec((tm, n), lambda i: (i, 0))
    full = lambda a: _const_spec(a.shape)
    return pl.pallas_call(
        functools.partial(_mla_prep_kernel, scale=(MLA_NOPE + MLA_ROPE) ** -0.5 * LOG2E),
        grid=(L // tm,),
        in_specs=[row(D), _const_spec((1, D)), full(w1), _const_spec((1, MLA_Q_LORA)),
                  _const_spec((1, MLA_KV_LORA)), full(wqa), full(wqb), full(wka), full(wv),
                  row(LANES), row(LANES)],
        out_specs=[row(HP), row(HP), row(HP)],
        out_shape=[jax.ShapeDtypeStruct((L, HP), MXU_DTYPE)] * 3,
        compiler_params=_params("arbitrary"),
        name="mla_prep",
    )(h, g.reshape(1, D), w1, qg.reshape(1, -1), kvg.reshape(1, -1), wqa, wqb, wka, wv, ctab, stab)


def _mla_attn_kernel(qi_s, kj_s, q_ref, k_ref, v_ref, o_ref, m_ref, acc_ref, *, tq, tk, nheads):
    step = pl.program_id(0)
    i = qi_s[step]
    j = kj_s[step]
    n_kt = ((i + 1) * tq + tk - 1) // tk

    @pl.when(j == 0)
    def _init():
        m_ref[...] = jnp.full_like(m_ref, NEG_BIG)
        acc_ref[...] = jnp.zeros_like(acc_ref)

    qpos = i * tq + lax.broadcasted_iota(jnp.int32, (tq, tk), 0)
    kpos = j * tk + lax.broadcasted_iota(jnp.int32, (tq, tk), 1)
    bias = jnp.where(kpos <= qpos, 0.0, NEG_BIG).astype(SOFTMAX_DTYPE)
    _flash_heads(q_ref, k_ref, v_ref, m_ref, None, acc_ref, bias, nheads)

    @pl.when(j == n_kt - 1)
    def _finish():
        _flash_finish(o_ref, None, acc_ref, nheads)


def _mla_attn(q, k, v, tq=512, tk=1024):
    L, HP = q.shape
    qi_s, kj_s = _causal_steps(L // tq, tq, tk)
    qmap = lambda s, qs, ks: (qs[s], 0)
    kmap = lambda s, qs, ks: (ks[s], 0)
    grid_spec = pltpu.PrefetchScalarGridSpec(
        num_scalar_prefetch=2,
        grid=(qi_s.shape[0],),
        in_specs=[pl.BlockSpec((tq, HP), qmap),
                  pl.BlockSpec((tk, HP), kmap), pl.BlockSpec((tk, HP), kmap)],
        out_specs=pl.BlockSpec((tq, HP), qmap),
        scratch_shapes=[pltpu.VMEM((MLA_HEADS, tq, LANES), F32),
                        pltpu.VMEM((tq, HP), F32)],
    )
    return pl.pallas_call(
        functools.partial(_mla_attn_kernel, tq=tq, tk=tk, nheads=MLA_HEADS),
        grid_spec=grid_spec,
        out_shape=jax.ShapeDtypeStruct((L, HP), MXU_DTYPE),
        compiler_params=_params("arbitrary"),
        name="mla_attn",
    )(qi_s, kj_s, q, k, v)


def _place_heads(w, nheads, src_w, lane0, swap_half=0):
    K = w.shape[0]
    w3 = w.reshape(K, nheads, src_w)
    if swap_half:
        w3 = jnp.concatenate([w3[..., swap_half:], w3[..., :swap_half]], axis=-1)
    out = jnp.zeros((K, nheads, LANES), w.dtype)
    out = out.at[:, :, lane0:lane0 + src_w].set(w3)
    return out.reshape(K, nheads * LANES)


def _mla_layer(h, g, w_in, q_norm_g, w_uq, kv_norm_g, w_ukv, w_out):
    L, D = h.shape
    half = MLA_ROPE // 2
    o2 = MLA_Q_LORA + MLA_KV_LORA
    w_kr = w_in[:, o2:]
    w1 = jnp.concatenate(
        [w_in[:, :o2], _place_heads(w_kr, 1, MLA_ROPE, MLA_NOPE),
         _place_heads(w_kr, 1, MLA_ROPE, MLA_NOPE, swap_half=half)], axis=1).astype(MXU_DTYPE)
    uq = w_uq.reshape(MLA_Q_LORA, MLA_HEADS, MLA_NOPE + MLA_ROPE)
    uq_rope = uq[..., MLA_NOPE:].reshape(MLA_Q_LORA, MLA_HEADS * MLA_ROPE)
    wqa = _place_heads(w_uq, MLA_HEADS, MLA_NOPE + MLA_ROPE, 0).astype(MXU_DTYPE)
    wqb = _place_heads(uq_rope, MLA_HEADS, MLA_ROPE, MLA_NOPE, swap_half=half).astype(MXU_DTYPE)
    ukv = w_ukv.reshape(MLA_KV_LORA, MLA_HEADS, MLA_NOPE + MLA_V)
    wka = _place_heads(ukv[..., :MLA_NOPE].reshape(MLA_KV_LORA, -1), MLA_HEADS, MLA_NOPE, 0).astype(MXU_DTYPE)
    wv = _place_heads(ukv[..., MLA_NOPE:].reshape(MLA_KV_LORA, -1), MLA_HEADS, MLA_V, 0).astype(MXU_DTYPE)
    pos = jnp.arange(L, dtype=F32)
    inv_freq = ROPE_THETA ** (-jnp.arange(0, MLA_ROPE, 2, dtype=F32) / MLA_ROPE)
    ang = pos[:, None] * inv_freq[None, :]
    cos, sin = jnp.cos(ang), jnp.sin(ang)
    zeros = jnp.zeros((L, LANES - MLA_NOPE - MLA_ROPE), F32)
    ctab = jnp.concatenate([jnp.ones((L, MLA_NOPE), F32), cos, cos, zeros], axis=1)
    stab = jnp.concatenate([jnp.zeros((L, MLA_NOPE), F32), -sin, sin, zeros], axis=1)
    q, k, v = _mla_prep(h, g, w1, q_norm_g, kv_norm_g, wqa, wqb, wka, wv, ctab, stab)
    o = _mla_attn(q, k, v)
    wo = w_out.reshape(MLA_HEADS, MLA_V, D)
    wo_pad = jnp.concatenate([wo, jnp.zeros_like(wo)], axis=1).reshape(MLA_HEADS * LANES, D).astype(MXU_DTYPE)
    return _out_proj(o, wo_pad, jnp.zeros((D,), F32), h)


def _gdn_scan_kernel(h_ref, g_ref, w_ref, cw_ref, alog_ref, dtb_ref, o_ref, gate_ref, x_ref, s_ref,
                     *, nheads, dk, dv, c, nch):
    n = pl.program_id(0)
    halo = 8
    nqk = nheads * dk

    @pl.when(n == 0)
    def _init():
        x_ref[:halo, :] = jnp.zeros((halo, x_ref.shape[1]), F32)
        s_ref[...] = jnp.zeros_like(s_ref)

    rows = nch * c
    nqkv = x_ref.shape[1]
    nv = nheads * dv
    xn = _rmsnorm_rows(h_ref[...], g_ref[...]).astype(MXU_DTYPE)
    cw_cols = 512
    for c0 in range(0, nqkv, cw_cols):
        x_ref[halo:, c0:c0 + cw_cols] = jnp.dot(xn, w_ref[:, c0:c0 + cw_cols], preferred_element_type=F32)
    gate_ref[...] = jnp.dot(xn, w_ref[:, nqkv:nqkv + nv], preferred_element_type=F32)
    ba = jnp.dot(xn, w_ref[:, nqkv + nv:], preferred_element_type=F32)
    off = halo - (GDN_CONV - 1)
    acc = jnp.zeros((rows, nqkv), F32)
    for t in range(GDN_CONV):
        acc = acc + cw_ref[t:t + 1, :] * x_ref[off + t:off + t + rows, :]
    x_ref[:halo, :] = x_ref[rows:rows + halo, :]
    qkv = _silu(acc)

    beta_all = _sigmoid(ba[:, :LANES])
    a_raw = ba[:, LANES:] + dtb_ref[...]
    softplus = jnp.maximum(a_raw, 0.0) + jnp.log(1.0 + jnp.exp(-jnp.abs(a_raw)))
    g_all = -jnp.exp(alog_ref[...]) * softplus
    row = lax.broadcasted_iota(jnp.int32, (c, c), 0)
    col = lax.broadcasted_iota(jnp.int32, (c, c), 1)
    tril = row >= col
    tril_strict = row > col
    eye = jnp.where(row == col, 1.0, 0.0)

    units = [(ch, hd) for ch in range(nch) for hd in range(nheads)]
    gc_all = [jnp.dot(tril.astype(F32), g_all[ch * c:(ch + 1) * c], preferred_element_type=F32,
                      precision=lax.Precision.HIGHEST) for ch in range(nch)]
    gc_t = [x.T for x in gc_all]

    def chunk_rows(x, ch):
        return x[ch * c:(ch + 1) * c]

    q = [chunk_rows(qkv[:, hd * dk:(hd + 1) * dk], ch) for ch, hd in units]
    k = [chunk_rows(qkv[:, nqk + hd * dk:nqk + (hd + 1) * dk], ch) for ch, hd in units]
    v = [chunk_rows(qkv[:, 2 * nqk + hd * dv:2 * nqk + (hd + 1) * dv], ch) for ch, hd in units]
    q = [x * lax.rsqrt(jnp.sum(x * x, axis=-1, keepdims=True) + EPS) * (dk ** -0.5) for x in q]
    k = [x * lax.rsqrt(jnp.sum(x * x, axis=-1, keepdims=True) + EPS) for x in k]
    beta = [chunk_rows(beta_all[:, hd:hd + 1], ch) for ch, hd in units]
    gc = [gc_all[ch][:, hd:hd + 1] for ch, hd in units]
    gc_last = [gc_all[ch][c - 1:c, hd:hd + 1] for ch, hd in units]
    n_u = range(len(units))
    decay = [jnp.where(tril, jnp.exp(jnp.where(tril, gc[u] - gc_t[units[u][0]][units[u][1]:units[u][1] + 1, :],
                                               0.0)), 0.0) for u in n_u]
    kk = [_dot_nt(k[u], k[u]) for u in n_u]
    qk = [_dot_nt(q[u], k[u]) for u in n_u]
    a_mat = [jnp.where(tril_strict, beta[u] * kk[u] * decay[u], 0.0) for u in n_u]
    t_inv = [eye - jnp.where((row >> 1) == (col >> 1), a_mat[u], 0.0) for u in n_u]
    for lvl in range(1, int(math.log2(c))):
        blk_mask = jnp.logical_and((row >> (lvl + 1)) == (col >> (lvl + 1)), (row >> lvl) != (col >> lvl))
        tm = [_dot(t_inv[u], jnp.where(blk_mask, a_mat[u], 0.0)) for u in n_u]
        tmt = [_dot(tm[u], t_inv[u]) for u in n_u]
        t_inv = [t_inv[u] - tmt[u] for u in n_u]
    e_gc = [jnp.exp(gc[u]) for u in n_u]
    uu = [_dot(t_inv[u], v[u] * beta[u]) for u in n_u]
    w = [_dot(t_inv[u], k[u] * (beta[u] * e_gc[u])) for u in n_u]
    q_dec = [q[u] * e_gc[u] for u in n_u]
    k_dec_t = [(k[u] * jnp.exp(gc_last[u] - gc[u])).T for u in n_u]
    qk_dec = [qk[u] * decay[u] for u in n_u]
    g_last = [jnp.exp(gc_last[u]) for u in n_u]

    s = [s_ref[hd] for hd in range(nheads)]
    for ch in range(nch):
        us = [ch * nheads + hd for hd in range(nheads)]
        ws = [_dot(w[u], s[hd]) for hd, u in enumerate(us)]
        qs = [_dot(q_dec[u], s[hd]) for hd, u in enumerate(us)]
        v_new = [uu[u] - ws[hd] for hd, u in enumerate(us)]
        o_intra = [_dot(qk_dec[u], v_new[hd]) for hd, u in enumerate(us)]
        s_upd = [_dot(k_dec_t[u], v_new[hd]) for hd, u in enumerate(us)]
        for hd, u in enumerate(us):
            o_ref[ch * c:(ch + 1) * c, hd * dv:(hd + 1) * dv] = qs[hd] + o_intra[hd]
        s = [s[hd] * g_last[u] + s_upd[hd] for hd, u in enumerate(us)]
    for hd in range(nheads):
        s_ref[hd] = s[hd]


def _gdn_scan(h, g, w_cat, conv_w, a_log_pad, dt_bias_pad, nch=4):
    L, D = h.shape
    c = GDN_CHUNK
    rows = nch * c
    nqkv = 2 * GDN_HEADS * GDN_DK + GDN_HEADS * GDN_DV
    nv = GDN_HEADS * GDN_DV
    assert w_cat.shape == (D, nqkv + nv + 2 * LANES)
    return pl.pallas_call(
        functools.partial(_gdn_scan_kernel, nheads=GDN_HEADS, dk=GDN_DK, dv=GDN_DV, c=c, nch=nch),
        grid=(L // rows,),
        in_specs=[pl.BlockSpec((rows, D), lambda n: (n, 0)), _const_spec((1, D)),
                  _const_spec(w_cat.shape),
                  _const_spec((GDN_CONV, nqkv)), _const_spec((1, LANES)), _const_spec((1, LANES))],
        out_specs=[pl.BlockSpec((rows, nv), lambda n: (n, 0)), pl.BlockSpec((rows, nv), lambda n: (n, 0))],
        out_shape=[jax.ShapeDtypeStruct((L, nv), F32), jax.ShapeDtypeStruct((L, nv), F32)],
        scratch_shapes=[pltpu.VMEM((rows + 8, nqkv), F32),
                        pltpu.VMEM((GDN_HEADS, GDN_DK, GDN_DV), F32)],
        compiler_params=_params("arbitrary"),
        name="gdn_scan",
    )(h, g.reshape(1, D), w_cat, conv_w, a_log_pad, dt_bias_pad)


def _gdn_out_kernel(o_ref, gate_ref, og_ref, w_ref, h_ref, y_ref, *, nheads, dv):
    parts = []
    for hd in range(nheads):
        cs = slice(hd * dv, (hd + 1) * dv)
        on = _rmsnorm_rows(o_ref[:, cs], og_ref[...])
        parts.append((on * _silu(gate_ref[:, cs])).astype(MXU_DTYPE))
    y_ref[...] = h_ref[...] + jnp.dot(jnp.concatenate(parts, axis=1), w_ref[...],
                                      preferred_element_type=F32)


def _gdn_out(o, gate, o_norm_g, w_out, h, tm=512):
    L, D = h.shape
    nv = GDN_HEADS * GDN_DV
    return pl.pallas_call(
        functools.partial(_gdn_out_kernel, nheads=GDN_HEADS, dv=GDN_DV),
        grid=(L // tm,),
        in_specs=[pl.BlockSpec((tm, nv), lambda i: (i, 0)),
                  pl.BlockSpec((tm, nv), lambda i: (i, 0)),
                  _const_spec((1, GDN_DV)), _const_spec((nv, D)),
                  pl.BlockSpec((tm, D), lambda i: (i, 0))],
        out_specs=pl.BlockSpec((tm, D), lambda i: (i, 0)),
        out_shape=jax.ShapeDtypeStruct((L, D), F32),
        compiler_params=_params("arbitrary"),
        name="gdn_out",
    )(o, gate, o_norm_g.reshape(1, GDN_DV), w_out, h)


def _gdn_layer(h, g, w_in, conv_w, a_log, dt_bias, o_norm_g, w_out):
    L, D = h.shape
    nmain = 2 * GDN_HEADS * GDN_DK + 2 * GDN_HEADS * GDN_DV
    zpad = jnp.zeros((D, LANES - GDN_HEADS), w_in.dtype)
    w_cat = jnp.concatenate([w_in[:, :nmain], w_in[:, nmain:nmain + GDN_HEADS], zpad,
                             w_in[:, nmain + GDN_HEADS:], zpad], axis=1).astype(MXU_DTYPE)
    vpad = lambda a: jnp.concatenate([a.astype(F32), jnp.zeros((LANES - GDN_HEADS,), F32)]).reshape(1, LANES)
    o, gate = _gdn_scan(h, g, w_cat, conv_w, vpad(a_log), vpad(dt_bias))
    return _gdn_out(o, gate, o_norm_g, w_out.astype(MXU_DTYPE), h)


def kernel(x, norm_mix_g, norm_mlp_g, final_g, mlp_w1, mlp_w2, dsa_w_in, dsa_idx_k_g, dsa_idx_k_b, dsa_w_out, conv_w_pw1, conv_b_pw1, conv_w_dw, conv_b_dw, conv_ln_g, conv_ln_b, conv_w_pw2, conv_b_pw2, mla_w_in, mla_q_norm_g, mla_w_uq, mla_kv_norm_g, mla_w_ukv, mla_w_out, gdn_w_in, gdn_conv_w, gdn_a_log, gdn_dt_bias, gdn_o_norm_g, gdn_w_out):
    b, L, D = x.shape
    depth = norm_mix_g.shape[0]
    outs = []
    rows = x.reshape(b * L, D)
    for bi in range(b):
        h = rows if b == 1 else rows[bi * L:(bi + 1) * L]
        for i in range(depth):
            m = i % 4
            jl = i // 4
            g = norm_mix_g[i]
            if m == 0:
                h = _dsa_layer(h, g, dsa_w_in[jl], dsa_idx_k_g[jl], dsa_idx_k_b[jl], dsa_w_out[jl])
            elif m == 1:
                h = _conv_layer(h, g, conv_w_pw1[jl], conv_b_pw1[jl], conv_w_dw[jl], conv_b_dw[jl],
                                conv_ln_g[jl], conv_ln_b[jl], conv_w_pw2[jl], conv_b_pw2[jl])
            elif m == 2:
                h = _mla_layer(h, g, mla_w_in[jl], mla_q_norm_g[jl], mla_w_uq[jl], mla_kv_norm_g[jl],
                               mla_w_ukv[jl], mla_w_out[jl])
            else:
                h = _gdn_layer(h, g, gdn_w_in[jl], gdn_conv_w[jl], gdn_a_log[jl], gdn_dt_bias[jl],
                               gdn_o_norm_g[jl], gdn_w_out[jl])
            h = _mlp(h, norm_mlp_g[i], mlp_w1[i].astype(MXU_DTYPE), mlp_w2[i].astype(MXU_DTYPE),
                     final_g, final_norm=(i == depth - 1))
        outs.append(h)
    return outs[0].reshape(1, L, D) if b == 1 else jnp.stack(outs)
```

```python
import functools
import math

import jax
import jax.numpy as jnp
import numpy as np
from jax import lax
from jax.experimental import pallas as pl
from jax.experimental.pallas import tpu as pltpu

F32 = jnp.float32
MXU_DTYPE = jnp.bfloat16
EPS = 1e-6
LANES = 128
SUBLANES = 8
VMEM_LIMIT_BYTES = 56 * 1024 * 1024
SOFTMAX_DTYPE = jnp.bfloat16
NEG_BIG = -(2.0 ** 100)
INT_MIN = -(2 ** 31)

IDX_HEADS = 8
IDX_DIM = 64
TOPK_MAX = 256
DSA_HEADS = 8
CONV_WIDTH = 31
MLA_HEADS = 16
MLA_Q_LORA = 384
MLA_KV_LORA = 256
MLA_NOPE = 64
MLA_ROPE = 32
MLA_V = 64
ROPE_THETA = 10000.0
GDN_HEADS = 8
GDN_DK = 128
GDN_DV = 128
GDN_CONV = 4
GDN_CHUNK = 64


def _params(*sem):
    return pltpu.CompilerParams(dimension_semantics=sem, vmem_limit_bytes=VMEM_LIMIT_BYTES)


def _dot(a, b):
    return jnp.dot(a.astype(MXU_DTYPE), b.astype(MXU_DTYPE), preferred_element_type=F32)


def _dot_nt(a, b):
    return lax.dot_general(a.astype(MXU_DTYPE), b.astype(MXU_DTYPE),
                           (((1,), (1,)), ((), ())), preferred_element_type=F32)


def _rmsnorm_rows(x, g):
    return x * lax.rsqrt(jnp.mean(x * x, axis=-1, keepdims=True) + EPS) * g


def _sigmoid(x):
    return 1.0 / (1.0 + jnp.exp(-x))


def _silu(x):
    return x * _sigmoid(x)


def _const_spec(shape):
    return pl.BlockSpec(shape, lambda *_: (0,) * len(shape))


LOG2E = math.log2(math.e)
LOGITS_AHEAD = 2


def _causal_steps(n_q, tq, tk):
    qi, kj = [], []
    for i in range(n_q):
        for j in range(((i + 1) * tq - 1) // tk + 1):
            qi.append(i)
            kj.append(j)
    return jnp.asarray(np.array(qi, np.int32)), jnp.asarray(np.array(kj, np.int32))


def _flash_heads(q_ref, k_ref, v_ref, m_ref, l_ref, acc_ref, bias, nheads):
    ones = None if l_ref is None else jnp.ones((k_ref.shape[0], LANES), MXU_DTYPE)

    def logits(h):
        cs = slice(h * LANES, (h + 1) * LANES)
        return _dot_nt(q_ref[:, cs], k_ref[:, cs])

    ahead = [logits(h) for h in range(min(LOGITS_AHEAD, nheads))]
    for h in range(nheads):
        cs = slice(h * LANES, (h + 1) * LANES)
        s = ahead.pop(0)
        if h + LOGITS_AHEAD < nheads:
            ahead.append(logits(h + LOGITS_AHEAD))
        s = s.astype(SOFTMAX_DTYPE) + bias
        m_prev = m_ref[h]
        m_next = jnp.maximum(m_prev, jnp.max(s, axis=1, keepdims=True).astype(F32))
        alpha = jnp.exp2(m_prev - m_next)
        p = jnp.exp2(s - m_next[:, 0:1].astype(SOFTMAX_DTYPE)).astype(MXU_DTYPE)
        m_ref[h] = m_next
        if l_ref is None:
            acc_ref[:, cs] = alpha * acc_ref[:, cs] + jnp.dot(p, v_ref[:, cs], preferred_element_type=F32)
        else:
            v_ext = jnp.concatenate([v_ref[:, cs].astype(MXU_DTYPE), ones], axis=1)
            pv = jnp.dot(p, v_ext, preferred_element_type=F32)
            l_ref[h] = alpha * l_ref[h] + pv[:, LANES:]
            acc_ref[:, cs] = alpha * acc_ref[:, cs] + pv[:, :LANES]


def _flash_finish(o_ref, l_ref, acc_ref, nheads, w_ref, h_ref):
    half = LANES // 2
    proj = jnp.zeros(o_ref.shape, F32)
    for h in range(nheads):
        cs = slice(h * LANES, (h + 1) * LANES)
        acc = acc_ref[:, cs]
        if l_ref is None:
            lane = lax.broadcasted_iota(jnp.int32, acc.shape, 1)
            out = jnp.where(lane < half, acc / pltpu.roll(acc, half, axis=1), 0.0)
        else:
            out = acc / l_ref[h]
        proj = proj + _dot(out, w_ref[cs, :])
    o_ref[...] = h_ref[...] + proj


def _mlp_kernel(h_ref, g_ref, w1_ref, w2_ref, gf_ref, o_ref, xn_ref, acc_ref, *, final_norm):
    f = pl.program_id(1)

    @pl.when(f == 0)
    def _():
        xn_ref[...] = _rmsnorm_rows(h_ref[...], g_ref[...]).astype(xn_ref.dtype)
        acc_ref[...] = jnp.zeros_like(acc_ref)

    a = _dot(xn_ref[...], w1_ref[...])
    a = jnp.square(jnp.maximum(a, 0.0))
    acc_ref[...] += _dot(a, w2_ref[...])

    @pl.when(f == pl.num_programs(1) - 1)
    def _():
        y = h_ref[...] + acc_ref[...]
        if final_norm:
            y = _rmsnorm_rows(y, gf_ref[...])
        o_ref[...] = y


def _mlp(h, g, w1, w2, gf, final_norm, tm=1024, tf=2048):
    L, D = h.shape
    dff = w1.shape[1]
    return pl.pallas_call(
        functools.partial(_mlp_kernel, final_norm=final_norm),
        grid=(L // tm, dff // tf),
        in_specs=[
            pl.BlockSpec((tm, D), lambda i, f: (i, 0)),
            _const_spec((1, D)),
            pl.BlockSpec((D, tf), lambda i, f: (0, f)),
            pl.BlockSpec((tf, D), lambda i, f: (f, 0)),
            _const_spec((1, D)),
        ],
        out_specs=pl.BlockSpec((tm, D), lambda i, f: (i, 0)),
        out_shape=jax.ShapeDtypeStruct((L, D), F32),
        scratch_shapes=[pltpu.VMEM((tm, D), MXU_DTYPE), pltpu.VMEM((tm, D), F32)],
        compiler_params=_params("arbitrary", "arbitrary"),
        name="mlp",
    )(h, g.reshape(1, D), w1, w2, gf.reshape(1, D))


def _dsa_proj_kernel(h_ref, g_ref, w_ref, lg_ref, lb_ref,
                     q_ref, k_ref, v_ref, qi_ref, kia_ref, kib_ref, wi_ref, *, d, scale, wi_scale):
    xn = _rmsnorm_rows(h_ref[...], g_ref[...]).astype(MXU_DTYPE)

    def mm(c0, n):
        return jnp.dot(xn, w_ref[:, c0:c0 + n], preferred_element_type=F32)

    cw = 512
    for c in range(0, d, cw):
        q_ref[:, c:c + cw] = (mm(c, cw) * scale).astype(q_ref.dtype)
        k_ref[:, c:c + cw] = mm(d + c, cw).astype(k_ref.dtype)
        v_ref[:, c:c + cw] = mm(2 * d + c, cw).astype(v_ref.dtype)
    nqi = IDX_HEADS * IDX_DIM
    qi_ref[...] = mm(3 * d, nqi).astype(qi_ref.dtype)
    xa = mm(3 * d + nqi, LANES)
    xb = mm(3 * d + nqi + LANES, LANES)
    lane = lax.broadcasted_iota(jnp.int32, xa.shape, 1)

    def masked_ln(x, m, gain, bias):
        mu = jnp.sum(jnp.where(m, x, 0.0), axis=-1, keepdims=True) * (1.0 / IDX_DIM)
        dlt = jnp.where(m, x - mu, 0.0)
        var = jnp.sum(dlt * dlt, axis=-1, keepdims=True) * (1.0 / IDX_DIM)
        return dlt * lax.rsqrt(var + EPS) * gain + bias

    kia_ref[...] = masked_ln(xa, lane < IDX_DIM, lg_ref[0:1, :], lb_ref[0:1, :]).astype(kia_ref.dtype)
    kib_ref[...] = masked_ln(xb, lane >= IDX_DIM, lg_ref[1:2, :], lb_ref[1:2, :]).astype(kib_ref.dtype)
    wi_ref[...] = jnp.where(lane < IDX_HEADS, xb, 0.0) * wi_scale


def _dsa_proj(h, g, w_cat, lg2, lb2, tm=1024):
    L, D = h.shape
    N = w_cat.shape[1]
    nqi = IDX_HEADS * IDX_DIM
    row = lambda n: pl.BlockSpec((tm, n), lambda i: (i, 0))
    return pl.pallas_call(
        functools.partial(_dsa_proj_kernel, d=D, scale=(D // DSA_HEADS) ** -0.5 * LOG2E,
                          wi_scale=IDX_HEADS ** -0.5 * IDX_DIM ** -0.5),
        grid=(L // tm,),
        in_specs=[row(D), _const_spec((1, D)), _const_spec((D, N)),
                  _const_spec((2, LANES)), _const_spec((2, LANES))],
        out_specs=[row(D), row(D), row(D), row(nqi), row(LANES), row(LANES), row(LANES)],
        out_shape=[jax.ShapeDtypeStruct((L, D), MXU_DTYPE)] * 3
        + [jax.ShapeDtypeStruct((L, nqi), MXU_DTYPE)]
        + [jax.ShapeDtypeStruct((L, LANES), MXU_DTYPE)] * 2
        + [jax.ShapeDtypeStruct((L, LANES), F32)],
        compiler_params=_params("arbitrary"),
        name="dsa_proj",
    )(h, g.reshape(1, D), w_cat, lg2, lb2)


I16 = jnp.int16
I16_MIN = -(2 ** 15)
TIE_ROWS = 16


def _dsa_select_kernel(qi_ref, wi_ref, kia_ref, kib_ref, bias_ref, hi_ref, lo_ref, cand_ref,
                       thr_hi_ref, thr_lo_ref, surplus_ref,
                       *, tq, tk, topk, rb):
    i = pl.program_id(0)
    n_tiles = hi_ref.shape[0]
    n_kt = ((i + 1) * tq + tk - 1) // tk
    q0 = i * tq
    qpos = q0 + lax.broadcasted_iota(jnp.int32, (tq, tk), 0)
    lane_pos = lax.broadcasted_iota(jnp.int32, (tq, tk), 1)

    def score_tile(c, carry):
        k0 = pl.multiple_of(c * tk, tk)
        ka = kia_ref[pl.ds(k0, tk), :]
        kb = kib_ref[pl.ds(k0, tk), :]
        acc = jnp.zeros((tq, tk), F32)
        for p in range(IDX_HEADS // 2):
            lhs = qi_ref[:, p * LANES:(p + 1) * LANES]
            sa = jnp.maximum(_dot_nt(lhs, ka), 0.0)
            sb = jnp.maximum(_dot_nt(lhs, kb), 0.0)
            acc = acc + wi_ref[:, 2 * p:2 * p + 1] * sa
            acc = acc + wi_ref[:, 2 * p + 1:2 * p + 2] * sb
        bits = pltpu.bitcast(acc, jnp.int32)
        key = bits ^ ((bits >> 31) & jnp.int32(0x7FFFFFFF))
        key = jnp.where(k0 + lane_pos <= qpos, key, jnp.int32(INT_MIN))
        hi_ref[c] = (key >> 16).astype(I16)
        lo_ref[c] = ((key & 0xFFFF) + I16_MIN).astype(I16)
        return carry

    lax.fori_loop(0, n_kt, score_tile, 0)

    def count(pred):
        def body(c, acc):
            parts = []
            for r0 in range(0, tq, rb):
                cand = cand_ref[r0:r0 + rb, :]
                part = jnp.zeros((rb, LANES), I16)
                for t in range(tk // LANES):
                    ls = slice(t * LANES, (t + 1) * LANES)
                    hit = pred(hi_ref[c, r0:r0 + rb, ls], lo_ref[c, r0:r0 + rb, ls], cand,
                               c * tk + t * LANES, r0)
                    part = part + hit.astype(I16)
                parts.append(part)
            return acc + jnp.concatenate(parts, axis=0)
        acc = lax.fori_loop(0, n_kt, body, jnp.zeros((tq, LANES), I16))
        return jnp.sum(acc.astype(F32), axis=1, keepdims=True)

    def set_cand(x):
        cand_ref[...] = jnp.broadcast_to(x, (tq, LANES)).astype(I16)

    def bisect16(pred, n_ge0):
        def bit_step(carry):
            b, c_best, n_best = carry
            cand = c_best + lax.shift_left(jnp.int32(1), 15 - b)
            set_cand(cand)
            cnt = count(pred)
            keep = cnt >= topk
            return b + 1, jnp.where(keep, cand, c_best), jnp.where(keep, cnt, n_best)

        def unresolved(carry):
            b, _, n_best = carry
            return jnp.logical_and(b < 16, jnp.max(jnp.abs(n_best - topk)) > 0.0)

        _, c_best, n_best = lax.while_loop(
            unresolved, bit_step, (jnp.int32(0), jnp.full((tq, 1), I16_MIN, jnp.int32), n_ge0))
        return c_best, n_best

    n_all = jnp.broadcast_to((n_kt * tk).astype(F32), (tq, 1))
    thr_hi, n_hi = bisect16(lambda hi, lo, cd, k0, r0: hi >= cd, n_all)
    thr_hi16 = jnp.broadcast_to(thr_hi, (tq, LANES)).astype(I16)

    def clamp_low(c, carry):
        for r0 in range(0, tq, rb):
            hi = hi_ref[c, r0:r0 + rb, :]
            th = jnp.concatenate([thr_hi16[r0:r0 + rb]] * (tk // LANES), axis=1)
            side = jnp.where(hi > th, jnp.asarray(2 ** 15 - 1, I16), jnp.asarray(I16_MIN, I16))
            lo_ref[c, r0:r0 + rb, :] = jnp.where(hi == th, lo_ref[c, r0:r0 + rb, :], side)
        return carry

    lax.fori_loop(0, n_kt, clamp_low, 0)
    thr_lo, n_ge = bisect16(lambda hi, lo, cd, k0, r0: lo >= cd, n_hi)
    thr_lo16 = jnp.broadcast_to(thr_lo, (tq, LANES)).astype(I16)
    is_floor = jnp.logical_and(thr_hi == I16_MIN, thr_lo == I16_MIN)

    cand_ref[...] = jnp.broadcast_to(jnp.where(is_floor, -1, 2 ** 15 - 1), (tq, LANES)).astype(I16)
    thr_hi_ref[...] = thr_hi16
    thr_lo_ref[...] = thr_lo16
    surplus_ref[...] = jnp.broadcast_to(jnp.where(is_floor, 0.0, n_ge - topk), (tq, LANES))
    lane_idx16 = lax.broadcasted_iota(jnp.int32, (TIE_ROWS, LANES), 1)

    def tie_group(gi, carry):
        r0 = pl.multiple_of(gi * TIE_ROWS, TIE_ROWS)
        rows = pl.ds(r0, TIE_ROWS)

        @pl.when(jnp.max(surplus_ref[rows, :]) > 0.0)
        def _ties():
            th = thr_hi_ref[rows, :]
            tl = thr_lo_ref[rows, :]

            def count_rows(pred):
                def body(c, acc):
                    part = jnp.zeros((TIE_ROWS, LANES), I16)
                    for t in range(tk // LANES):
                        ls = slice(t * LANES, (t + 1) * LANES)
                        hit = pred(hi_ref[c, rows, ls], lo_ref[c, rows, ls], c * tk + t * LANES)
                        part = part + hit.astype(I16)
                    return acc + part
                acc = lax.fori_loop(0, n_kt, body, jnp.zeros((TIE_ROWS, LANES), I16))
                return jnp.sum(acc.astype(F32), axis=1, keepdims=True)

            need = topk - count_rows(lambda hi, lo, k0: jnp.logical_or(
                hi > th, jnp.logical_and(hi == th, lo > tl)))

            def idx_step(b, cut):
                cand = cut + lax.shift_left(jnp.int32(1), 14 - b)
                cd = jnp.broadcast_to(cand, (TIE_ROWS, LANES)).astype(I16)
                cnt = count_rows(lambda hi, lo, k0: jnp.logical_and(
                    jnp.logical_and(hi == th, lo == tl), (k0 + lane_idx16).astype(I16) < cd))
                return jnp.where(cnt < need, cand, cut)

            cut = lax.fori_loop(0, 15, idx_step, jnp.zeros((TIE_ROWS, 1), jnp.int32))
            floor = jnp.logical_and(th.astype(jnp.int32) == I16_MIN, tl.astype(jnp.int32) == I16_MIN)
            cand_ref[rows, :] = jnp.where(floor, -1, jnp.broadcast_to(cut, (TIE_ROWS, LANES))).astype(I16)

        return carry

    lax.fori_loop(0, tq // TIE_ROWS, tie_group, 0)

    lane_idx = lax.broadcasted_iota(jnp.int32, (rb, tk), 1)

    def write_tile(c, carry):
        for r0 in range(0, tq, rb):
            hi = hi_ref[c, r0:r0 + rb, :]
            lo = lo_ref[c, r0:r0 + rb, :]
            th = jnp.concatenate([thr_hi16[r0:r0 + rb]] * (tk // LANES), axis=1)
            tl = jnp.concatenate([thr_lo16[r0:r0 + rb]] * (tk // LANES), axis=1)
            cut = jnp.concatenate([cand_ref[r0:r0 + rb, :]] * (tk // LANES), axis=1)
            gt = jnp.logical_or(hi > th, jnp.logical_and(hi == th, lo > tl))
            eq = jnp.logical_and(jnp.logical_and(hi == th, lo == tl),
                                 (c * tk + lane_idx).astype(I16) <= cut)
            keep = jnp.logical_or(gt, eq)
            bias_ref[c, r0:r0 + rb, :] = jnp.where(
                keep, jnp.zeros((), SOFTMAX_DTYPE), jnp.asarray(NEG_BIG, SOFTMAX_DTYPE))
        return carry

    lax.fori_loop(0, n_kt, write_tile, 0)

    def fill_tile(c, carry):
        bias_ref[c] = jnp.full((tq, tk), NEG_BIG, SOFTMAX_DTYPE)
        return carry

    lax.fori_loop(n_kt, n_tiles, fill_tile, 0)


def _dsa_select(qi, kia, kib, wi, topk, tq=256, tk=1024, rb=32):
    L, nqi = qi.shape
    assert L // LANES < 2 ** 15 and L < 2 ** 15
    n_tiles = L // tk
    return pl.pallas_call(
        functools.partial(_dsa_select_kernel, tq=tq, tk=tk, topk=topk, rb=rb),
        grid=(L // tq,),
        in_specs=[
            pl.BlockSpec((tq, nqi), lambda i: (i, 0)),
            pl.BlockSpec((tq, LANES), lambda i: (i, 0)),
            _const_spec((L, LANES)),
            _const_spec((L, LANES)),
        ],
        out_specs=pl.BlockSpec((n_tiles, tq, tk), lambda i: (0, i, 0)),
        out_shape=jax.ShapeDtypeStruct((n_tiles, L, tk), SOFTMAX_DTYPE),
        scratch_shapes=[
            pltpu.VMEM((n_tiles, tq, tk), I16),
            pltpu.VMEM((n_tiles, tq, tk), I16),
            pltpu.VMEM((tq, LANES), I16),
            pltpu.VMEM((tq, LANES), I16),
            pltpu.VMEM((tq, LANES), I16),
            pltpu.VMEM((tq, LANES), F32),
        ],
        compiler_params=_params("arbitrary"),
        name="dsa_select",
    )(qi, wi, kia, kib)


def _dsa_attn_kernel(qi_s, kj_s, q_ref, k_ref, v_ref, bias_ref, w_ref, h_ref, o_ref, m_ref, l_ref, acc_ref,
                     *, tq, tk, nheads):
    step = pl.program_id(0)
    i = qi_s[step]
    j = kj_s[step]
    n_kt = ((i + 1) * tq + tk - 1) // tk

    @pl.when(j == 0)
    def _init():
        m_ref[...] = jnp.full_like(m_ref, NEG_BIG)
        l_ref[...] = jnp.zeros_like(l_ref)
        acc_ref[...] = jnp.zeros_like(acc_ref)

    bias = jnp.concatenate([bias_ref[t] for t in range(bias_ref.shape[0])], axis=1)
    _flash_heads(q_ref, k_ref, v_ref, m_ref, l_ref, acc_ref, bias, nheads)

    @pl.when(j == n_kt - 1)
    def _finish():
        _flash_finish(o_ref, l_ref, acc_ref, nheads, w_ref, h_ref)


def _dsa_attn(q, k, v, bias, w_out, h, tq=512, tk=1024):
    L, D = q.shape
    nheads = DSA_HEADS
    tb = bias.shape[2]
    assert D == nheads * LANES and bias.shape == (L // tb, L, tb) and tk % tb == 0
    qi_s, kj_s = _causal_steps(L // tq, tq, tk)
    qmap = lambda s, qs, ks: (qs[s], 0)
    kmap = lambda s, qs, ks: (ks[s], 0)
    grid_spec = pltpu.PrefetchScalarGridSpec(
        num_scalar_prefetch=2,
        grid=(qi_s.shape[0],),
        in_specs=[
            pl.BlockSpec((tq, D), qmap),
            pl.BlockSpec((tk, D), kmap),
            pl.BlockSpec((tk, D), kmap),
            pl.BlockSpec((tk // tb, tq, tb), lambda s, qs, ks: (ks[s], qs[s], 0)),
            pl.BlockSpec(w_out.shape, lambda s, qs, ks: (0, 0)),
            pl.BlockSpec((tq, D), qmap),
        ],
        out_specs=pl.BlockSpec((tq, D), qmap),
        scratch_shapes=[
            pltpu.VMEM((nheads, tq, LANES), F32),
            pltpu.VMEM((nheads, tq, LANES), F32),
            pltpu.VMEM((tq, D), F32),
        ],
    )
    return pl.pallas_call(
        functools.partial(_dsa_attn_kernel, tq=tq, tk=tk, nheads=nheads),
        grid_spec=grid_spec,
        out_shape=jax.ShapeDtypeStruct((L, D), F32),
        compiler_params=_params("arbitrary"),
        name="dsa_attn",
    )(qi_s, kj_s, q, k, v, bias, w_out, h)


def _dsa_layer(h, g, w_in, idx_k_g, idx_k_b, w_out):
    L, D = h.shape
    nqi = IDX_HEADS * IDX_DIM
    o3 = 3 * D
    o4 = o3 + nqi
    o5 = o4 + IDX_DIM
    w_ki = w_in[:, o4:o5]
    w_wi = w_in[:, o5:]
    zpad = lambda n: jnp.zeros((D, n), w_in.dtype)
    w_cat = jnp.concatenate(
        [w_in[:, :o4],
         w_ki, w_wi, zpad(LANES - IDX_DIM - IDX_HEADS),
         w_wi, zpad(LANES - IDX_DIM - IDX_HEADS), w_ki], axis=1).astype(MXU_DTYPE)
    z = jnp.zeros((IDX_DIM,), F32)
    lg2 = jnp.stack([jnp.concatenate([idx_k_g, z]), jnp.concatenate([z, idx_k_g])])
    lb2 = jnp.stack([jnp.concatenate([idx_k_b, z]), jnp.concatenate([z, idx_k_b])])
    q, k, v, qi, kia, kib, wi = _dsa_proj(h, g, w_cat, lg2, lb2)
    bias = _dsa_select(qi, kia, kib, wi, topk=min(TOPK_MAX, L // 4))
    return _dsa_attn(q, k, v, bias, w_out.astype(MXU_DTYPE), h)


def _glu_proj_kernel(h_ref, g_ref, w_ref, b_ref, u_ref, *, d):
    xn = _rmsnorm_rows(h_ref[...], g_ref[...]).astype(MXU_DTYPE)
    cw = 512
    for c in range(0, d, cw):
        a = jnp.dot(xn, w_ref[:, c:c + cw], preferred_element_type=F32) + b_ref[:, c:c + cw]
        gt = jnp.dot(xn, w_ref[:, d + c:d + c + cw], preferred_element_type=F32) + b_ref[:, d + c:d + c + cw]
        u_ref[:, c:c + cw] = a * _sigmoid(gt)


def _glu_proj(h, g, w, b, tm=512):
    L, D = h.shape
    return pl.pallas_call(
        functools.partial(_glu_proj_kernel, d=D),
        grid=(L // tm,),
        in_specs=[pl.BlockSpec((tm, D), lambda i: (i, 0)), _const_spec((1, D)),
                  _const_spec((D, 2 * D)), _const_spec((1, 2 * D))],
        out_specs=pl.BlockSpec((tm, D), lambda i: (i, 0)),
        out_shape=jax.ShapeDtypeStruct((L, D), F32),
        compiler_params=_params("arbitrary"),
        name="conv_glu_proj",
    )(h, g.reshape(1, D), w, b.reshape(1, 2 * D))


CONV_HALO = 32


def _conv_out_kernel(u_ref, up_ref, wdw_ref, bdw_ref, lg_ref, lb_ref, w2_ref, b2_ref, h_ref, o_ref,
                     x_ref, xs_ref, *, tm):
    i = pl.program_id(0)
    x_ref[CONV_HALO:, :] = u_ref[...]
    x_ref[:CONV_HALO, :] = jnp.where(i > 0, up_ref[...], 0.0)
    off = CONV_HALO - (CONV_WIDTH - 1)
    acc = jnp.zeros(u_ref.shape, F32)
    for r in range(SUBLANES):
        taps = [t for t in range(CONV_WIDTH) if (off + t) % SUBLANES == r]
        if not taps:
            continue
        span = off + taps[-1] - r + tm
        src = x_ref
        if r:
            xs_ref[:span, :] = x_ref[r:r + span, :]
            src = xs_ref
        for t in taps:
            a = off + t - r
            acc = acc + wdw_ref[t:t + 1, :] * src[a:a + tm, :]
    y = acc + bdw_ref[...]
    mu = jnp.mean(y, axis=-1, keepdims=True)
    dlt = y - mu
    var = jnp.mean(dlt * dlt, axis=-1, keepdims=True)
    y = _silu(dlt * lax.rsqrt(var + EPS) * lg_ref[...] + lb_ref[...])
    o_ref[...] = h_ref[...] + _dot(y, w2_ref[...]) + b2_ref[...]


def _conv_out(u, w_dw, b_dw, ln_g, ln_b, w2, b2, h, tm=512):
    L, D = u.shape
    r = tm // CONV_HALO
    vec = lambda a: a.reshape(1, D)
    return pl.pallas_call(
        functools.partial(_conv_out_kernel, tm=tm),
        grid=(L // tm,),
        in_specs=[
            pl.BlockSpec((tm, D), lambda i: (i, 0)),
            pl.BlockSpec((CONV_HALO, D), lambda i: (jnp.maximum(i * r - 1, 0), 0)),
            _const_spec((CONV_WIDTH, D)),
            _const_spec((1, D)), _const_spec((1, D)), _const_spec((1, D)),
            _const_spec((D, D)), _const_spec((1, D)),
            pl.BlockSpec((tm, D), lambda i: (i, 0)),
        ],
        out_specs=pl.BlockSpec((tm, D), lambda i: (i, 0)),
        out_shape=jax.ShapeDtypeStruct((L, D), F32),
        scratch_shapes=[pltpu.VMEM((tm + CONV_HALO, D), F32), pltpu.VMEM((tm + CONV_HALO, D), F32)],
        compiler_params=_params("arbitrary"),
        name="conv_out",
    )(u, u, w_dw, vec(b_dw), vec(ln_g), vec(ln_b), w2, vec(b2), h)


def _conv_layer(h, g, w_pw1, b_pw1, w_dw, b_dw, ln_g, ln_b, w_pw2, b_pw2):
    u = _glu_proj(h, g, w_pw1.astype(MXU_DTYPE), b_pw1)
    return _conv_out(u, w_dw, b_dw, ln_g, ln_b, w_pw2.astype(MXU_DTYPE), b_pw2, h)


def _mla_prep_kernel(h_ref, g_ref, w1_ref, qg_ref, kvg_ref, wqa_ref, wqb_ref, wka_ref, wv_ref,
                     c_ref, s_ref, q_ref, k_ref, v_ref, *, scale):
    xn = _rmsnorm_rows(h_ref[...], g_ref[...]).astype(MXU_DTYPE)
    proj = jnp.dot(xn, w1_ref[...], preferred_element_type=F32)
    o1 = MLA_Q_LORA
    o2 = o1 + MLA_KV_LORA
    cq = _rmsnorm_rows(proj[:, :o1], qg_ref[...]).astype(MXU_DTYPE)
    ckv = _rmsnorm_rows(proj[:, o1:o2], kvg_ref[...]).astype(MXU_DTYPE)
    cos = c_ref[...]
    sin = s_ref[...]
    kr = proj[:, o2:o2 + LANES] * cos + proj[:, o2 + LANES:o2 + 2 * LANES] * sin
    for hd in range(MLA_HEADS):
        cs = slice(hd * LANES, (hd + 1) * LANES)
        qa = jnp.dot(cq, wqa_ref[:, cs], preferred_element_type=F32)
        qb = jnp.dot(cq, wqb_ref[:, cs], preferred_element_type=F32)
        q_ref[:, cs] = ((qa * cos + qb * sin) * scale).astype(q_ref.dtype)
        ka = jnp.dot(ckv, wka_ref[:, cs], preferred_element_type=F32)
        k_ref[:, cs] = (ka + kr).astype(k_ref.dtype)
        va = jnp.dot(ckv, wv_ref[:, cs], preferred_element_type=F32)
        lane = lax.broadcasted_iota(jnp.int32, va.shape, 1)
        v_ref[:, cs] = jnp.where(lane < MLA_V, va, 1.0).astype(v_ref.dtype)


def _mla_prep(h, g, w1, qg, kvg, wqa, wqb, wka, wv, ctab, stab, tm=512):
    L, D = h.shape
    HP = MLA_HEADS * LANES
    row = lambda n: pl.BlockSpec((tm, n), lambda i: (i, 0))
    full = lambda a: _const_spec(a.shape)
    return pl.pallas_call(
        functools.partial(_mla_prep_kernel, scale=(MLA_NOPE + MLA_ROPE) ** -0.5 * LOG2E),
        grid=(L // tm,),
        in_specs=[row(D), _const_spec((1, D)), full(w1), _const_spec((1, MLA_Q_LORA)),
                  _const_spec((1, MLA_KV_LORA)), full(wqa), full(wqb), full(wka), full(wv),
                  row(LANES), row(LANES)],
        out_specs=[row(HP), row(HP), row(HP)],
        out_shape=[jax.ShapeDtypeStruct((L, HP), MXU_DTYPE)] * 3,
        compiler_params=_params("arbitrary"),
        name="mla_prep",
    )(h, g.reshape(1, D), w1, qg.reshape(1, -1), kvg.reshape(1, -1), wqa, wqb, wka, wv, ctab, stab)


def _mla_attn_kernel(qi_s, kj_s, q_ref, k_ref, v_ref, w_ref, h_ref, o_ref, m_ref, acc_ref, *, tq, tk, nheads):
    step = pl.program_id(0)
    i = qi_s[step]
    j = kj_s[step]
    n_kt = ((i + 1) * tq + tk - 1) // tk

    @pl.when(j == 0)
    def _init():
        m_ref[...] = jnp.full_like(m_ref, NEG_BIG)
        acc_ref[...] = jnp.zeros_like(acc_ref)

    qpos = i * tq + lax.broadcasted_iota(jnp.int32, (tq, tk), 0)
    kpos = j * tk + lax.broadcasted_iota(jnp.int32, (tq, tk), 1)
    bias = jnp.where(kpos <= qpos, 0.0, NEG_BIG).astype(SOFTMAX_DTYPE)
    _flash_heads(q_ref, k_ref, v_ref, m_ref, None, acc_ref, bias, nheads)

    @pl.when(j == n_kt - 1)
    def _finish():
        _flash_finish(o_ref, None, acc_ref, nheads, w_ref, h_ref)


def _mla_attn(q, k, v, w_out, h, tq=512, tk=1024):
    L, HP = q.shape
    D = h.shape[1]
    qi_s, kj_s = _causal_steps(L // tq, tq, tk)
    qmap = lambda s, qs, ks: (qs[s], 0)
    kmap = lambda s, qs, ks: (ks[s], 0)
    grid_spec = pltpu.PrefetchScalarGridSpec(
        num_scalar_prefetch=2,
        grid=(qi_s.shape[0],),
        in_specs=[pl.BlockSpec((tq, HP), qmap),
                  pl.BlockSpec((tk, HP), kmap), pl.BlockSpec((tk, HP), kmap),
                  pl.BlockSpec(w_out.shape, lambda s, qs, ks: (0, 0)),
                  pl.BlockSpec((tq, D), qmap)],
        out_specs=pl.BlockSpec((tq, D), qmap),
        scratch_shapes=[pltpu.VMEM((MLA_HEADS, tq, LANES), F32),
                        pltpu.VMEM((tq, HP), F32)],
    )
    return pl.pallas_call(
        functools.partial(_mla_attn_kernel, tq=tq, tk=tk, nheads=MLA_HEADS),
        grid_spec=grid_spec,
        out_shape=jax.ShapeDtypeStruct((L, D), F32),
        compiler_params=_params("arbitrary"),
        name="mla_attn",
    )(qi_s, kj_s, q, k, v, w_out, h)


def _place_heads(w, nheads, src_w, lane0, swap_half=0):
    K = w.shape[0]
    w3 = w.reshape(K, nheads, src_w)
    if swap_half:
        w3 = jnp.concatenate([w3[..., swap_half:], w3[..., :swap_half]], axis=-1)
    out = jnp.zeros((K, nheads, LANES), w.dtype)
    out = out.at[:, :, lane0:lane0 + src_w].set(w3)
    return out.reshape(K, nheads * LANES)


def _mla_layer(h, g, w_in, q_norm_g, w_uq, kv_norm_g, w_ukv, w_out):
    L, D = h.shape
    half = MLA_ROPE // 2
    o2 = MLA_Q_LORA + MLA_KV_LORA
    w_kr = w_in[:, o2:]
    w1 = jnp.concatenate(
        [w_in[:, :o2], _place_heads(w_kr, 1, MLA_ROPE, MLA_NOPE),
         _place_heads(w_kr, 1, MLA_ROPE, MLA_NOPE, swap_half=half)], axis=1).astype(MXU_DTYPE)
    uq = w_uq.reshape(MLA_Q_LORA, MLA_HEADS, MLA_NOPE + MLA_ROPE)
    uq_rope = uq[..., MLA_NOPE:].reshape(MLA_Q_LORA, MLA_HEADS * MLA_ROPE)
    wqa = _place_heads(w_uq, MLA_HEADS, MLA_NOPE + MLA_ROPE, 0).astype(MXU_DTYPE)
    wqb = _place_heads(uq_rope, MLA_HEADS, MLA_ROPE, MLA_NOPE, swap_half=half).astype(MXU_DTYPE)
    ukv = w_ukv.reshape(MLA_KV_LORA, MLA_HEADS, MLA_NOPE + MLA_V)
    wka = _place_heads(ukv[..., :MLA_NOPE].reshape(MLA_KV_LORA, -1), MLA_HEADS, MLA_NOPE, 0).astype(MXU_DTYPE)
    wv = _place_heads(ukv[..., MLA_NOPE:].reshape(MLA_KV_LORA, -1), MLA_HEADS, MLA_V, 0).astype(MXU_DTYPE)
    pos = jnp.arange(L, dtype=F32)
    inv_freq = ROPE_THETA ** (-jnp.arange(0, MLA_ROPE, 2, dtype=F32) / MLA_ROPE)
    ang = pos[:, None] * inv_freq[None, :]
    cos, sin = jnp.cos(ang), jnp.sin(ang)
    zeros = jnp.zeros((L, LANES - MLA_NOPE - MLA_ROPE), F32)
    ctab = jnp.concatenate([jnp.ones((L, MLA_NOPE), F32), cos, cos, zeros], axis=1)
    stab = jnp.concatenate([jnp.zeros((L, MLA_NOPE), F32), -sin, sin, zeros], axis=1)
    q, k, v = _mla_prep(h, g, w1, q_norm_g, kv_norm_g, wqa, wqb, wka, wv, ctab, stab)
    wo = w_out.reshape(MLA_HEADS, MLA_V, D)
    wo_pad = jnp.concatenate([wo, jnp.zeros_like(wo)], axis=1).reshape(MLA_HEADS * LANES, D).astype(MXU_DTYPE)
    return _mla_attn(q, k, v, wo_pad, h)


def _gdn_scan_kernel(h_ref, g_ref, w_ref, cw_ref, alog_ref, dtb_ref, o_ref, gate_ref, x_ref, s_ref,
                     *, nheads, dk, dv, c, nch):
    n = pl.program_id(0)
    halo = 8
    nqk = nheads * dk

    @pl.when(n == 0)
    def _init():
        x_ref[:halo, :] = jnp.zeros((halo, x_ref.shape[1]), F32)
        s_ref[...] = jnp.zeros_like(s_ref)

    rows = nch * c
    nqkv = x_ref.shape[1]
    nv = nheads * dv
    xn = _rmsnorm_rows(h_ref[...], g_ref[...]).astype(MXU_DTYPE)
    cw_cols = 512
    for c0 in range(0, nqkv, cw_cols):
        x_ref[halo:, c0:c0 + cw_cols] = jnp.dot(xn, w_ref[:, c0:c0 + cw_cols], preferred_element_type=F32)
    gate_ref[...] = jnp.dot(xn, w_ref[:, nqkv:nqkv + nv], preferred_element_type=F32)
    ba = jnp.dot(xn, w_ref[:, nqkv + nv:], preferred_element_type=F32)
    off = halo - (GDN_CONV - 1)
    acc = jnp.zeros((rows, nqkv), F32)
    for t in range(GDN_CONV):
        acc = acc + cw_ref[t:t + 1, :] * x_ref[off + t:off + t + rows, :]
    x_ref[:halo, :] = x_ref[rows:rows + halo, :]
    qkv = _silu(acc)

    beta_all = _sigmoid(ba[:, :LANES])
    a_raw = ba[:, LANES:] + dtb_ref[...]
    softplus = jnp.maximum(a_raw, 0.0) + jnp.log(1.0 + jnp.exp(-jnp.abs(a_raw)))
    g_all = -jnp.exp(alog_ref[...]) * softplus
    row = lax.broadcasted_iota(jnp.int32, (c, c), 0)
    col = lax.broadcasted_iota(jnp.int32, (c, c), 1)
    tril = row >= col
    tril_strict = row > col
    eye = jnp.where(row == col, 1.0, 0.0)

    units = [(ch, hd) for ch in range(nch) for hd in range(nheads)]
    gc_all = [jnp.dot(tril.astype(F32), g_all[ch * c:(ch + 1) * c], preferred_element_type=F32,
                      precision=lax.Precision.HIGHEST) for ch in range(nch)]
    gc_t = [x.T for x in gc_all]

    def chunk_rows(x, ch):
        return x[ch * c:(ch + 1) * c]

    q = [chunk_rows(qkv[:, hd * dk:(hd + 1) * dk], ch) for ch, hd in units]
    k = [chunk_rows(qkv[:, nqk + hd * dk:nqk + (hd + 1) * dk], ch) for ch, hd in units]
    v = [chunk_rows(qkv[:, 2 * nqk + hd * dv:2 * nqk + (hd + 1) * dv], ch) for ch, hd in units]
    q = [x * lax.rsqrt(jnp.sum(x * x, axis=-1, keepdims=True) + EPS) * (dk ** -0.5) for x in q]
    k = [x * lax.rsqrt(jnp.sum(x * x, axis=-1, keepdims=True) + EPS) for x in k]
    beta = [chunk_rows(beta_all[:, hd:hd + 1], ch) for ch, hd in units]
    gc = [gc_all[ch][:, hd:hd + 1] for ch, hd in units]
    gc_last = [gc_all[ch][c - 1:c, hd:hd + 1] for ch, hd in units]
    n_u = range(len(units))
    decay = [jnp.where(tril, jnp.exp(jnp.where(tril, gc[u] - gc_t[units[u][0]][units[u][1]:units[u][1] + 1, :],
                                               0.0)), 0.0) for u in n_u]
    kk = [_dot_nt(k[u], k[u]) for u in n_u]
    qk = [_dot_nt(q[u], k[u]) for u in n_u]
    a_mat = [jnp.where(tril_strict, beta[u] * kk[u] * decay[u], 0.0) for u in n_u]
    t_inv = [eye - jnp.where((row >> 1) == (col >> 1), a_mat[u], 0.0) for u in n_u]
    for lvl in range(1, int(math.log2(c))):
        blk_mask = jnp.logical_and((row >> (lvl + 1)) == (col >> (lvl + 1)), (row >> lvl) != (col >> lvl))
        tm = [_dot(t_inv[u], jnp.where(blk_mask, a_mat[u], 0.0)) for u in n_u]
        tmt = [_dot(tm[u], t_inv[u]) for u in n_u]
        t_inv = [t_inv[u] - tmt[u] for u in n_u]
    e_gc = [jnp.exp(gc[u]) for u in n_u]
    uu = [_dot(t_inv[u], v[u] * beta[u]) for u in n_u]
    w = [_dot(t_inv[u], k[u] * (beta[u] * e_gc[u])) for u in n_u]
    q_dec = [q[u] * e_gc[u] for u in n_u]
    k_dec_t = [(k[u] * jnp.exp(gc_last[u] - gc[u])).T for u in n_u]
    qk_dec = [qk[u] * decay[u] for u in n_u]
    g_last = [jnp.exp(gc_last[u]) for u in n_u]

    s = [s_ref[hd] for hd in range(nheads)]
    for ch in range(nch):
        us = [ch * nheads + hd for hd in range(nheads)]
        ws = [_dot(w[u], s[hd]) for hd, u in enumerate(us)]
        qs = [_dot(q_dec[u], s[hd]) for hd, u in enumerate(us)]
        v_new = [uu[u] - ws[hd] for hd, u in enumerate(us)]
        o_intra = [_dot(qk_dec[u], v_new[hd]) for hd, u in enumerate(us)]
        s_upd = [_dot(k_dec_t[u], v_new[hd]) for hd, u in enumerate(us)]
        for hd, u in enumerate(us):
            o_ref[ch * c:(ch + 1) * c, hd * dv:(hd + 1) * dv] = qs[hd] + o_intra[hd]
        s = [s[hd] * g_last[u] + s_upd[hd] for hd, u in enumerate(us)]
    for hd in range(nheads):
        s_ref[hd] = s[hd]


def _gdn_scan(h, g, w_cat, conv_w, a_log_pad, dt_bias_pad, nch=4):
    L, D = h.shape
    c = GDN_CHUNK
    rows = nch * c
    nqkv = 2 * GDN_HEADS * GDN_DK + GDN_HEADS * GDN_DV
    nv = GDN_HEADS * GDN_DV
    assert w_cat.shape == (D, nqkv + nv + 2 * LANES)
    return pl.pallas_call(
        functools.partial(_gdn_scan_kernel, nheads=GDN_HEADS, dk=GDN_DK, dv=GDN_DV, c=c, nch=nch),
        grid=(L // rows,),
        in_specs=[pl.BlockSpec((rows, D), lambda n: (n, 0)), _const_spec((1, D)),
                  _const_spec(w_cat.shape),
                  _const_spec((GDN_CONV, nqkv)), _const_spec((1, LANES)), _const_spec((1, LANES))],
        out_specs=[pl.BlockSpec((rows, nv), lambda n: (n, 0)), pl.BlockSpec((rows, nv), lambda n: (n, 0))],
        out_shape=[jax.ShapeDtypeStruct((L, nv), F32), jax.ShapeDtypeStruct((L, nv), F32)],
        scratch_shapes=[pltpu.VMEM((rows + 8, nqkv), F32),
                        pltpu.VMEM((GDN_HEADS, GDN_DK, GDN_DV), F32)],
        compiler_params=_params("arbitrary"),
        name="gdn_scan",
    )(h, g.reshape(1, D), w_cat, conv_w, a_log_pad, dt_bias_pad)


def _gdn_out_kernel(o_ref, gate_ref, og_ref, w_ref, h_ref, y_ref, *, nheads, dv):
    parts = []
    for hd in range(nheads):
        cs = slice(hd * dv, (hd + 1) * dv)
        on = _rmsnorm_rows(o_ref[:, cs], og_ref[...])
        parts.append((on * _silu(gate_ref[:, cs])).astype(MXU_DTYPE))
    y_ref[...] = h_ref[...] + jnp.dot(jnp.concatenate(parts, axis=1), w_ref[...],
                                      preferred_element_type=F32)


def _gdn_out(o, gate, o_norm_g, w_out, h, tm=512):
    L, D = h.shape
    nv = GDN_HEADS * GDN_DV
    return pl.pallas_call(
        functools.partial(_gdn_out_kernel, nheads=GDN_HEADS, dv=GDN_DV),
        grid=(L // tm,),
        in_specs=[pl.BlockSpec((tm, nv), lambda i: (i, 0)),
                  pl.BlockSpec((tm, nv), lambda i: (i, 0)),
                  _const_spec((1, GDN_DV)), _const_spec((nv, D)),
                  pl.BlockSpec((tm, D), lambda i: (i, 0))],
        out_specs=pl.BlockSpec((tm, D), lambda i: (i, 0)),
        out_shape=jax.ShapeDtypeStruct((L, D), F32),
        compiler_params=_params("arbitrary"),
        name="gdn_out",
    )(o, gate, o_norm_g.reshape(1, GDN_DV), w_out, h)


def _gdn_layer(h, g, w_in, conv_w, a_log, dt_bias, o_norm_g, w_out):
    L, D = h.shape
    nmain = 2 * GDN_HEADS * GDN_DK + 2 * GDN_HEADS * GDN_DV
    zpad = jnp.zeros((D, LANES - GDN_HEADS), w_in.dtype)
    w_cat = jnp.concatenate([w_in[:, :nmain], w_in[:, nmain:nmain + GDN_HEADS], zpad,
                             w_in[:, nmain + GDN_HEADS:], zpad], axis=1).astype(MXU_DTYPE)
    vpad = lambda a: jnp.concatenate([a.astype(F32), jnp.zeros((LANES - GDN_HEADS,), F32)]).reshape(1, LANES)
    o, gate = _gdn_scan(h, g, w_cat, conv_w, vpad(a_log), vpad(dt_bias))
    return _gdn_out(o, gate, o_norm_g, w_out.astype(MXU_DTYPE), h)


def kernel(x, norm_mix_g, norm_mlp_g, final_g, mlp_w1, mlp_w2, dsa_w_in, dsa_idx_k_g, dsa_idx_k_b, dsa_w_out, conv_w_pw1, conv_b_pw1, conv_w_dw, conv_b_dw, conv_ln_g, conv_ln_b, conv_w_pw2, conv_b_pw2, mla_w_in, mla_q_norm_g, mla_w_uq, mla_kv_norm_g, mla_w_ukv, mla_w_out, gdn_w_in, gdn_conv_w, gdn_a_log, gdn_dt_bias, gdn_o_norm_g, gdn_w_out):
    b, L, D = x.shape
    depth = norm_mix_g.shape[0]
    outs = []
    rows = x.reshape(b * L, D)
    for bi in range(b):
        h = rows if b == 1 else rows[bi * L:(bi + 1) * L]
        for i in range(depth):
            m = i % 4
            jl = i // 4
            g = norm_mix_g[i]
            if m == 0:
                h = _dsa_layer(h, g, dsa_w_in[jl], dsa_idx_k_g[jl], dsa_idx_k_b[jl], dsa_w_out[jl])
            elif m == 1:
                h = _conv_layer(h, g, conv_w_pw1[jl], conv_b_pw1[jl], conv_w_dw[jl], conv_b_dw[jl],
                                conv_ln_g[jl], conv_ln_b[jl], conv_w_pw2[jl], conv_b_pw2[jl])
            elif m == 2:
                h = _mla_layer(h, g, mla_w_in[jl], mla_q_norm_g[jl], mla_w_uq[jl], mla_kv_norm_g[jl],
                               mla_w_ukv[jl], mla_w_out[jl])
            else:
                h = _gdn_layer(h, g, gdn_w_in[jl], gdn_conv_w[jl], gdn_a_log[jl], gdn_dt_bias[jl],
                               gdn_o_norm_g[jl], gdn_w_out[jl])
            h = _mlp(h, norm_mlp_g[i], mlp_w1[i].astype(MXU_DTYPE), mlp_w2[i].astype(MXU_DTYPE),
                     final_g, final_norm=(i == depth - 1))
        outs.append(h)
    return outs[0].reshape(1, L, D) if b == 1 else jnp.stack(outs)
```
